```python
import jax, jax.numpy as jnp
from jax import lax
import numpy as np

D_MODEL = 2048
BATCH = 16
SEQ = 2048
DEPTH = 1

CHUNK = 128
A_GROUPS = 8
A_GROUP_DIM = 128
A_WIDTH = A_GROUPS * A_GROUP_DIM
N_HEADS = 8
HEAD_DIM = 128
N_KV = 2
Q_WIDTH = N_HEADS * HEAD_DIM
KV_WIDTH = N_KV * HEAD_DIM
IDX_HEADS = 16
IDX_DIM = 64
TOPK_MAX = 256
Q_BLOCK = 128
D_FF = 5632
CONV_W = 3
EPS = 1e-6

kernel_name = "hybrid_gated_sgu_dsa_convffn_block"


def _col_sizes():
    return [A_WIDTH, A_WIDTH, Q_WIDTH, KV_WIDTH, KV_WIDTH,
            IDX_HEADS * IDX_DIM, IDX_DIM, IDX_HEADS, D_MODEL, D_MODEL]


def rms_norm(x, g):
    xf = x.astype(jnp.float32)
    y = xf * lax.rsqrt(jnp.mean(xf * xf, axis=-1, keepdims=True) + EPS)
    return (y * g.astype(jnp.float32)).astype(x.dtype)


def chunked_spatial_gating(u, v, v_norm_g, w_spatial, b_spatial):
    B, S, _ = v.shape
    n = S // CHUNK
    v = rms_norm(v, v_norm_g)
    vc = v.reshape(B, n, CHUNK, A_GROUPS, A_GROUP_DIM)
    causal = jnp.tril(jnp.ones((CHUNK, CHUNK), dtype=bool))
    w = jnp.where(causal[None], w_spatial, 0)
    mixed = jnp.einsum('gts,bnsgc->bntgc', w, vc) + b_spatial.T[None, None, :, :, None]
    return u * mixed.reshape(B, S, A_WIDTH)


def dsa_attention(q, k, v, q_idx, k_idx, w_idx):
    B, S = q.shape[0], q.shape[1]
    topk = min(TOPK_MAX, S // 4)
    n_blocks = S // Q_BLOCK
    rep = N_HEADS // N_KV
    key_pos = jnp.arange(S)
    k_idx_f = k_idx.astype(jnp.float32)
    idx_scale = IDX_DIM ** -0.5 * IDX_HEADS ** -0.5
    att_scale = HEAD_DIM ** -0.5

    def block(i):
        start = i * Q_BLOCK
        qb = lax.dynamic_slice_in_dim(q, start, Q_BLOCK, axis=1)
        qib = lax.dynamic_slice_in_dim(q_idx, start, Q_BLOCK, axis=1)
        wib = lax.dynamic_slice_in_dim(w_idx, start, Q_BLOCK, axis=1)
        q_pos = start + jnp.arange(Q_BLOCK)
        causal = key_pos[None, :] <= q_pos[:, None]
        logits = jnp.einsum('bthd,bsd->bths', qib.astype(jnp.float32), k_idx_f)
        score = jnp.einsum('bth,bths->bts', wib.astype(jnp.float32), jax.nn.relu(logits)) * idx_scale
        score = jnp.where(causal[None], score, -jnp.inf)
        top_val, top_idx = lax.top_k(score, topk)
        valid = top_val > -jnp.inf
        ks = jax.vmap(lambda kk, ii: kk[ii])(k, top_idx)
        vs = jax.vmap(lambda vv, ii: vv[ii])(v, top_idx)
        qg = qb.reshape(B, Q_BLOCK, N_KV, rep, HEAD_DIM)
        s = jnp.einsum('btgrd,btkgd->btgrk', qg, ks).astype(jnp.float32) * att_scale
        s = jnp.where(valid[:, :, None, None, :], s, -jnp.inf)
        p = jax.nn.softmax(s, axis=-1)
        o = jnp.einsum('btgrk,btkgd->btgrd', p.astype(vs.dtype), vs)
        return o.reshape(B, Q_BLOCK, Q_WIDTH)

    out = lax.map(block, jnp.arange(n_blocks))
    return out.transpose(1, 0, 2, 3).reshape(B, S, Q_WIDTH)


def causal_depthwise_conv(x, w, b):
    S = x.shape[1]
    xp = jnp.pad(x, ((0, 0), (CONV_W - 1, 0), (0, 0)))
    y = b + w[0] * xp[:, 0:S]
    for j in range(1, CONV_W):
        y = y + w[j] * xp[:, j:j + S]
    return y


def setup_inputs(seed: int = 0) -> dict:
    key = jax.random.key(seed)
    ks = jax.random.split(key, 20)
    n_cols = sum(_col_sizes())
    nrm = lambda k, shape, s: jax.random.normal(k, shape, jnp.float32) * s
    return {
        "x": nrm(ks[0], (BATCH, SEQ, D_MODEL), 1.0),
        "c": nrm(ks[1], (BATCH, D_MODEL), 1.0),
        "w_ada": nrm(ks[2], (D_MODEL, 6 * D_MODEL), 0.5 * D_MODEL ** -0.5),
        "b_ada": nrm(ks[3], (6 * D_MODEL,), 0.01),
        "norm1_g": 1.0 + nrm(ks[4], (D_MODEL,), 0.02),
        "w_in": nrm(ks[5], (D_MODEL, n_cols), D_MODEL ** -0.5),
        "v_norm_g": 1.0 + nrm(ks[6], (A_WIDTH,), 0.02),
        "w_spatial": nrm(ks[7], (A_GROUPS, CHUNK, CHUNK), CHUNK ** -0.5),
        "b_spatial": 1.0 + nrm(ks[8], (A_GROUPS, CHUNK), 0.1),
        "q_norm_g": 1.0 + nrm(ks[9], (HEAD_DIM,), 0.02),
        "k_norm_g": 1.0 + nrm(ks[10], (HEAD_DIM,), 0.02),
        "w_proj_a": nrm(ks[11], (A_WIDTH, D_MODEL), A_WIDTH ** -0.5),
        "w_proj_b": nrm(ks[12], (Q_WIDTH, D_MODEL), Q_WIDTH ** -0.5),
        "w_out": nrm(ks[13], (D_MODEL, D_MODEL), D_MODEL ** -0.5),
        "norm2_g": 1.0 + nrm(ks[14], (D_MODEL,), 0.02),
        "w_up": nrm(ks[15], (D_MODEL, 2 * D_FF), D_MODEL ** -0.5),
        "conv_w": nrm(ks[16], (CONV_W, 2 * D_FF), CONV_W ** -0.5),
        "conv_b": nrm(ks[17], (2 * D_FF,), 0.01),
        "w_down": nrm(ks[18], (D_FF, D_MODEL), D_FF ** -0.5),
    }


def reference(x, c, w_ada, b_ada, norm1_g, w_in, v_norm_g, w_spatial, b_spatial,
              q_norm_g, k_norm_g, w_proj_a, w_proj_b, w_out, norm2_g, w_up,
              conv_w, conv_b, w_down):
    B, S, _ = x.shape
    mod = (jax.nn.silu(c) @ w_ada + b_ada)[:, None, :]
    sh1, sc1, g1, sh2, sc2, g2 = jnp.split(mod, 6, axis=-1)
    split_pts = []
    acc = 0
    for n in _col_sizes()[:-1]:
        acc += n
        split_pts.append(acc)
    for _ in range(DEPTH):
        h = rms_norm(x, norm1_g) * (1.0 + sc1) + sh1
        proj = h @ w_in
        u, va, q, k, vb, qi, ki, wi, ga, gb = jnp.split(proj, split_pts, axis=-1)
        ya = chunked_spatial_gating(jax.nn.gelu(u), jax.nn.gelu(va), v_norm_g, w_spatial, b_spatial)
        qh = rms_norm(q.reshape(B, S, N_HEADS, HEAD_DIM), q_norm_g)
        kh = rms_norm(k.reshape(B, S, N_KV, HEAD_DIM), k_norm_g)
        vh = vb.reshape(B, S, N_KV, HEAD_DIM)
        yb = dsa_attention(qh, kh, vh, qi.reshape(B, S, IDX_HEADS, IDX_DIM), ki, wi)
        merged = jax.nn.sigmoid(ga) * (ya @ w_proj_a) + jax.nn.sigmoid(gb) * (yb @ w_proj_b)
        x = x + g1 * (merged @ w_out)
        h2 = rms_norm(x, norm2_g) * (1.0 + sc2) + sh2
        up = causal_depthwise_conv(h2 @ w_up, conv_w, conv_b)
        a, bval = jnp.split(up, 2, axis=-1)
        x = x + g2 * ((jax.nn.silu(a) * bval) @ w_down)
    return x
```

```python
import functools
import math

import jax
import jax.numpy as jnp
from jax import lax
from jax.experimental import pallas as pl
from jax.experimental.pallas import tpu as pltpu

F32 = jnp.float32
BF16 = jnp.bfloat16

D_MODEL = 2048
CHUNK = 128
A_GROUPS = 8
A_GROUP_DIM = 128
A_WIDTH = A_GROUPS * A_GROUP_DIM
N_HEADS = 8
HEAD_DIM = 128
N_KV = 2
Q_WIDTH = N_HEADS * HEAD_DIM
KV_WIDTH = N_KV * HEAD_DIM
IDX_HEADS = 16
IDX_DIM = 64
TOPK_MAX = 256
D_FF = 5632
CONV_W = 3
EPS = 1e-6

LANES = 128
VMEM_LIMIT = 56 * 1024 * 1024

PROJ_TN = 1024
COL_GA, COL_GB, COL_U, COL_VA, COL_Q, COL_MISC, COL_QI = 0, 2048, 4096, 5120, 6144, 7168, 8192
PROJ_COLS = 9216
MISC_K, MISC_V, MISC_KI_EVEN, MISC_KI_ODD, MISC_WI = 0, 256, 512, 640, 768

ATT_T = 128
ATT_CH = 256
NEG_BIG = -1e30
INT_MIN = -(2 ** 31)
KEY_NEG_INF = -2139095041


def _dot(a, b):
    return jnp.dot(a, b, preferred_element_type=F32)


def _dot_nt(a, b):
    return lax.dot_general(a, b, (((1,), (1,)), ((), ())), preferred_element_type=F32)


def _ada_kernel(c_ref, w_ref, b_ref, o_ref):
    cs = jax.nn.silu(c_ref[...]).astype(BF16)
    o_ref[...] = _dot(cs, w_ref[...].astype(BF16)) + b_ref[...]


def _ada(c, w_ada, b_ada):
    bsz = c.shape[0]
    n = w_ada.shape[1]
    tn = 1024
    return pl.pallas_call(
        _ada_kernel,
        out_shape=jax.ShapeDtypeStruct((bsz, n), F32),
        grid=(n // tn,),
        in_specs=[
            pl.BlockSpec((bsz, D_MODEL), lambda j: (0, 0)),
            pl.BlockSpec((D_MODEL, tn), lambda j: (0, j)),
            pl.BlockSpec((1, tn), lambda j: (0, j)),
        ],
        out_specs=pl.BlockSpec((bsz, tn), lambda j: (0, j)),
        compiler_params=pltpu.CompilerParams(
            dimension_semantics=("arbitrary",), vmem_limit_bytes=VMEM_LIMIT),
        name="ada_mod",
    )(c, w_ada, b_ada.reshape(1, n))


def _head_rms(x, g, scale):
    r = lax.rsqrt(jnp.mean(x * x, axis=-1, keepdims=True) + EPS)
    y = x * r * g
    return y if scale is None else y * scale


def _proj_kernel(x_ref, sc_ref, sh_ref, g_ref, w_ref, vg_ref, qg_ref, kg_ref, o_ref, h_ref):
    j = pl.program_id(1)

    @pl.when(j == 0)
    def _():
        x = x_ref[...]
        r = lax.rsqrt(jnp.mean(x * x, axis=-1, keepdims=True) + EPS)
        h = (x * r * g_ref[...]) * (1.0 + sc_ref[...]) + sh_ref[...]
        h_ref[...] = h.astype(BF16)

    acc = _dot(h_ref[...], w_ref[...])

    @pl.when(j < COL_U // PROJ_TN)
    def _():
        o_ref[...] = jax.nn.sigmoid(acc).astype(BF16)

    @pl.when(j == COL_U // PROJ_TN)
    def _():
        o_ref[...] = jax.nn.gelu(acc).astype(BF16)

    @pl.when(j == COL_VA // PROJ_TN)
    def _():
        v = jax.nn.gelu(acc)
        r = lax.rsqrt(jnp.mean(v * v, axis=-1, keepdims=True) + EPS)
        o_ref[...] = (v * r * vg_ref[...]).astype(BF16)

    @pl.when(j == COL_Q // PROJ_TN)
    def _():
        qscale = (HEAD_DIM ** -0.5) * math.log2(math.e)
        for h in range(N_HEADS):
            sl = slice(h * HEAD_DIM, (h + 1) * HEAD_DIM)
            o_ref[:, sl] = _head_rms(acc[:, sl], qg_ref[...], qscale).astype(BF16)

    @pl.when(j == COL_MISC // PROJ_TN)
    def _():
        for h in range(N_KV):
            sl = slice(MISC_K + h * HEAD_DIM, MISC_K + (h + 1) * HEAD_DIM)
            o_ref[:, sl] = _head_rms(acc[:, sl], kg_ref[...], None).astype(BF16)
        o_ref[:, MISC_V:] = acc[:, MISC_V:].astype(BF16)

    @pl.when(j == COL_QI // PROJ_TN)
    def _():
        o_ref[...] = acc.astype(BF16)


def _proj(x2, sc1, sh1, norm1_g, w_cat, v_norm_g, q_norm_g, k_norm_g, seq):
    n = x2.shape[0]
    tm = 512
    tiles_per_batch = seq // tm
    mod_spec = pl.BlockSpec((None, 1, D_MODEL), lambda i, j: (i // tiles_per_batch, 0, 0))
    return pl.pallas_call(
        _proj_kernel,
        out_shape=jax.ShapeDtypeStruct((n, PROJ_COLS), BF16),
        grid=(n // tm, PROJ_COLS // PROJ_TN),
        in_specs=[
            pl.BlockSpec((tm, D_MODEL), lambda i, j: (i, 0)),
            mod_spec, mod_spec,
            pl.BlockSpec((1, D_MODEL), lambda i, j: (0, 0)),
            pl.BlockSpec((D_MODEL, PROJ_TN), lambda i, j: (0, j)),
            pl.BlockSpec((1, A_WIDTH), lambda i, j: (0, 0)),
            pl.BlockSpec((1, HEAD_DIM), lambda i, j: (0, 0)),
            pl.BlockSpec((1, HEAD_DIM), lambda i, j: (0, 0)),
        ],
        out_specs=pl.BlockSpec((tm, PROJ_TN), lambda i, j: (i, j)),
        scratch_shapes=[pltpu.VMEM((tm, D_MODEL), BF16)],
        compiler_params=pltpu.CompilerParams(
            dimension_semantics=("parallel", "arbitrary"), vmem_limit_bytes=VMEM_LIMIT),
        name="proj_in",
    )(x2, sc1, sh1, norm1_g.reshape(1, D_MODEL), w_cat, v_norm_g.reshape(1, A_WIDTH),
      q_norm_g.reshape(1, HEAD_DIM), k_norm_g.reshape(1, HEAD_DIM))


def _attn_kernel(q_ref, misc_ref, wi_ref, qi_ref, o_ref, vt_ref, keys_ref, bias_ref, *, seq, topk):
    i = pl.program_id(1)
    n_chunks = seq // ATT_CH

    @pl.when(i == 0)
    def _():
        for g in range(N_KV):
            for c in range(n_chunks):
                vg = misc_ref[c * ATT_CH:(c + 1) * ATT_CH, MISC_V + g * HEAD_DIM:MISC_V + (g + 1) * HEAD_DIM]
                vt_ref[g, c] = vg.astype(F32).T.astype(BF16)

    idx_scale = IDX_DIM ** -0.5 * IDX_HEADS ** -0.5
    w_t = wi_ref[...].astype(F32).T * idx_scale
    rhs = [jnp.concatenate([qi_ref[:, 256 * r:256 * r + 128], qi_ref[:, 256 * r + 128:256 * r + 256]], axis=0)
           for r in range(IDX_HEADS // 4)]
    q_pos = i * ATT_T + lax.broadcasted_iota(jnp.int32, (ATT_CH, ATT_T), 1)
    row_iota = lax.broadcasted_iota(jnp.int32, (ATT_CH, ATT_T), 0)

    def idx_body(c, carry):
        r0 = pl.multiple_of(c * ATT_CH, ATT_CH)
        k_even = misc_ref[pl.ds(r0, ATT_CH), MISC_KI_EVEN:MISC_KI_EVEN + LANES]
        k_odd = misc_ref[pl.ds(r0, ATT_CH), MISC_KI_ODD:MISC_KI_ODD + LANES]
        acc = jnp.zeros((ATT_CH, ATT_T), F32)
        for r in range(IDX_HEADS // 4):
            l_even = _dot_nt(k_even, rhs[r])
            l_odd = _dot_nt(k_odd, rhs[r])
            acc = acc + w_t[4 * r:4 * r + 1] * jnp.maximum(l_even[:, :ATT_T], 0.0)
            acc = acc + w_t[4 * r + 1:4 * r + 2] * jnp.maximum(l_odd[:, :ATT_T], 0.0)
            acc = acc + w_t[4 * r + 2:4 * r + 3] * jnp.maximum(l_even[:, ATT_T:], 0.0)
            acc = acc + w_t[4 * r + 3:4 * r + 4] * jnp.maximum(l_odd[:, ATT_T:], 0.0)
        score = jnp.where(r0 + row_iota <= q_pos, acc, -jnp.inf)
        bits = lax.bitcast_convert_type(score, jnp.int32)
        keys_ref[pl.ds(r0, ATT_CH), :] = bits ^ ((bits >> 31) & 0x7FFFFFFF)
        return carry

    lax.fori_loop(0, n_chunks, idx_body, 0)

    def bs_body(it, thr):
        cand = thr | lax.shift_left(jnp.int32(1), 31 - it)
        cand_s = cand ^ INT_MIN
        cnt = jnp.sum(jnp.where(keys_ref[...] >= cand_s, 1.0, 0.0), axis=0, keepdims=True)
        return jnp.where(cnt >= float(topk), cand, thr)

    thr = lax.fori_loop(0, 32, bs_body, jnp.zeros((1, ATT_T), jnp.int32))
    thr_s = jnp.maximum(thr ^ INT_MIN, KEY_NEG_INF + 1)
    bias_ref[...] = jnp.where(keys_ref[...] >= thr_s, 0.0, NEG_BIG)

    for p in range(N_HEADS // 2):
        g = (2 * p) // (N_HEADS // N_KV)
        q_pair = jnp.concatenate([q_ref[:, 256 * p:256 * p + 128], q_ref[:, 256 * p + 128:256 * p + 256]], axis=0)

        def att_body(c, carry, g=g, q_pair=q_pair):
            m, l, acc = carry
            r0 = pl.multiple_of(c * ATT_CH, ATT_CH)
            kc = misc_ref[pl.ds(r0, ATT_CH), MISC_K + g * HEAD_DIM:MISC_K + (g + 1) * HEAD_DIM]
            bias = bias_ref[pl.ds(r0, ATT_CH), :]
            s = _dot_nt(kc, q_pair) + jnp.concatenate([bias, bias], axis=1)
            m_new = jnp.maximum(m, jnp.max(s, axis=0, keepdims=True))
            alpha = jnp.exp2(m - m_new)
            pm = jnp.exp2(s - m_new)
            l = alpha * l + jnp.sum(pm, axis=0, keepdims=True)
            pv = _dot(vt_ref[g, c], pm.astype(BF16))
            return m_new, l, alpha * acc + pv

        m0 = jnp.full((1, 2 * ATT_T), NEG_BIG, F32)
        l0 = jnp.zeros((1, 2 * ATT_T), F32)
        a0 = jnp.zeros((HEAD_DIM, 2 * ATT_T), F32)
        _, l, acc = lax.fori_loop(0, n_chunks, att_body, (m0, l0, a0))
        out = acc / l
        o_ref[:, 256 * p:256 * p + 128] = out[:, :ATT_T].T.astype(BF16)
        o_ref[:, 256 * p + 128:256 * p + 256] = out[:, ATT_T:].T.astype(BF16)


def _attention(proj, bsz, seq):
    n = proj.shape[0]
    nblk = seq // ATT_T
    topk = min(TOPK_MAX, seq // 4)
    kern = functools.partial(_attn_kernel, seq=seq, topk=topk)
    return pl.pallas_call(
        kern,
        out_shape=jax.ShapeDtypeStruct((n, Q_WIDTH), BF16),
        grid=(bsz, nblk),
        in_specs=[
            pl.BlockSpec((ATT_T, Q_WIDTH), lambda b, i: (b * nblk + i, COL_Q // Q_WIDTH)),
            pl.BlockSpec((seq, PROJ_TN), lambda b, i: (b, COL_MISC // PROJ_TN)),
            pl.BlockSpec((ATT_T, LANES), lambda b, i: (b * nblk + i, (COL_MISC + MISC_WI) // LANES)),
            pl.BlockSpec((ATT_T, IDX_HEADS * IDX_DIM), lambda b, i: (b * nblk + i, COL_QI // (IDX_HEADS * IDX_DIM))),
        ],
        out_specs=pl.BlockSpec((ATT_T, Q_WIDTH), lambda b, i: (b * nblk + i, 0)),
        scratch_shapes=[
            pltpu.VMEM((N_KV, seq // ATT_CH, HEAD_DIM, ATT_CH), BF16),
            pltpu.VMEM((seq, ATT_T), jnp.int32),
            pltpu.VMEM((seq, ATT_T), F32),
        ],
        compiler_params=pltpu.CompilerParams(
            dimension_semantics=("arbitrary", "arbitrary"), vmem_limit_bytes=VMEM_LIMIT),
        name="sparse_attn",
    )(proj, proj, proj, proj)


def _merge_kernel(ga_ref, gb_ref, u_ref, v_ref, yb_ref, x_ref, g1_ref, sc2_ref, sh2_ref, n2g_ref,
                  wsp_ref, bsp_ref, wpa_ref, wpb_ref, wout_ref, x1_ref, h2_ref, ya_ref, *, tm):
    row = lax.broadcasted_iota(jnp.int32, (CHUNK, CHUNK), 0)
    col = lax.broadcasted_iota(jnp.int32, (CHUNK, CHUNK), 1)
    causal = col <= row
    bsp = bsp_ref[...]
    for g in range(A_GROUPS):
        wm = jnp.where(causal, wsp_ref[g], jnp.zeros((), BF16))
        gcols = slice(g * A_GROUP_DIM, (g + 1) * A_GROUP_DIM)
        for ci in range(tm // CHUNK):
            rows = slice(ci * CHUNK, (ci + 1) * CHUNK)
            mixed = _dot(wm, v_ref[rows, gcols]) + bsp[:, g:g + 1]
            ya_ref[rows, gcols] = (u_ref[rows, gcols].astype(F32) * mixed).astype(BF16)

    a = _dot(ya_ref[...], wpa_ref[...])
    b = _dot(yb_ref[...], wpb_ref[...])
    merged = ga_ref[...].astype(F32) * a + gb_ref[...].astype(F32) * b
    o = _dot(merged.astype(BF16), wout_ref[...])
    x1 = x_ref[...] + g1_ref[...] * o
    x1_ref[...] = x1
    r = lax.rsqrt(jnp.mean(x1 * x1, axis=-1, keepdims=True) + EPS)
    h2 = (x1 * r * n2g_ref[...]) * (1.0 + sc2_ref[...]) + sh2_ref[...]
    h2_ref[...] = h2.astype(BF16)


def _merge(proj, yb, x2, g1, sc2, sh2, norm2_g, wsp, bsp_t, wpa, wpb, wout, seq):
    n = x2.shape[0]
    tm = 256
    tiles_per_batch = seq // tm
    mod_spec = pl.BlockSpec((None, 1, D_MODEL), lambda i: (i // tiles_per_batch, 0, 0))
    const2 = lambda i: (0, 0)
    kern = functools.partial(_merge_kernel, tm=tm)
    return pl.pallas_call(
        kern,
        out_shape=(jax.ShapeDtypeStruct((n, D_MODEL), F32), jax.ShapeDtypeStruct((n, D_MODEL), BF16)),
        grid=(n // tm,),
        in_specs=[
            pl.BlockSpec((tm, D_MODEL), lambda i: (i, COL_GA // D_MODEL)),
            pl.BlockSpec((tm, D_MODEL), lambda i: (i, COL_GB // D_MODEL)),
            pl.BlockSpec((tm, A_WIDTH), lambda i: (i, COL_U // A_WIDTH)),
            pl.BlockSpec((tm, A_WIDTH), lambda i: (i, COL_VA // A_WIDTH)),
            pl.BlockSpec((tm, Q_WIDTH), lambda i: (i, 0)),
            pl.BlockSpec((tm, D_MODEL), lambda i: (i, 0)),
            mod_spec, mod_spec, mod_spec,
            pl.BlockSpec((1, D_MODEL), const2),
            pl.BlockSpec((A_GROUPS, CHUNK, CHUNK), lambda i: (0, 0, 0)),
            pl.BlockSpec((CHUNK, A_GROUPS), const2),
            pl.BlockSpec((A_WIDTH, D_MODEL), const2, pipeline_mode=pl.Buffered(1)),
            pl.BlockSpec((Q_WIDTH, D_MODEL), const2, pipeline_mode=pl.Buffered(1)),
            pl.BlockSpec((D_MODEL, D_MODEL), const2, pipeline_mode=pl.Buffered(1)),
        ],
        out_specs=(pl.BlockSpec((tm, D_MODEL), lambda i: (i, 0)),
                   pl.BlockSpec((tm, D_MODEL), lambda i: (i, 0))),
        scratch_shapes=[pltpu.VMEM((tm, A_WIDTH), BF16)],
        compiler_params=pltpu.CompilerParams(
            dimension_semantics=("parallel",), vmem_limit_bytes=VMEM_LIMIT),
        name="merge_out",
    )(proj, proj, proj, proj, yb, x2, g1, sc2, sh2, norm2_g.reshape(1, D_MODEL), wsp, bsp_t, wpa, wpb, wout)


FFN_TM = 512
FFN_TF = 512
FFN_HALO = 16


def _ffn_kernel(h_ref, hprev_ref, wa_ref, wb_ref, cwa_ref, cwb_ref, cba_ref, cbb_ref, wd_ref, x1_ref, g2_ref,
                o_ref, hh_ref, up_ref, acc_ref, *, tiles_per_batch):
    i = pl.program_id(0)
    f = pl.program_id(1)

    @pl.when(f == 0)
    def _():
        first = (i % tiles_per_batch) == 0

        @pl.when(first)
        def _():
            hh_ref[:FFN_HALO] = jnp.zeros((FFN_HALO, D_MODEL), BF16)

        @pl.when(jnp.logical_not(first))
        def _():
            hh_ref[:FFN_HALO] = hprev_ref[...]

        hh_ref[FFN_HALO:] = h_ref[...]
        acc_ref[...] = jnp.zeros_like(acc_ref)

    hh = hh_ref[...]
    up_ref[:, :FFN_TF] = _dot(hh, wa_ref[...])
    up_ref[:, FFN_TF:] = _dot(hh, wb_ref[...])

    def conv(cols, cw_ref, cb_ref):
        y = cb_ref[...] + cw_ref[0:1] * up_ref[FFN_HALO - 2:FFN_HALO - 2 + FFN_TM, cols]
        y = y + cw_ref[1:2] * up_ref[FFN_HALO - 1:FFN_HALO - 1 + FFN_TM, cols]
        return y + cw_ref[2:3] * up_ref[FFN_HALO:FFN_HALO + FFN_TM, cols]

    a = conv(slice(0, FFN_TF), cwa_ref, cba_ref)
    b = conv(slice(FFN_TF, 2 * FFN_TF), cwb_ref, cbb_ref)
    gated = (jax.nn.silu(a) * b).astype(BF16)
    acc_ref[...] += _dot(gated, wd_ref[...])

    @pl.when(f == pl.num_programs(1) - 1)
    def _():
        o_ref[...] = x1_ref[...] + g2_ref[...] * acc_ref[...]


def _ffn(h2, x1, g2, w_up, conv_w, conv_b, w_down, seq):
    n = h2.shape[0]
    tm, tf = FFN_TM, FFN_TF
    nf = D_FF // tf
    tiles_per_batch = seq // tm
    halo_blocks = tm // FFN_HALO
    kern = functools.partial(_ffn_kernel, tiles_per_batch=tiles_per_batch)
    return pl.pallas_call(
        kern,
        out_shape=jax.ShapeDtypeStruct((n, D_MODEL), F32),
        grid=(n // tm, nf),
        in_specs=[
            pl.BlockSpec((tm, D_MODEL), lambda i, f: (i, 0)),
            pl.BlockSpec((FFN_HALO, D_MODEL), lambda i, f: (jnp.maximum(i * halo_blocks - 1, 0), 0)),
            pl.BlockSpec((D_MODEL, tf), lambda i, f: (0, f)),
            pl.BlockSpec((D_MODEL, tf), lambda i, f: (0, f + nf)),
            pl.BlockSpec((CONV_W, tf), lambda i, f: (0, f)),
            pl.BlockSpec((CONV_W, tf), lambda i, f: (0, f + nf)),
            pl.BlockSpec((1, tf), lambda i, f: (0, f)),
            pl.BlockSpec((1, tf), lambda i, f: (0, f + nf)),
            pl.BlockSpec((tf, D_MODEL), lambda i, f: (f, 0)),
            pl.BlockSpec((tm, D_MODEL), lambda i, f: (i, 0)),
            pl.BlockSpec((None, 1, D_MODEL), lambda i, f: (i // tiles_per_batch, 0, 0)),
        ],
        out_specs=pl.BlockSpec((tm, D_MODEL), lambda i, f: (i, 0)),
        scratch_shapes=[
            pltpu.VMEM((tm + FFN_HALO, D_MODEL), BF16),
            pltpu.VMEM((tm + FFN_HALO, 2 * tf), F32),
            pltpu.VMEM((tm, D_MODEL), F32),
        ],
        compiler_params=pltpu.CompilerParams(
            dimension_semantics=("parallel", "arbitrary"), vmem_limit_bytes=VMEM_LIMIT),
        name="conv_ffn",
    )(h2, h2, w_up, w_up, conv_w, conv_w, conv_b.reshape(1, 2 * D_FF), conv_b.reshape(1, 2 * D_FF),
      w_down, x1, g2)


def _fused_in_weight(w_in):
    sizes = [A_WIDTH, A_WIDTH, Q_WIDTH, KV_WIDTH, KV_WIDTH, IDX_HEADS * IDX_DIM, IDX_DIM, IDX_HEADS, D_MODEL, D_MODEL]
    offs = [0]
    for s in sizes:
        offs.append(offs[-1] + s)
    u, va, q, k, vb, qi, ki, wi, ga, gb = [w_in[:, offs[t]:offs[t + 1]] for t in range(len(sizes))]
    z = lambda ncol: jnp.zeros((D_MODEL, ncol), w_in.dtype)
    misc = jnp.concatenate([k, vb, ki, z(IDX_DIM), z(IDX_DIM), ki, wi, z(LANES - IDX_HEADS), z(LANES)], axis=1)
    w_cat = jnp.concatenate([ga, gb, u, va, q, misc, qi], axis=1)
    assert w_cat.shape[1] == PROJ_COLS and misc.shape[1] == PROJ_TN
    return w_cat.astype(BF16)


def kernel(x, c, w_ada, b_ada, norm1_g, w_in, v_norm_g, w_spatial, b_spatial, q_norm_g, k_norm_g, w_proj_a,
           w_proj_b, w_out, norm2_g, w_up, conv_w, conv_b, w_down):
    bsz, seq, _ = x.shape
    n = bsz * seq
    x2 = x.reshape(n, D_MODEL)

    mod = _ada(c, w_ada, b_ada)
    sh1, sc1, g1, sh2, sc2, g2 = [mod[:, t * D_MODEL:(t + 1) * D_MODEL].reshape(bsz, 1, D_MODEL) for t in range(6)]

    proj = _proj(x2, sc1, sh1, norm1_g, _fused_in_weight(w_in), v_norm_g, q_norm_g, k_norm_g, seq)
    yb = _attention(proj, bsz, seq)
    x1, h2 = _merge(proj, yb, x2, g1, sc2, sh2, norm2_g, w_spatial.astype(BF16), b_spatial.T,
                    w_proj_a.astype(BF16), w_proj_b.astype(BF16), w_out.astype(BF16), seq)
    out = _ffn(h2, x1, g2, w_up.astype(BF16), conv_w, conv_b, w_down.astype(BF16), seq)
    return out.reshape(bsz, seq, D_MODEL)
```

```python
import functools
import math

import jax
import jax.numpy as jnp
from jax import lax
from jax.experimental import pallas as pl
from jax.experimental.pallas import tpu as pltpu

F32 = jnp.float32
BF16 = jnp.bfloat16

D_MODEL = 2048
CHUNK = 128
A_GROUPS = 8
A_GROUP_DIM = 128
A_WIDTH = A_GROUPS * A_GROUP_DIM
N_HEADS = 8
HEAD_DIM = 128
N_KV = 2
Q_WIDTH = N_HEADS * HEAD_DIM
KV_WIDTH = N_KV * HEAD_DIM
IDX_HEADS = 16
IDX_DIM = 64
TOPK_MAX = 256
D_FF = 5632
CONV_W = 3
EPS = 1e-6

LANES = 128
VMEM_LIMIT = 56 * 1024 * 1024

PROJ_TN = 1024
COL_GA, COL_GB, COL_U, COL_VA, COL_Q, COL_MISC, COL_QI = 0, 2048, 4096, 5120, 6144, 7168, 8192
PROJ_COLS = 9216
MISC_K, MISC_V, MISC_KI_EVEN, MISC_KI_ODD, MISC_WI = 0, 256, 512, 640, 768

ATT_T = 128
ATT_CH = 256
ATT_SPAN = 512
NEG_BIG = -1e30
INT_MIN = -(2 ** 31)
KEY_NEG_INF = -2139095041


def _dot(a, b):
    return jnp.dot(a, b, preferred_element_type=F32)


def _dot_nt(a, b):
    return lax.dot_general(a, b, (((1,), (1,)), ((), ())), preferred_element_type=F32)


def _ada_kernel(c_ref, w_ref, b_ref, o_ref):
    cs = jax.nn.silu(c_ref[...]).astype(BF16)
    o_ref[...] = _dot(cs, w_ref[...].astype(BF16)) + b_ref[...]


def _ada(c, w_ada, b_ada):
    bsz = c.shape[0]
    n = w_ada.shape[1]
    tn = 1024
    return pl.pallas_call(
        _ada_kernel,
        out_shape=jax.ShapeDtypeStruct((bsz, n), F32),
        grid=(n // tn,),
        in_specs=[
            pl.BlockSpec((bsz, D_MODEL), lambda j: (0, 0)),
            pl.BlockSpec((D_MODEL, tn), lambda j: (0, j)),
            pl.BlockSpec((1, tn), lambda j: (0, j)),
        ],
        out_specs=pl.BlockSpec((bsz, tn), lambda j: (0, j)),
        compiler_params=pltpu.CompilerParams(
            dimension_semantics=("arbitrary",), vmem_limit_bytes=VMEM_LIMIT),
        name="ada_mod",
    )(c, w_ada, b_ada.reshape(1, n))


def _head_rms(x, g, scale):
    r = lax.rsqrt(jnp.mean(x * x, axis=-1, keepdims=True) + EPS)
    y = x * r * g
    return y if scale is None else y * scale


def _proj_kernel(x_ref, sc_ref, sh_ref, g_ref, w_ref, vg_ref, qg_ref, kg_ref, o_ref, h_ref):
    j = pl.program_id(1)

    @pl.when(j == 0)
    def _():
        x = x_ref[...]
        r = lax.rsqrt(jnp.mean(x * x, axis=-1, keepdims=True) + EPS)
        h = (x * r * g_ref[...]) * (1.0 + sc_ref[...]) + sh_ref[...]
        h_ref[...] = h.astype(BF16)

    acc = _dot(h_ref[...], w_ref[...])

    @pl.when(j < COL_U // PROJ_TN)
    def _():
        o_ref[...] = jax.nn.sigmoid(acc).astype(BF16)

    @pl.when(j == COL_U // PROJ_TN)
    def _():
        o_ref[...] = jax.nn.gelu(acc).astype(BF16)

    @pl.when(j == COL_VA // PROJ_TN)
    def _():
        v = jax.nn.gelu(acc)
        r = lax.rsqrt(jnp.mean(v * v, axis=-1, keepdims=True) + EPS)
        o_ref[...] = (v * r * vg_ref[...]).astype(BF16)

    @pl.when(j == COL_Q // PROJ_TN)
    def _():
        qscale = (HEAD_DIM ** -0.5) * math.log2(math.e)
        for h in range(N_HEADS):
            sl = slice(h * HEAD_DIM, (h + 1) * HEAD_DIM)
            o_ref[:, sl] = _head_rms(acc[:, sl], qg_ref[...], qscale).astype(BF16)

    @pl.when(j == COL_MISC // PROJ_TN)
    def _():
        for h in range(N_KV):
            sl = slice(MISC_K + h * HEAD_DIM, MISC_K + (h + 1) * HEAD_DIM)
            o_ref[:, sl] = _head_rms(acc[:, sl], kg_ref[...], None).astype(BF16)
        o_ref[:, MISC_V:] = acc[:, MISC_V:].astype(BF16)

    @pl.when(j == COL_QI // PROJ_TN)
    def _():
        o_ref[...] = acc.astype(BF16)


def _proj(x2, sc1, sh1, norm1_g, w_cat, v_norm_g, q_norm_g, k_norm_g, seq):
    n = x2.shape[0]
    tm = 512
    tiles_per_batch = seq // tm
    mod_spec = pl.BlockSpec((None, 1, D_MODEL), lambda i, j: (i // tiles_per_batch, 0, 0))
    return pl.pallas_call(
        _proj_kernel,
        out_shape=jax.ShapeDtypeStruct((n, PROJ_COLS), BF16),
        grid=(n // tm, PROJ_COLS // PROJ_TN),
        in_specs=[
            pl.BlockSpec((tm, D_MODEL), lambda i, j: (i, 0)),
            mod_spec, mod_spec,
            pl.BlockSpec((1, D_MODEL), lambda i, j: (0, 0)),
            pl.BlockSpec((D_MODEL, PROJ_TN), lambda i, j: (0, j)),
            pl.BlockSpec((1, A_WIDTH), lambda i, j: (0, 0)),
            pl.BlockSpec((1, HEAD_DIM), lambda i, j: (0, 0)),
            pl.BlockSpec((1, HEAD_DIM), lambda i, j: (0, 0)),
        ],
        out_specs=pl.BlockSpec((tm, PROJ_TN), lambda i, j: (i, j)),
        scratch_shapes=[pltpu.VMEM((tm, D_MODEL), BF16)],
        compiler_params=pltpu.CompilerParams(
            dimension_semantics=("parallel", "arbitrary"), vmem_limit_bytes=VMEM_LIMIT),
        name="proj_in",
    )(x2, sc1, sh1, norm1_g.reshape(1, D_MODEL), w_cat, v_norm_g.reshape(1, A_WIDTH),
      q_norm_g.reshape(1, HEAD_DIM), k_norm_g.reshape(1, HEAD_DIM))


def _sublane_tree(x, op):
    r, c = x.shape
    x = x.reshape(r // 64, 8, 8, c)
    y = x[0]
    for t in range(1, r // 64):
        y = op(y, x[t])
    z = op(op(y[0], y[1]), op(y[2], y[3]))
    return op(z, op(op(y[4], y[5]), op(y[6], y[7])))


def _attn_block(s_eff, i, q_ref, misc_ref, wi_ref, qi_ref, o_ref, vt_ref, keys_ref, bias_ref, s_ref, p_ref, topk):
    nch = s_eff // ATT_CH

    idx_scale = IDX_DIM ** -0.5 * IDX_HEADS ** -0.5
    w_t = wi_ref[...].astype(F32).T * idx_scale
    rhs = [jnp.concatenate([qi_ref[:, 256 * r:256 * r + 128], qi_ref[:, 256 * r + 128:256 * r + 256]], axis=0)
           for r in range(IDX_HEADS // 4)]
    q_pos = i * ATT_T + lax.broadcasted_iota(jnp.int32, (ATT_CH, ATT_T), 1)
    row_iota = lax.broadcasted_iota(jnp.int32, (ATT_CH, ATT_T), 0)

    for c in range(nch):
        rows = slice(c * ATT_CH, (c + 1) * ATT_CH)
        k_even = misc_ref[rows, MISC_KI_EVEN:MISC_KI_EVEN + LANES]
        k_odd = misc_ref[rows, MISC_KI_ODD:MISC_KI_ODD + LANES]
        acc = [None, None]
        for r in range(IDX_HEADS // 4):
            l_even = _dot_nt(k_even, rhs[r])
            l_odd = _dot_nt(k_odd, rhs[r])
            terms = (w_t[4 * r:4 * r + 1] * jnp.maximum(l_even[:, :ATT_T], 0.0),
                     w_t[4 * r + 1:4 * r + 2] * jnp.maximum(l_odd[:, :ATT_T], 0.0),
                     w_t[4 * r + 2:4 * r + 3] * jnp.maximum(l_even[:, ATT_T:], 0.0),
                     w_t[4 * r + 3:4 * r + 4] * jnp.maximum(l_odd[:, ATT_T:], 0.0))
            for t, term in enumerate(terms):
                acc[t % 2] = term if acc[t % 2] is None else acc[t % 2] + term
        score = acc[0] + acc[1]
        if (c + 1) * ATT_CH > s_eff - ATT_SPAN:
            score = jnp.where(c * ATT_CH + row_iota <= q_pos, score, -jnp.inf)
        bits = lax.bitcast_convert_type(score, jnp.int32)
        keys_ref[rows, :] = bits ^ ((bits >> 31) & 0x7FFFFFFF)

    def bs_body(it, thr):
        cand = thr | lax.shift_left(jnp.int32(1), 31 - it)
        cand_s = cand ^ INT_MIN
        tot = None
        for c in range(nch):
            ones = jnp.where(keys_ref[c * ATT_CH:(c + 1) * ATT_CH, :] >= cand_s, 1.0, 0.0)
            part = _sublane_tree(ones, jnp.add)
            tot = part if tot is None else tot + part
        cnt = jnp.sum(tot, axis=0, keepdims=True)
        return jnp.where(cnt >= float(topk), cand, thr)

    thr = lax.fori_loop(0, 32, bs_body, jnp.zeros((1, ATT_T), jnp.int32))
    thr_s = jnp.maximum(thr ^ INT_MIN, KEY_NEG_INF + 1)
    for c in range(nch):
        rows = slice(c * ATT_CH, (c + 1) * ATT_CH)
        bias_ref[rows, :] = jnp.where(keys_ref[rows, :] >= thr_s, 0.0, NEG_BIG)

    for p in range(N_HEADS // 2):
        g = (2 * p) // (N_HEADS // N_KV)
        q_pair = jnp.concatenate([q_ref[:, 256 * p:256 * p + 128], q_ref[:, 256 * p + 128:256 * p + 256]], axis=0)
        mx = None
        for c in range(nch):
            rows = slice(c * ATT_CH, (c + 1) * ATT_CH)
            bias = bias_ref[rows, :]
            s = _dot_nt(misc_ref[rows, MISC_K + g * HEAD_DIM:MISC_K + (g + 1) * HEAD_DIM], q_pair)
            s = s + jnp.concatenate([bias, bias], axis=1)
            s_ref[rows, :] = s
            part = _sublane_tree(s, jnp.maximum)
            mx = part if mx is None else jnp.maximum(mx, part)
        m = jnp.max(mx, axis=0, keepdims=True)
        ls = None
        for c in range(nch):
            rows = slice(c * ATT_CH, (c + 1) * ATT_CH)
            pm = jnp.exp2(s_ref[rows, :] - m)
            p_ref[rows, :] = pm.astype(BF16)
            part = _sublane_tree(pm, jnp.add)
            ls = part if ls is None else ls + part
        l = jnp.sum(ls, axis=0, keepdims=True)
        out = _dot(vt_ref[g, :, 0:s_eff], p_ref[0:s_eff, :]) / l
        o_ref[:, 256 * p:256 * p + 128] = out[:, :ATT_T].T.astype(BF16)
        o_ref[:, 256 * p + 128:256 * p + 256] = out[:, ATT_T:].T.astype(BF16)


def _attn_kernel(q_ref, misc_ref, wi_ref, qi_ref, o_ref, vt_ref, keys_ref, bias_ref, s_ref, p_ref, *, seq, topk):
    i = pl.program_id(1)

    @pl.when(i == 0)
    def _():
        for g in range(N_KV):
            for c in range(seq // ATT_CH):
                rows = slice(c * ATT_CH, (c + 1) * ATT_CH)
                vg = misc_ref[rows, MISC_V + g * HEAD_DIM:MISC_V + (g + 1) * HEAD_DIM]
                vt_ref[g, :, rows] = vg.astype(F32).T.astype(BF16)

    n_span = (i * ATT_T + ATT_T + ATT_SPAN - 1) // ATT_SPAN
    for ns in range(1, seq // ATT_SPAN + 1):
        @pl.when(n_span == ns)
        def _(ns=ns):
            _attn_block(ns * ATT_SPAN, i, q_ref, misc_ref, wi_ref, qi_ref, o_ref, vt_ref, keys_ref, bias_ref,
                        s_ref, p_ref, topk)


def _attention(proj, bsz, seq):
    n = proj.shape[0]
    nblk = seq // ATT_T
    topk = min(TOPK_MAX, seq // 4)
    kern = functools.partial(_attn_kernel, seq=seq, topk=topk)
    return pl.pallas_call(
        kern,
        out_shape=jax.ShapeDtypeStruct((n, Q_WIDTH), BF16),
        grid=(bsz, nblk),
        in_specs=[
            pl.BlockSpec((ATT_T, Q_WIDTH), lambda b, i: (b * nblk + i, COL_Q // Q_WIDTH)),
            pl.BlockSpec((seq, PROJ_TN), lambda b, i: (b, COL_MISC // PROJ_TN)),
            pl.BlockSpec((ATT_T, LANES), lambda b, i: (b * nblk + i, (COL_MISC + MISC_WI) // LANES)),
            pl.BlockSpec((ATT_T, IDX_HEADS * IDX_DIM), lambda b, i: (b * nblk + i, COL_QI // (IDX_HEADS * IDX_DIM))),
        ],
        out_specs=pl.BlockSpec((ATT_T, Q_WIDTH), lambda b, i: (b * nblk + i, 0)),
        scratch_shapes=[
            pltpu.VMEM((N_KV, HEAD_DIM, seq), BF16),
            pltpu.VMEM((seq, ATT_T), jnp.int32),
            pltpu.VMEM((seq, ATT_T), F32),
            pltpu.VMEM((seq, 2 * ATT_T), F32),
            pltpu.VMEM((seq, 2 * ATT_T), BF16),
        ],
        compiler_params=pltpu.CompilerParams(
            dimension_semantics=("arbitrary", "arbitrary"), vmem_limit_bytes=VMEM_LIMIT),
        name="sparse_attn",
    )(proj, proj, proj, proj)


def _merge_kernel(ga_ref, gb_ref, u_ref, v_ref, yb_ref, x_ref, g1_ref, sc2_ref, sh2_ref, n2g_ref,
                  wsp_ref, bsp_ref, wpa_ref, wpb_ref, wout_ref, x1_ref, h2_ref, ya_ref, *, tm):
    row = lax.broadcasted_iota(jnp.int32, (CHUNK, CHUNK), 0)
    col = lax.broadcasted_iota(jnp.int32, (CHUNK, CHUNK), 1)
    causal = col <= row
    bsp = bsp_ref[...]
    for g in range(A_GROUPS):
        wm = jnp.where(causal, wsp_ref[g], jnp.zeros((), BF16))
        gcols = slice(g * A_GROUP_DIM, (g + 1) * A_GROUP_DIM)
        for ci in range(tm // CHUNK):
            rows = slice(ci * CHUNK, (ci + 1) * CHUNK)
            mixed = _dot(wm, v_ref[rows, gcols]) + bsp[:, g:g + 1]
            ya_ref[rows, gcols] = (u_ref[rows, gcols].astype(F32) * mixed).astype(BF16)

    a = _dot(ya_ref[...], wpa_ref[...])
    b = _dot(yb_ref[...], wpb_ref[...])
    merged = ga_ref[...].astype(F32) * a + gb_ref[...].astype(F32) * b
    o = _dot(merged.astype(BF16), wout_ref[...])
    x1 = x_ref[...] + g1_ref[...] * o
    x1_ref[...] = x1
    r = lax.rsqrt(jnp.mean(x1 * x1, axis=-1, keepdims=True) + EPS)
    h2 = (x1 * r * n2g_ref[...]) * (1.0 + sc2_ref[...]) + sh2_ref[...]
    h2_ref[...] = h2.astype(BF16)


def _merge(proj, yb, x2, g1, sc2, sh2, norm2_g, wsp, bsp_t, wpa, wpb, wout, seq):
    n = x2.shape[0]
    tm = 256
    tiles_per_batch = seq // tm
    mod_spec = pl.BlockSpec((None, 1, D_MODEL), lambda i: (i // tiles_per_batch, 0, 0))
    const2 = lambda i: (0, 0)
    kern = functools.partial(_merge_kernel, tm=tm)
    return pl.pallas_call(
        kern,
        out_shape=(jax.ShapeDtypeStruct((n, D_MODEL), F32), jax.ShapeDtypeStruct((n, D_MODEL), BF16)),
        grid=(n // tm,),
        in_specs=[
            pl.BlockSpec((tm, D_MODEL), lambda i: (i, COL_GA // D_MODEL)),
            pl.BlockSpec((tm, D_MODEL), lambda i: (i, COL_GB // D_MODEL)),
            pl.BlockSpec((tm, A_WIDTH), lambda i: (i, COL_U // A_WIDTH)),
            pl.BlockSpec((tm, A_WIDTH), lambda i: (i, COL_VA // A_WIDTH)),
            pl.BlockSpec((tm, Q_WIDTH), lambda i: (i, 0)),
            pl.BlockSpec((tm, D_MODEL), lambda i: (i, 0)),
            mod_spec, mod_spec, mod_spec,
            pl.BlockSpec((1, D_MODEL), const2),
            pl.BlockSpec((A_GROUPS, CHUNK, CHUNK), lambda i: (0, 0, 0)),
            pl.BlockSpec((CHUNK, A_GROUPS), const2),
            pl.BlockSpec((A_WIDTH, D_MODEL), const2, pipeline_mode=pl.Buffered(1)),
            pl.BlockSpec((Q_WIDTH, D_MODEL), const2, pipeline_mode=pl.Buffered(1)),
            pl.BlockSpec((D_MODEL, D_MODEL), const2, pipeline_mode=pl.Buffered(1)),
        ],
        out_specs=(pl.BlockSpec((tm, D_MODEL), lambda i: (i, 0)),
                   pl.BlockSpec((tm, D_MODEL), lambda i: (i, 0))),
        scratch_shapes=[pltpu.VMEM((tm, A_WIDTH), BF16)],
        compiler_params=pltpu.CompilerParams(
            dimension_semantics=("parallel",), vmem_limit_bytes=VMEM_LIMIT),
        name="merge_out",
    )(proj, proj, proj, proj, yb, x2, g1, sc2, sh2, norm2_g.reshape(1, D_MODEL), wsp, bsp_t, wpa, wpb, wout)


FFN_TM = 512
FFN_TF = 512
FFN_HALO = 16


def _ffn_kernel(h_ref, hprev_ref, wa_ref, wb_ref, cwa_ref, cwb_ref, cba_ref, cbb_ref, wd_ref, x1_ref, g2_ref,
                o_ref, hh_ref, up_ref, acc_ref, *, tiles_per_batch):
    i = pl.program_id(0)
    f = pl.program_id(1)

    @pl.when(f == 0)
    def _():
        first = (i % tiles_per_batch) == 0

        @pl.when(first)
        def _():
            hh_ref[:FFN_HALO] = jnp.zeros((FFN_HALO, D_MODEL), BF16)

        @pl.when(jnp.logical_not(first))
        def _():
            hh_ref[:FFN_HALO] = hprev_ref[...]

        hh_ref[FFN_HALO:] = h_ref[...]
        acc_ref[...] = jnp.zeros_like(acc_ref)

    hh = hh_ref[...]
    up_ref[:, :FFN_TF] = _dot(hh, wa_ref[...])
    up_ref[:, FFN_TF:] = _dot(hh, wb_ref[...])

    def conv(cols, cw_ref, cb_ref):
        y = cb_ref[...] + cw_ref[0:1] * up_ref[FFN_HALO - 2:FFN_HALO - 2 + FFN_TM, cols]
        y = y + cw_ref[1:2] * up_ref[FFN_HALO - 1:FFN_HALO - 1 + FFN_TM, cols]
        return y + cw_ref[2:3] * up_ref[FFN_HALO:FFN_HALO + FFN_TM, cols]

    a = conv(slice(0, FFN_TF), cwa_ref, cba_ref)
    b = conv(slice(FFN_TF, 2 * FFN_TF), cwb_ref, cbb_ref)
    gated = (jax.nn.silu(a) * b).astype(BF16)
    acc_ref[...] += _dot(gated, wd_ref[...])

    @pl.when(f == pl.num_programs(1) - 1)
    def _():
        o_ref[...] = x1_ref[...] + g2_ref[...] * acc_ref[...]


def _ffn(h2, x1, g2, w_up, conv_w, conv_b, w_down, seq):
    n = h2.shape[0]
    tm, tf = FFN_TM, FFN_TF
    nf = D_FF // tf
    tiles_per_batch = seq // tm
    halo_blocks = tm // FFN_HALO
    kern = functools.partial(_ffn_kernel, tiles_per_batch=tiles_per_batch)
    return pl.pallas_call(
        kern,
        out_shape=jax.ShapeDtypeStruct((n, D_MODEL), F32),
        grid=(n // tm, nf),
        in_specs=[
            pl.BlockSpec((tm, D_MODEL), lambda i, f: (i, 0)),
            pl.BlockSpec((FFN_HALO, D_MODEL), lambda i, f: (jnp.maximum(i * halo_blocks - 1, 0), 0)),
            pl.BlockSpec((D_MODEL, tf), lambda i, f: (0, f)),
            pl.BlockSpec((D_MODEL, tf), lambda i, f: (0, f + nf)),
            pl.BlockSpec((CONV_W, tf), lambda i, f: (0, f)),
            pl.BlockSpec((CONV_W, tf), lambda i, f: (0, f + nf)),
            pl.BlockSpec((1, tf), lambda i, f: (0, f)),
            pl.BlockSpec((1, tf), lambda i, f: (0, f + nf)),
            pl.BlockSpec((tf, D_MODEL), lambda i, f: (f, 0)),
            pl.BlockSpec((tm, D_MODEL), lambda i, f: (i, 0)),
            pl.BlockSpec((None, 1, D_MODEL), lambda i, f: (i // tiles_per_batch, 0, 0)),
        ],
        out_specs=pl.BlockSpec((tm, D_MODEL), lambda i, f: (i, 0)),
        scratch_shapes=[
            pltpu.VMEM((tm + FFN_HALO, D_MODEL), BF16),
            pltpu.VMEM((tm + FFN_HALO, 2 * tf), F32),
            pltpu.VMEM((tm, D_MODEL), F32),
        ],
        compiler_params=pltpu.CompilerParams(
            dimension_semantics=("parallel", "arbitrary"), vmem_limit_bytes=VMEM_LIMIT),
        name="conv_ffn",
    )(h2, h2, w_up, w_up, conv_w, conv_w, conv_b.reshape(1, 2 * D_FF), conv_b.reshape(1, 2 * D_FF),
      w_down, x1, g2)


def _fused_in_weight(w_in):
    sizes = [A_WIDTH, A_WIDTH, Q_WIDTH, KV_WIDTH, KV_WIDTH, IDX_HEADS * IDX_DIM, IDX_DIM, IDX_HEADS, D_MODEL, D_MODEL]
    offs = [0]
    for s in sizes:
        offs.append(offs[-1] + s)
    u, va, q, k, vb, qi, ki, wi, ga, gb = [w_in[:, offs[t]:offs[t + 1]] for t in range(len(sizes))]
    z = lambda ncol: jnp.zeros((D_MODEL, ncol), w_in.dtype)
    misc = jnp.concatenate([k, vb, ki, z(IDX_DIM), z(IDX_DIM), ki, wi, z(LANES - IDX_HEADS), z(LANES)], axis=1)
    w_cat = jnp.concatenate([ga, gb, u, va, q, misc, qi], axis=1)
    assert w_cat.shape[1] == PROJ_COLS and misc.shape[1] == PROJ_TN
    return w_cat.astype(BF16)


def kernel(x, c, w_ada, b_ada, norm1_g, w_in, v_norm_g, w_spatial, b_spatial, q_norm_g, k_norm_g, w_proj_a,
           w_proj_b, w_out, norm2_g, w_up, conv_w, conv_b, w_down):
    bsz, seq, _ = x.shape
    n = bsz * seq
    x2 = x.reshape(n, D_MODEL)

    mod = _ada(c, w_ada, b_ada)
    sh1, sc1, g1, sh2, sc2, g2 = [mod[:, t * D_MODEL:(t + 1) * D_MODEL].reshape(bsz, 1, D_MODEL) for t in range(6)]

    proj = _proj(x2, sc1, sh1, norm1_g, _fused_in_weight(w_in), v_norm_g, q_norm_g, k_norm_g, seq)
    yb = _attention(proj, bsz, seq)
    x1, h2 = _merge(proj, yb, x2, g1, sc2, sh2, norm2_g, w_spatial.astype(BF16), b_spatial.T,
                    w_proj_a.astype(BF16), w_proj_b.astype(BF16), w_out.astype(BF16), seq)
    out = _ffn(h2, x1, g2, w_up.astype(BF16), conv_w, conv_b, w_down.astype(BF16), seq)
    return out.reshape(bsz, seq, D_MODEL)
```

```python
import functools
import math

import jax
import jax.numpy as jnp
from jax import lax
from jax.experimental import pallas as pl
from jax.experimental.pallas import tpu as pltpu

F32 = jnp.float32
BF16 = jnp.bfloat16

D_MODEL = 2048
CHUNK = 128
A_GROUPS = 8
A_GROUP_DIM = 128
A_WIDTH = A_GROUPS * A_GROUP_DIM
N_HEADS = 8
HEAD_DIM = 128
N_KV = 2
Q_WIDTH = N_HEADS * HEAD_DIM
KV_WIDTH = N_KV * HEAD_DIM
IDX_HEADS = 16
IDX_DIM = 64
TOPK_MAX = 256
D_FF = 5632
CONV_W = 3
EPS = 1e-6

LANES = 128
VMEM_LIMIT = 56 * 1024 * 1024

PROJ_TN = 1024
COL_GA, COL_GB, COL_U, COL_VA, COL_Q, COL_MISC, COL_QI = 0, 2048, 4096, 5120, 6144, 7168, 8192
PROJ_COLS = 9216
MISC_K, MISC_V, MISC_KI_EVEN, MISC_KI_ODD, MISC_WI = 0, 256, 512, 640, 768

ATT_T = 128
ATT_CH = 256
ATT_SPAN = 512
NEG_BIG = -1e30
INT_MIN = -(2 ** 31)
KEY_NEG_INF = -2139095041
MOST_NEG_F32 = -3.4028234663852886e38


def _dot(a, b):
    return jnp.dot(a, b, preferred_element_type=F32)


def _dot_nt(a, b):
    return lax.dot_general(a, b, (((1,), (1,)), ((), ())), preferred_element_type=F32)


def _ada_kernel(c_ref, w_ref, b_ref, o_ref):
    cs = jax.nn.silu(c_ref[...]).astype(BF16)
    o_ref[...] = _dot(cs, w_ref[...].astype(BF16)) + b_ref[...]


def _ada(c, w_ada, b_ada):
    bsz = c.shape[0]
    n = w_ada.shape[1]
    tn = 1024
    return pl.pallas_call(
        _ada_kernel,
        out_shape=jax.ShapeDtypeStruct((bsz, n), F32),
        grid=(n // tn,),
        in_specs=[
            pl.BlockSpec((bsz, D_MODEL), lambda j: (0, 0)),
            pl.BlockSpec((D_MODEL, tn), lambda j: (0, j)),
            pl.BlockSpec((1, tn), lambda j: (0, j)),
        ],
        out_specs=pl.BlockSpec((bsz, tn), lambda j: (0, j)),
        compiler_params=pltpu.CompilerParams(
            dimension_semantics=("arbitrary",), vmem_limit_bytes=VMEM_LIMIT),
        name="ada_mod",
    )(c, w_ada, b_ada.reshape(1, n))


def _head_rms(x, g, scale):
    r = lax.rsqrt(jnp.mean(x * x, axis=-1, keepdims=True) + EPS)
    y = x * r * g
    return y if scale is None else y * scale


def _proj_kernel(x_ref, sc_ref, sh_ref, g_ref, w_ref, vg_ref, qg_ref, kg_ref, o_ref, h_ref):
    j = pl.program_id(1)

    @pl.when(j == 0)
    def _():
        x = x_ref[...]
        r = lax.rsqrt(jnp.mean(x * x, axis=-1, keepdims=True) + EPS)
        h = (x * r * g_ref[...]) * (1.0 + sc_ref[...]) + sh_ref[...]
        h_ref[...] = h.astype(BF16)

    acc = _dot(h_ref[...], w_ref[...])

    @pl.when(j < COL_U // PROJ_TN)
    def _():
        o_ref[...] = jax.nn.sigmoid(acc).astype(BF16)

    @pl.when(j == COL_U // PROJ_TN)
    def _():
        o_ref[...] = jax.nn.gelu(acc).astype(BF16)

    @pl.when(j == COL_VA // PROJ_TN)
    def _():
        v = jax.nn.gelu(acc)
        r = lax.rsqrt(jnp.mean(v * v, axis=-1, keepdims=True) + EPS)
        o_ref[...] = (v * r * vg_ref[...]).astype(BF16)

    @pl.when(j == COL_Q // PROJ_TN)
    def _():
        qscale = (HEAD_DIM ** -0.5) * math.log2(math.e)
        for h in range(N_HEADS):
            sl = slice(h * HEAD_DIM, (h + 1) * HEAD_DIM)
            o_ref[:, sl] = _head_rms(acc[:, sl], qg_ref[...], qscale).astype(BF16)

    @pl.when(j == COL_MISC // PROJ_TN)
    def _():
        for h in range(N_KV):
            sl = slice(MISC_K + h * HEAD_DIM, MISC_K + (h + 1) * HEAD_DIM)
            o_ref[:, sl] = _head_rms(acc[:, sl], kg_ref[...], None).astype(BF16)
        o_ref[:, MISC_V:] = acc[:, MISC_V:].astype(BF16)

    @pl.when(j == COL_QI // PROJ_TN)
    def _():
        o_ref[...] = acc.astype(BF16)


def _proj(x2, sc1, sh1, norm1_g, w_cat, v_norm_g, q_norm_g, k_norm_g, seq):
    n = x2.shape[0]
    tm = 512
    tiles_per_batch = seq // tm
    mod_spec = pl.BlockSpec((None, 1, D_MODEL), lambda i, j: (i // tiles_per_batch, 0, 0))
    return pl.pallas_call(
        _proj_kernel,
        out_shape=jax.ShapeDtypeStruct((n, PROJ_COLS), BF16),
        grid=(n // tm, PROJ_COLS // PROJ_TN),
        in_specs=[
            pl.BlockSpec((tm, D_MODEL), lambda i, j: (i, 0)),
            mod_spec, mod_spec,
            pl.BlockSpec((1, D_MODEL), lambda i, j: (0, 0)),
            pl.BlockSpec((D_MODEL, PROJ_TN), lambda i, j: (0, j)),
            pl.BlockSpec((1, A_WIDTH), lambda i, j: (0, 0)),
            pl.BlockSpec((1, HEAD_DIM), lambda i, j: (0, 0)),
            pl.BlockSpec((1, HEAD_DIM), lambda i, j: (0, 0)),
        ],
        out_specs=pl.BlockSpec((tm, PROJ_TN), lambda i, j: (i, j)),
        scratch_shapes=[pltpu.VMEM((tm, D_MODEL), BF16)],
        compiler_params=pltpu.CompilerParams(
            dimension_semantics=("parallel", "arbitrary"), vmem_limit_bytes=VMEM_LIMIT),
        name="proj_in",
    )(x2, sc1, sh1, norm1_g.reshape(1, D_MODEL), w_cat, v_norm_g.reshape(1, A_WIDTH),
      q_norm_g.reshape(1, HEAD_DIM), k_norm_g.reshape(1, HEAD_DIM))


def _sublane_tree(x, op):
    r, c = x.shape
    x = x.reshape(r // 64, 8, 8, c)
    y = x[0]
    for t in range(1, r // 64):
        y = op(y, x[t])
    z = op(op(y[0], y[1]), op(y[2], y[3]))
    return op(z, op(op(y[4], y[5]), op(y[6], y[7])))


def _attn_block(s_eff, i, q_ref, misc_ref, wi_ref, qi_ref, o_ref, vt_ref, score_ref, bias_ref, s_ref, p_ref, topk):
    nch = s_eff // ATT_CH

    idx_scale = IDX_DIM ** -0.5 * IDX_HEADS ** -0.5
    w_t = wi_ref[...].astype(F32).T * idx_scale
    rhs = [jnp.concatenate([qi_ref[:, 256 * r:256 * r + 128], qi_ref[:, 256 * r + 128:256 * r + 256]], axis=0)
           for r in range(IDX_HEADS // 4)]
    q_pos = i * ATT_T + lax.broadcasted_iota(jnp.int32, (ATT_CH, ATT_T), 1)
    row_iota = lax.broadcasted_iota(jnp.int32, (ATT_CH, ATT_T), 0)

    for c in range(nch):
        rows = slice(c * ATT_CH, (c + 1) * ATT_CH)
        k_even = misc_ref[rows, MISC_KI_EVEN:MISC_KI_EVEN + LANES]
        k_odd = misc_ref[rows, MISC_KI_ODD:MISC_KI_ODD + LANES]
        acc = [None, None]
        for r in range(IDX_HEADS // 4):
            l_even = _dot_nt(k_even, rhs[r])
            l_odd = _dot_nt(k_odd, rhs[r])
            terms = (w_t[4 * r:4 * r + 1] * jnp.maximum(l_even[:, :ATT_T], 0.0),
                     w_t[4 * r + 1:4 * r + 2] * jnp.maximum(l_odd[:, :ATT_T], 0.0),
                     w_t[4 * r + 2:4 * r + 3] * jnp.maximum(l_even[:, ATT_T:], 0.0),
                     w_t[4 * r + 3:4 * r + 4] * jnp.maximum(l_odd[:, ATT_T:], 0.0))
            for t, term in enumerate(terms):
                acc[t % 2] = term if acc[t % 2] is None else acc[t % 2] + term
        score = acc[0] + acc[1]
        if (c + 1) * ATT_CH > s_eff - ATT_SPAN:
            score = jnp.where(c * ATT_CH + row_iota <= q_pos, score, -jnp.inf)
        score_ref[rows, :] = score

    def count_ge(thr_f):
        tot = None
        for c in range(nch):
            ones = jnp.where(score_ref[c * ATT_CH:(c + 1) * ATT_CH, :] >= thr_f, 1.0, 0.0)
            part = _sublane_tree(ones, jnp.add)
            tot = part if tot is None else tot + part
        return jnp.sum(tot, axis=0, keepdims=True)

    def code_to_float(code):
        key = code ^ INT_MIN
        return lax.bitcast_convert_type(key ^ ((key >> 31) & 0x7FFFFFFF), F32)

    def bs_body(it, carry):
        code, n_ge = carry
        cand = code | lax.shift_left(jnp.int32(1), 31 - it)
        cnt = count_ge(code_to_float(cand))
        ok = cnt >= float(topk)
        return jnp.where(ok, cand, code), jnp.where(ok, cnt, n_ge)

    code, n_ge = lax.fori_loop(0, 32, bs_body, (jnp.zeros((1, ATT_T), jnp.int32),
                                                jnp.full((1, ATT_T), float(s_eff), F32)))
    thr_f = code_to_float(jnp.maximum(code ^ INT_MIN, KEY_NEG_INF + 1) ^ INT_MIN)
    excess = jnp.max(jnp.where(thr_f > MOST_NEG_F32, n_ge - float(topk), 0.0))

    @pl.when(excess <= 0.0)
    def _():
        for c in range(nch):
            rows = slice(c * ATT_CH, (c + 1) * ATT_CH)
            bias_ref[rows, :] = jnp.where(score_ref[rows, :] >= thr_f, 0.0, NEG_BIG)

    @pl.when(excess > 0.0)
    def _():
        n_tie = None
        for c in range(nch):
            ones = jnp.where(score_ref[c * ATT_CH:(c + 1) * ATT_CH, :] == thr_f, 1.0, 0.0)
            part = _sublane_tree(ones, jnp.add)
            n_tie = part if n_tie is None else n_tie + part
        n_tie = jnp.sum(n_tie, axis=0, keepdims=True)
        need = float(topk) - (n_ge - n_tie)
        lower = (lax.broadcasted_iota(jnp.int32, (ATT_CH, ATT_CH), 1)
                 <= lax.broadcasted_iota(jnp.int32, (ATT_CH, ATT_CH), 0))
        tril = jnp.where(lower, 1.0, 0.0).astype(BF16)
        run = jnp.zeros((1, ATT_T), F32)
        for c in range(nch):
            rows = slice(c * ATT_CH, (c + 1) * ATT_CH)
            sc = score_ref[rows, :]
            tie = jnp.where(sc == thr_f, 1.0, 0.0)
            rank = _dot(tril, tie.astype(BF16)) + run
            keep = (sc > thr_f) | ((sc == thr_f) & (rank <= need))
            bias_ref[rows, :] = jnp.where(keep, 0.0, NEG_BIG)
            run = run + jnp.sum(_sublane_tree(tie, jnp.add), axis=0, keepdims=True)

    for p in range(N_HEADS // 2):
        g = (2 * p) // (N_HEADS // N_KV)
        q_pair = jnp.concatenate([q_ref[:, 256 * p:256 * p + 128], q_ref[:, 256 * p + 128:256 * p + 256]], axis=0)
        mx = None
        for c in range(nch):
            rows = slice(c * ATT_CH, (c + 1) * ATT_CH)
            bias = bias_ref[rows, :]
            s = _dot_nt(misc_ref[rows, MISC_K + g * HEAD_DIM:MISC_K + (g + 1) * HEAD_DIM], q_pair)
            s = s + jnp.concatenate([bias, bias], axis=1)
            s_ref[rows, :] = s
            part = _sublane_tree(s, jnp.maximum)
            mx = part if mx is None else jnp.maximum(mx, part)
        m = jnp.max(mx, axis=0, keepdims=True)
        ls = None
        for c in range(nch):
            rows = slice(c * ATT_CH, (c + 1) * ATT_CH)
            pm = jnp.exp2(s_ref[rows, :] - m)
            p_ref[rows, :] = pm.astype(BF16)
            part = _sublane_tree(pm, jnp.add)
            ls = part if ls is None else ls + part
        l = jnp.sum(ls, axis=0, keepdims=True)
        out = _dot(vt_ref[g, :, 0:s_eff], p_ref[0:s_eff, :]) / l
        o_ref[:, 256 * p:256 * p + 128] = out[:, :ATT_T].T.astype(BF16)
        o_ref[:, 256 * p + 128:256 * p + 256] = out[:, ATT_T:].T.astype(BF16)


def _attn_kernel(q_ref, misc_ref, wi_ref, qi_ref, o_ref, vt_ref, score_ref, bias_ref, s_ref, p_ref, *, seq, topk):
    i = pl.program_id(1)

    @pl.when(i == 0)
    def _():
        for g in range(N_KV):
            for c in range(seq // ATT_CH):
                rows = slice(c * ATT_CH, (c + 1) * ATT_CH)
                vg = misc_ref[rows, MISC_V + g * HEAD_DIM:MISC_V + (g + 1) * HEAD_DIM]
                vt_ref[g, :, rows] = vg.astype(F32).T.astype(BF16)

    n_span = (i * ATT_T + ATT_T + ATT_SPAN - 1) // ATT_SPAN
    for ns in range(1, seq // ATT_SPAN + 1):
        @pl.when(n_span == ns)
        def _(ns=ns):
            _attn_block(ns * ATT_SPAN, i, q_ref, misc_ref, wi_ref, qi_ref, o_ref, vt_ref, score_ref, bias_ref,
                        s_ref, p_ref, topk)


def _attention(proj, bsz, seq):
    n = proj.shape[0]
    nblk = seq // ATT_T
    topk = min(TOPK_MAX, seq // 4)
    kern = functools.partial(_attn_kernel, seq=seq, topk=topk)
    return pl.pallas_call(
        kern,
        out_shape=jax.ShapeDtypeStruct((n, Q_WIDTH), BF16),
        grid=(bsz, nblk),
        in_specs=[
            pl.BlockSpec((ATT_T, Q_WIDTH), lambda b, i: (b * nblk + i, COL_Q // Q_WIDTH)),
            pl.BlockSpec((seq, PROJ_TN), lambda b, i: (b, COL_MISC // PROJ_TN)),
            pl.BlockSpec((ATT_T, LANES), lambda b, i: (b * nblk + i, (COL_MISC + MISC_WI) // LANES)),
            pl.BlockSpec((ATT_T, IDX_HEADS * IDX_DIM), lambda b, i: (b * nblk + i, COL_QI // (IDX_HEADS * IDX_DIM))),
        ],
        out_specs=pl.BlockSpec((ATT_T, Q_WIDTH), lambda b, i: (b * nblk + i, 0)),
        scratch_shapes=[
            pltpu.VMEM((N_KV, HEAD_DIM, seq), BF16),
            pltpu.VMEM((seq, ATT_T), F32),
            pltpu.VMEM((seq, ATT_T), F32),
            pltpu.VMEM((seq, 2 * ATT_T), F32),
            pltpu.VMEM((seq, 2 * ATT_T), BF16),
        ],
        compiler_params=pltpu.CompilerParams(
            dimension_semantics=("arbitrary", "arbitrary"), vmem_limit_bytes=VMEM_LIMIT),
        name="sparse_attn",
    )(proj, proj, proj, proj)


def _merge_kernel(ga_ref, gb_ref, u_ref, v_ref, yb_ref, x_ref, g1_ref, sc2_ref, sh2_ref, n2g_ref,
                  wsp_ref, bsp_ref, wpa_ref, wpb_ref, wout_ref, x1_ref, h2_ref, ya_ref, *, tm):
    row = lax.broadcasted_iota(jnp.int32, (CHUNK, CHUNK), 0)
    col = lax.broadcasted_iota(jnp.int32, (CHUNK, CHUNK), 1)
    causal = col <= row
    bsp = bsp_ref[...]
    for g in range(A_GROUPS):
        wm = jnp.where(causal, wsp_ref[g], jnp.zeros((), BF16))
        gcols = slice(g * A_GROUP_DIM, (g + 1) * A_GROUP_DIM)
        for ci in range(tm // CHUNK):
            rows = slice(ci * CHUNK, (ci + 1) * CHUNK)
            mixed = _dot(wm, v_ref[rows, gcols]) + bsp[:, g:g + 1]
            ya_ref[rows, gcols] = (u_ref[rows, gcols].astype(F32) * mixed).astype(BF16)

    a = _dot(ya_ref[...], wpa_ref[...])
    b = _dot(yb_ref[...], wpb_ref[...])
    merged = ga_ref[...].astype(F32) * a + gb_ref[...].astype(F32) * b
    o = _dot(merged.astype(BF16), wout_ref[...])
    x1 = x_ref[...] + g1_ref[...] * o
    x1_ref[...] = x1
    r = lax.rsqrt(jnp.mean(x1 * x1, axis=-1, keepdims=True) + EPS)
    h2 = (x1 * r * n2g_ref[...]) * (1.0 + sc2_ref[...]) + sh2_ref[...]
    h2_ref[...] = h2.astype(BF16)


def _merge(proj, yb, x2, g1, sc2, sh2, norm2_g, wsp, bsp_t, wpa, wpb, wout, seq):
    n = x2.shape[0]
    tm = 256
    tiles_per_batch = seq // tm
    mod_spec = pl.BlockSpec((None, 1, D_MODEL), lambda i: (i // tiles_per_batch, 0, 0))
    const2 = lambda i: (0, 0)
    kern = functools.partial(_merge_kernel, tm=tm)
    return pl.pallas_call(
        kern,
        out_shape=(jax.ShapeDtypeStruct((n, D_MODEL), F32), jax.ShapeDtypeStruct((n, D_MODEL), BF16)),
        grid=(n // tm,),
        in_specs=[
            pl.BlockSpec((tm, D_MODEL), lambda i: (i, COL_GA // D_MODEL)),
            pl.BlockSpec((tm, D_MODEL), lambda i: (i, COL_GB // D_MODEL)),
            pl.BlockSpec((tm, A_WIDTH), lambda i: (i, COL_U // A_WIDTH)),
            pl.BlockSpec((tm, A_WIDTH), lambda i: (i, COL_VA // A_WIDTH)),
            pl.BlockSpec((tm, Q_WIDTH), lambda i: (i, 0)),
            pl.BlockSpec((tm, D_MODEL), lambda i: (i, 0)),
            mod_spec, mod_spec, mod_spec,
            pl.BlockSpec((1, D_MODEL), const2),
            pl.BlockSpec((A_GROUPS, CHUNK, CHUNK), lambda i: (0, 0, 0)),
            pl.BlockSpec((CHUNK, A_GROUPS), const2),
            pl.BlockSpec((A_WIDTH, D_MODEL), const2, pipeline_mode=pl.Buffered(1)),
            pl.BlockSpec((Q_WIDTH, D_MODEL), const2, pipeline_mode=pl.Buffered(1)),
            pl.BlockSpec((D_MODEL, D_MODEL), const2, pipeline_mode=pl.Buffered(1)),
        ],
        out_specs=(pl.BlockSpec((tm, D_MODEL), lambda i: (i, 0)),
                   pl.BlockSpec((tm, D_MODEL), lambda i: (i, 0))),
        scratch_shapes=[pltpu.VMEM((tm, A_WIDTH), BF16)],
        compiler_params=pltpu.CompilerParams(
            dimension_semantics=("parallel",), vmem_limit_bytes=VMEM_LIMIT),
        name="merge_out",
    )(proj, proj, proj, proj, yb, x2, g1, sc2, sh2, norm2_g.reshape(1, D_MODEL), wsp, bsp_t, wpa, wpb, wout)


FFN_TM = 512
FFN_TF = 512
FFN_HALO = 16


def _ffn_kernel(h_ref, hprev_ref, wa_ref, wb_ref, cwa_ref, cwb_ref, cba_ref, cbb_ref, wd_ref, x1_ref, g2_ref,
                o_ref, hh_ref, up_ref, acc_ref, *, tiles_per_batch):
    i = pl.program_id(0)
    f = pl.program_id(1)

    @pl.when(f == 0)
    def _():
        first = (i % tiles_per_batch) == 0

        @pl.when(first)
        def _():
            hh_ref[:FFN_HALO] = jnp.zeros((FFN_HALO, D_MODEL), BF16)

        @pl.when(jnp.logical_not(first))
        def _():
            hh_ref[:FFN_HALO] = hprev_ref[...]

        hh_ref[FFN_HALO:] = h_ref[...]
        acc_ref[...] = jnp.zeros_like(acc_ref)

    hh = hh_ref[...]
    up_ref[:, :FFN_TF] = _dot(hh, wa_ref[...])
    up_ref[:, FFN_TF:] = _dot(hh, wb_ref[...])

    def conv(cols, cw_ref, cb_ref):
        y = cb_ref[...] + cw_ref[0:1] * up_ref[FFN_HALO - 2:FFN_HALO - 2 + FFN_TM, cols]
        y = y + cw_ref[1:2] * up_ref[FFN_HALO - 1:FFN_HALO - 1 + FFN_TM, cols]
        return y + cw_ref[2:3] * up_ref[FFN_HALO:FFN_HALO + FFN_TM, cols]

    a = conv(slice(0, FFN_TF), cwa_ref, cba_ref)
    b = conv(slice(FFN_TF, 2 * FFN_TF), cwb_ref, cbb_ref)
    gated = (jax.nn.silu(a) * b).astype(BF16)
    acc_ref[...] += _dot(gated, wd_ref[...])

    @pl.when(f == pl.num_programs(1) - 1)
    def _():
        o_ref[...] = x1_ref[...] + g2_ref[...] * acc_ref[...]


def _ffn(h2, x1, g2, w_up, conv_w, conv_b, w_down, seq):
    n = h2.shape[0]
    tm, tf = FFN_TM, FFN_TF
    nf = D_FF // tf
    tiles_per_batch = seq // tm
    halo_blocks = tm // FFN_HALO
    kern = functools.partial(_ffn_kernel, tiles_per_batch=tiles_per_batch)
    return pl.pallas_call(
        kern,
        out_shape=jax.ShapeDtypeStruct((n, D_MODEL), F32),
        grid=(n // tm, nf),
        in_specs=[
            pl.BlockSpec((tm, D_MODEL), lambda i, f: (i, 0)),
            pl.BlockSpec((FFN_HALO, D_MODEL), lambda i, f: (jnp.maximum(i * halo_blocks - 1, 0), 0)),
            pl.BlockSpec((D_MODEL, tf), lambda i, f: (0, f)),
            pl.BlockSpec((D_MODEL, tf), lambda i, f: (0, f + nf)),
            pl.BlockSpec((CONV_W, tf), lambda i, f: (0, f)),
            pl.BlockSpec((CONV_W, tf), lambda i, f: (0, f + nf)),
            pl.BlockSpec((1, tf), lambda i, f: (0, f)),
            pl.BlockSpec((1, tf), lambda i, f: (0, f + nf)),
            pl.BlockSpec((tf, D_MODEL), lambda i, f: (f, 0)),
            pl.BlockSpec((tm, D_MODEL), lambda i, f: (i, 0)),
            pl.BlockSpec((None, 1, D_MODEL), lambda i, f: (i // tiles_per_batch, 0, 0)),
        ],
        out_specs=pl.BlockSpec((tm, D_MODEL), lambda i, f: (i, 0)),
        scratch_shapes=[
            pltpu.VMEM((tm + FFN_HALO, D_MODEL), BF16),
            pltpu.VMEM((tm + FFN_HALO, 2 * tf), F32),
            pltpu.VMEM((tm, D_MODEL), F32),
        ],
        compiler_params=pltpu.CompilerParams(
            dimension_semantics=("parallel", "arbitrary"), vmem_limit_bytes=VMEM_LIMIT),
        name="conv_ffn",
    )(h2, h2, w_up, w_up, conv_w, conv_w, conv_b.reshape(1, 2 * D_FF), conv_b.reshape(1, 2 * D_FF),
      w_down, x1, g2)


def _fused_in_weight(w_in):
    sizes = [A_WIDTH, A_WIDTH, Q_WIDTH, KV_WIDTH, KV_WIDTH, IDX_HEADS * IDX_DIM, IDX_DIM, IDX_HEADS, D_MODEL, D_MODEL]
    offs = [0]
    for s in sizes:
        offs.append(offs[-1] + s)
    u, va, q, k, vb, qi, ki, wi, ga, gb = [w_in[:, offs[t]:offs[t + 1]] for t in range(len(sizes))]
    z = lambda ncol: jnp.zeros((D_MODEL, ncol), w_in.dtype)
    misc = jnp.concatenate([k, vb, ki, z(IDX_DIM), z(IDX_DIM), ki, wi, z(LANES - IDX_HEADS), z(LANES)], axis=1)
    w_cat = jnp.concatenate([ga, gb, u, va, q, misc, qi], axis=1)
    assert w_cat.shape[1] == PROJ_COLS and misc.shape[1] == PROJ_TN
    return w_cat.astype(BF16)


def kernel(x, c, w_ada, b_ada, norm1_g, w_in, v_norm_g, w_spatial, b_spatial, q_norm_g, k_norm_g, w_proj_a,
           w_proj_b, w_out, norm2_g, w_up, conv_w, conv_b, w_down):
    bsz, seq, _ = x.shape
    n = bsz * seq
    x2 = x.reshape(n, D_MODEL)

    mod = _ada(c, w_ada, b_ada)
    sh1, sc1, g1, sh2, sc2, g2 = [mod[:, t * D_MODEL:(t + 1) * D_MODEL].reshape(bsz, 1, D_MODEL) for t in range(6)]

    proj = _proj(x2, sc1, sh1, norm1_g, _fused_in_weight(w_in), v_norm_g, q_norm_g, k_norm_g, seq)
    yb = _attention(proj, bsz, seq)
    x1, h2 = _merge(proj, yb, x2, g1, sc2, sh2, norm2_g, w_spatial.astype(BF16), b_spatial.T,
                    w_proj_a.astype(BF16), w_proj_b.astype(BF16), w_out.astype(BF16), seq)
    out = _ffn(h2, x1, g2, w_up.astype(BF16), conv_w, conv_b, w_down.astype(BF16), seq)
    return out.reshape(bsz, seq, D_MODEL)
```

```python
import functools
import math

import jax
import jax.numpy as jnp
from jax import lax
from jax.experimental import pallas as pl
from jax.experimental.pallas import tpu as pltpu

F32 = jnp.float32
BF16 = jnp.bfloat16

D_MODEL = 2048
CHUNK = 128
A_GROUPS = 8
A_GROUP_DIM = 128
A_WIDTH = A_GROUPS * A_GROUP_DIM
N_HEADS = 8
HEAD_DIM = 128
N_KV = 2
Q_WIDTH = N_HEADS * HEAD_DIM
KV_WIDTH = N_KV * HEAD_DIM
IDX_HEADS = 16
IDX_DIM = 64
TOPK_MAX = 256
D_FF = 5632
CONV_W = 3
EPS = 1e-6

LANES = 128
VMEM_LIMIT = 56 * 1024 * 1024

PROJ_TN = 1024
COL_GA, COL_GB, COL_U, COL_VA, COL_Q, COL_MISC, COL_QI = 0, 2048, 4096, 5120, 6144, 7168, 8192
PROJ_COLS = 9216
MISC_K, MISC_V, MISC_KI_EVEN, MISC_KI_ODD, MISC_WI = 0, 256, 512, 640, 768

ATT_T = 128
ATT_CH = 256
ATT_SPAN = 256
NEG_BIG = -1e30
INT_MIN = -(2 ** 31)
KEY_NEG_INF = -2139095041
MOST_NEG_F32 = -3.4028234663852886e38


def _dot(a, b):
    return jnp.dot(a, b, preferred_element_type=F32)


def _dot_nt(a, b):
    return lax.dot_general(a, b, (((1,), (1,)), ((), ())), preferred_element_type=F32)


def _ada_kernel(c_ref, w_ref, b_ref, o_ref):
    cs = jax.nn.silu(c_ref[...]).astype(BF16)
    o_ref[...] = _dot(cs, w_ref[...].astype(BF16)) + b_ref[...]


def _ada(c, w_ada, b_ada):
    bsz = c.shape[0]
    n = w_ada.shape[1]
    tn = 1024
    return pl.pallas_call(
        _ada_kernel,
        out_shape=jax.ShapeDtypeStruct((bsz, n), F32),
        grid=(n // tn,),
        in_specs=[
            pl.BlockSpec((bsz, D_MODEL), lambda j: (0, 0)),
            pl.BlockSpec((D_MODEL, tn), lambda j: (0, j)),
            pl.BlockSpec((1, tn), lambda j: (0, j)),
        ],
        out_specs=pl.BlockSpec((bsz, tn), lambda j: (0, j)),
        compiler_params=pltpu.CompilerParams(
            dimension_semantics=("arbitrary",), vmem_limit_bytes=VMEM_LIMIT),
        name="ada_mod",
    )(c, w_ada, b_ada.reshape(1, n))


def _head_rms(x, g, scale):
    r = lax.rsqrt(jnp.mean(x * x, axis=-1, keepdims=True) + EPS)
    y = x * r * g
    return y if scale is None else y * scale


def _proj_kernel(x_ref, sc_ref, sh_ref, g_ref, w_ref, vg_ref, qg_ref, kg_ref, o_ref, h_ref):
    j = pl.program_id(1)

    def acc():
        return _dot(h_ref[...], w_ref[...])

    @pl.when(j == 0)
    def _():
        x = x_ref[...]
        r = lax.rsqrt(jnp.mean(x * x, axis=-1, keepdims=True) + EPS)
        h = ((x * r * g_ref[...]) * (1.0 + sc_ref[...]) + sh_ref[...]).astype(BF16)
        h_ref[...] = h
        o_ref[...] = jax.nn.sigmoid(_dot(h, w_ref[...])).astype(BF16)

    @pl.when((j > 0) & (j < COL_U // PROJ_TN))
    def _():
        o_ref[...] = jax.nn.sigmoid(acc()).astype(BF16)

    @pl.when(j == COL_U // PROJ_TN)
    def _():
        o_ref[...] = jax.nn.gelu(acc()).astype(BF16)

    @pl.when(j == COL_VA // PROJ_TN)
    def _():
        v = jax.nn.gelu(acc())
        r = lax.rsqrt(jnp.mean(v * v, axis=-1, keepdims=True) + EPS)
        o_ref[...] = (v * r * vg_ref[...]).astype(BF16)

    @pl.when(j == COL_Q // PROJ_TN)
    def _():
        qscale = (HEAD_DIM ** -0.5) * math.log2(math.e)
        a = acc()
        for h in range(N_HEADS):
            sl = slice(h * HEAD_DIM, (h + 1) * HEAD_DIM)
            o_ref[:, sl] = _head_rms(a[:, sl], qg_ref[...], qscale).astype(BF16)

    @pl.when(j == COL_MISC // PROJ_TN)
    def _():
        a = acc()
        for h in range(N_KV):
            sl = slice(MISC_K + h * HEAD_DIM, MISC_K + (h + 1) * HEAD_DIM)
            o_ref[:, sl] = _head_rms(a[:, sl], kg_ref[...], None).astype(BF16)
        o_ref[:, MISC_V:] = a[:, MISC_V:].astype(BF16)

    @pl.when(j == COL_QI // PROJ_TN)
    def _():
        o_ref[...] = acc().astype(BF16)


def _proj(x2, sc1, sh1, norm1_g, w_cat, v_norm_g, q_norm_g, k_norm_g, seq):
    n = x2.shape[0]
    tm = 512
    tiles_per_batch = seq // tm
    mod_spec = pl.BlockSpec((None, 1, D_MODEL), lambda i, j: (i // tiles_per_batch, 0, 0))
    return pl.pallas_call(
        _proj_kernel,
        out_shape=jax.ShapeDtypeStruct((n, PROJ_COLS), BF16),
        grid=(n // tm, PROJ_COLS // PROJ_TN),
        in_specs=[
            pl.BlockSpec((tm, D_MODEL), lambda i, j: (i, 0)),
            mod_spec, mod_spec,
            pl.BlockSpec((1, D_MODEL), lambda i, j: (0, 0)),
            pl.BlockSpec((D_MODEL, PROJ_TN), lambda i, j: (0, j)),
            pl.BlockSpec((1, A_WIDTH), lambda i, j: (0, 0)),
            pl.BlockSpec((1, HEAD_DIM), lambda i, j: (0, 0)),
            pl.BlockSpec((1, HEAD_DIM), lambda i, j: (0, 0)),
        ],
        out_specs=pl.BlockSpec((tm, PROJ_TN), lambda i, j: (i, j)),
        scratch_shapes=[pltpu.VMEM((tm, D_MODEL), BF16)],
        compiler_params=pltpu.CompilerParams(
            dimension_semantics=("parallel", "arbitrary"), vmem_limit_bytes=VMEM_LIMIT),
        name="proj_in",
    )(x2, sc1, sh1, norm1_g.reshape(1, D_MODEL), w_cat, v_norm_g.reshape(1, A_WIDTH),
      q_norm_g.reshape(1, HEAD_DIM), k_norm_g.reshape(1, HEAD_DIM))


def _sublane_tree(x, op):
    r, c = x.shape
    x = x.reshape(r // 64, 8, 8, c)
    y = x[0]
    for t in range(1, r // 64):
        y = op(y, x[t])
    z = op(op(y[0], y[1]), op(y[2], y[3]))
    return op(z, op(op(y[4], y[5]), op(y[6], y[7])))


def _attn_block(s_eff, i, q_ref, misc_ref, wi_ref, qi_ref, o_ref, vt_ref, score_ref, bias_ref, s_ref, p_ref, topk):
    nch = s_eff // ATT_CH

    idx_scale = IDX_DIM ** -0.5 * IDX_HEADS ** -0.5
    w_t = wi_ref[...].astype(F32).T * idx_scale
    rhs = [jnp.concatenate([qi_ref[:, 256 * r:256 * r + 128], qi_ref[:, 256 * r + 128:256 * r + 256]], axis=0)
           for r in range(IDX_HEADS // 4)]
    q_pos = i * ATT_T + lax.broadcasted_iota(jnp.int32, (ATT_CH, ATT_T), 1)
    row_iota = lax.broadcasted_iota(jnp.int32, (ATT_CH, ATT_T), 0)

    for c in range(nch):
        rows = slice(c * ATT_CH, (c + 1) * ATT_CH)
        k_even = misc_ref[rows, MISC_KI_EVEN:MISC_KI_EVEN + LANES]
        k_odd = misc_ref[rows, MISC_KI_ODD:MISC_KI_ODD + LANES]
        acc = [None, None]
        for r in range(IDX_HEADS // 4):
            l_even = _dot_nt(k_even, rhs[r])
            l_odd = _dot_nt(k_odd, rhs[r])
            terms = (w_t[4 * r:4 * r + 1] * jnp.maximum(l_even[:, :ATT_T], 0.0),
                     w_t[4 * r + 1:4 * r + 2] * jnp.maximum(l_odd[:, :ATT_T], 0.0),
                     w_t[4 * r + 2:4 * r + 3] * jnp.maximum(l_even[:, ATT_T:], 0.0),
                     w_t[4 * r + 3:4 * r + 4] * jnp.maximum(l_odd[:, ATT_T:], 0.0))
            for t, term in enumerate(terms):
                acc[t % 2] = term if acc[t % 2] is None else acc[t % 2] + term
        score = acc[0] + acc[1]
        if (c + 1) * ATT_CH > s_eff - ATT_SPAN:
            score = jnp.where(c * ATT_CH + row_iota <= q_pos, score, -jnp.inf)
        score_ref[rows, :] = score

    def count_ge(thr_f):
        tot = None
        for c in range(nch):
            ones = jnp.where(score_ref[c * ATT_CH:(c + 1) * ATT_CH, :] >= thr_f, 1.0, 0.0)
            part = _sublane_tree(ones, jnp.add)
            tot = part if tot is None else tot + part
        return jnp.sum(tot, axis=0, keepdims=True)

    def code_to_float(code):
        key = code ^ INT_MIN
        return lax.bitcast_convert_type(key ^ ((key >> 31) & 0x7FFFFFFF), F32)

    def bs_body(it, carry):
        code, n_ge = carry
        cand = code | lax.shift_left(jnp.int32(1), 31 - it)
        cnt = count_ge(code_to_float(cand))
        ok = cnt >= float(topk)
        return jnp.where(ok, cand, code), jnp.where(ok, cnt, n_ge)

    code, n_ge = lax.fori_loop(0, 32, bs_body, (jnp.zeros((1, ATT_T), jnp.int32),
                                                jnp.full((1, ATT_T), float(s_eff), F32)))
    thr_f = code_to_float(jnp.maximum(code ^ INT_MIN, KEY_NEG_INF + 1) ^ INT_MIN)
    excess = jnp.max(jnp.where(thr_f > MOST_NEG_F32, n_ge - float(topk), 0.0))

    @pl.when(excess <= 0.0)
    def _():
        for c in range(nch):
            rows = slice(c * ATT_CH, (c + 1) * ATT_CH)
            bias_ref[rows, :] = jnp.where(score_ref[rows, :] >= thr_f, 0.0, NEG_BIG)

    @pl.when(excess > 0.0)
    def _():
        n_tie = None
        for c in range(nch):
            ones = jnp.where(score_ref[c * ATT_CH:(c + 1) * ATT_CH, :] == thr_f, 1.0, 0.0)
            part = _sublane_tree(ones, jnp.add)
            n_tie = part if n_tie is None else n_tie + part
        n_tie = jnp.sum(n_tie, axis=0, keepdims=True)
        need = float(topk) - (n_ge - n_tie)
        lower = (lax.broadcasted_iota(jnp.int32, (ATT_CH, ATT_CH), 1)
                 <= lax.broadcasted_iota(jnp.int32, (ATT_CH, ATT_CH), 0))
        tril = jnp.where(lower, 1.0, 0.0).astype(BF16)
        run = jnp.zeros((1, ATT_T), F32)
        for c in range(nch):
            rows = slice(c * ATT_CH, (c + 1) * ATT_CH)
            sc = score_ref[rows, :]
            tie = jnp.where(sc == thr_f, 1.0, 0.0)
            rank = _dot(tril, tie.astype(BF16)) + run
            keep = (sc > thr_f) | ((sc == thr_f) & (rank <= need))
            bias_ref[rows, :] = jnp.where(keep, 0.0, NEG_BIG)
            run = run + jnp.sum(_sublane_tree(tie, jnp.add), axis=0, keepdims=True)

    for p in range(N_HEADS // 2):
        g = (2 * p) // (N_HEADS // N_KV)
        q_pair = jnp.concatenate([q_ref[:, 256 * p:256 * p + 128], q_ref[:, 256 * p + 128:256 * p + 256]], axis=0)
        mx = None
        for c in range(nch):
            rows = slice(c * ATT_CH, (c + 1) * ATT_CH)
            bias = bias_ref[rows, :]
            s = _dot_nt(misc_ref[rows, MISC_K + g * HEAD_DIM:MISC_K + (g + 1) * HEAD_DIM], q_pair)
            s = s + jnp.concatenate([bias, bias], axis=1)
            s_ref[rows, :] = s
            part = _sublane_tree(s, jnp.maximum)
            mx = part if mx is None else jnp.maximum(mx, part)
        m = jnp.max(mx, axis=0, keepdims=True)
        ls = None
        for c in range(nch):
            rows = slice(c * ATT_CH, (c + 1) * ATT_CH)
            pm = jnp.exp2(s_ref[rows, :] - m)
            p_ref[rows, :] = pm.astype(BF16)
            part = _sublane_tree(pm, jnp.add)
            ls = part if ls is None else ls + part
        l = jnp.sum(ls, axis=0, keepdims=True)
        out = _dot(vt_ref[g, :, 0:s_eff], p_ref[0:s_eff, :]) / l
        o_ref[:, 256 * p:256 * p + 128] = out[:, :ATT_T].T.astype(BF16)
        o_ref[:, 256 * p + 128:256 * p + 256] = out[:, ATT_T:].T.astype(BF16)


def _attn_kernel(q_ref, misc_ref, wi_ref, qi_ref, o_ref, vt_ref, score_ref, bias_ref, s_ref, p_ref, *, seq, topk):
    i = pl.program_id(1)

    @pl.when(i == 0)
    def _():
        for g in range(N_KV):
            for c in range(seq // ATT_CH):
                rows = slice(c * ATT_CH, (c + 1) * ATT_CH)
                vg = misc_ref[rows, MISC_V + g * HEAD_DIM:MISC_V + (g + 1) * HEAD_DIM]
                vt_ref[g, :, rows] = vg.astype(F32).T.astype(BF16)

    n_span = (i * ATT_T + ATT_T + ATT_SPAN - 1) // ATT_SPAN
    for ns in range(1, seq // ATT_SPAN + 1):
        @pl.when(n_span == ns)
        def _(ns=ns):
            _attn_block(ns * ATT_SPAN, i, q_ref, misc_ref, wi_ref, qi_ref, o_ref, vt_ref, score_ref, bias_ref,
                        s_ref, p_ref, topk)


def _attention(proj, bsz, seq):
    n = proj.shape[0]
    nblk = seq // ATT_T
    topk = min(TOPK_MAX, seq // 4)
    kern = functools.partial(_attn_kernel, seq=seq, topk=topk)
    return pl.pallas_call(
        kern,
        out_shape=jax.ShapeDtypeStruct((n, Q_WIDTH), BF16),
        grid=(bsz, nblk),
        in_specs=[
            pl.BlockSpec((ATT_T, Q_WIDTH), lambda b, i: (b * nblk + i, COL_Q // Q_WIDTH)),
            pl.BlockSpec((seq, PROJ_TN), lambda b, i: (b, COL_MISC // PROJ_TN)),
            pl.BlockSpec((ATT_T, LANES), lambda b, i: (b * nblk + i, (COL_MISC + MISC_WI) // LANES)),
            pl.BlockSpec((ATT_T, IDX_HEADS * IDX_DIM), lambda b, i: (b * nblk + i, COL_QI // (IDX_HEADS * IDX_DIM))),
        ],
        out_specs=pl.BlockSpec((ATT_T, Q_WIDTH), lambda b, i: (b * nblk + i, 0)),
        scratch_shapes=[
            pltpu.VMEM((N_KV, HEAD_DIM, seq), BF16),
            pltpu.VMEM((seq, ATT_T), F32),
            pltpu.VMEM((seq, ATT_T), F32),
            pltpu.VMEM((seq, 2 * ATT_T), F32),
            pltpu.VMEM((seq, 2 * ATT_T), BF16),
        ],
        compiler_params=pltpu.CompilerParams(
            dimension_semantics=("arbitrary", "arbitrary"), vmem_limit_bytes=VMEM_LIMIT),
        name="sparse_attn",
    )(proj, proj, proj, proj)


def _merge_kernel(ga_ref, gb_ref, u_ref, v_ref, yb_ref, x_ref, g1_ref, sc2_ref, sh2_ref, n2g_ref,
                  wsp_ref, bsp_ref, wpa_ref, wpb_ref, wout_ref, x1_ref, h2_ref, ya_ref, *, tm):
    row = lax.broadcasted_iota(jnp.int32, (CHUNK, CHUNK), 0)
    col = lax.broadcasted_iota(jnp.int32, (CHUNK, CHUNK), 1)
    causal = col <= row
    bsp = bsp_ref[...]
    for g in range(A_GROUPS):
        wm = jnp.where(causal, wsp_ref[g], jnp.zeros((), BF16))
        gcols = slice(g * A_GROUP_DIM, (g + 1) * A_GROUP_DIM)
        for ci in range(tm // CHUNK):
            rows = slice(ci * CHUNK, (ci + 1) * CHUNK)
            mixed = _dot(wm, v_ref[rows, gcols]) + bsp[:, g:g + 1]
            ya_ref[rows, gcols] = (u_ref[rows, gcols].astype(F32) * mixed).astype(BF16)

    a = _dot(ya_ref[...], wpa_ref[...])
    b = _dot(yb_ref[...], wpb_ref[...])
    merged = ga_ref[...].astype(F32) * a + gb_ref[...].astype(F32) * b
    o = _dot(merged.astype(BF16), wout_ref[...])
    x1 = x_ref[...] + g1_ref[...] * o
    x1_ref[...] = x1
    r = lax.rsqrt(jnp.mean(x1 * x1, axis=-1, keepdims=True) + EPS)
    h2 = (x1 * r * n2g_ref[...]) * (1.0 + sc2_ref[...]) + sh2_ref[...]
    h2_ref[...] = h2.astype(BF16)


def _merge(proj, yb, x2, g1, sc2, sh2, norm2_g, wsp, bsp_t, wpa, wpb, wout, seq):
    n = x2.shape[0]
    tm = 256
    tiles_per_batch = seq // tm
    mod_spec = pl.BlockSpec((None, 1, D_MODEL), lambda i: (i // tiles_per_batch, 0, 0))
    const2 = lambda i: (0, 0)
    kern = functools.partial(_merge_kernel, tm=tm)
    return pl.pallas_call(
        kern,
        out_shape=(jax.ShapeDtypeStruct((n, D_MODEL), F32), jax.ShapeDtypeStruct((n, D_MODEL), BF16)),
        grid=(n // tm,),
        in_specs=[
            pl.BlockSpec((tm, D_MODEL), lambda i: (i, COL_GA // D_MODEL)),
            pl.BlockSpec((tm, D_MODEL), lambda i: (i, COL_GB // D_MODEL)),
            pl.BlockSpec((tm, A_WIDTH), lambda i: (i, COL_U // A_WIDTH)),
            pl.BlockSpec((tm, A_WIDTH), lambda i: (i, COL_VA // A_WIDTH)),
            pl.BlockSpec((tm, Q_WIDTH), lambda i: (i, 0)),
            pl.BlockSpec((tm, D_MODEL), lambda i: (i, 0)),
            mod_spec, mod_spec, mod_spec,
            pl.BlockSpec((1, D_MODEL), const2),
            pl.BlockSpec((A_GROUPS, CHUNK, CHUNK), lambda i: (0, 0, 0)),
            pl.BlockSpec((CHUNK, A_GROUPS), const2),
            pl.BlockSpec((A_WIDTH, D_MODEL), const2, pipeline_mode=pl.Buffered(1)),
            pl.BlockSpec((Q_WIDTH, D_MODEL), const2, pipeline_mode=pl.Buffered(1)),
            pl.BlockSpec((D_MODEL, D_MODEL), const2, pipeline_mode=pl.Buffered(1)),
        ],
        out_specs=(pl.BlockSpec((tm, D_MODEL), lambda i: (i, 0)),
                   pl.BlockSpec((tm, D_MODEL), lambda i: (i, 0))),
        scratch_shapes=[pltpu.VMEM((tm, A_WIDTH), BF16)],
        compiler_params=pltpu.CompilerParams(
            dimension_semantics=("parallel",), vmem_limit_bytes=VMEM_LIMIT),
        name="merge_out",
    )(proj, proj, proj, proj, yb, x2, g1, sc2, sh2, norm2_g.reshape(1, D_MODEL), wsp, bsp_t, wpa, wpb, wout)


FFN_TM = 512
FFN_TF = 512
FFN_HALO = 16


def _ffn_kernel(h_ref, hprev_ref, wa_ref, wb_ref, cwa_ref, cwb_ref, cba_ref, cbb_ref, wd_ref, x1_ref, g2_ref,
                o_ref, hh_ref, up_ref, acc_ref, *, tiles_per_batch):
    i = pl.program_id(0)
    f = pl.program_id(1)

    @pl.when(f == 0)
    def _():
        first = (i % tiles_per_batch) == 0

        @pl.when(first)
        def _():
            hh_ref[:FFN_HALO] = jnp.zeros((FFN_HALO, D_MODEL), BF16)

        @pl.when(jnp.logical_not(first))
        def _():
            hh_ref[:FFN_HALO] = hprev_ref[...]

        hh_ref[FFN_HALO:] = h_ref[...]
        acc_ref[...] = jnp.zeros_like(acc_ref)

    hh = hh_ref[...]
    up_ref[:, :FFN_TF] = _dot(hh, wa_ref[...])
    up_ref[:, FFN_TF:] = _dot(hh, wb_ref[...])

    def conv(cols, cw_ref, cb_ref):
        y = cb_ref[...] + cw_ref[0:1] * up_ref[FFN_HALO - 2:FFN_HALO - 2 + FFN_TM, cols]
        y = y + cw_ref[1:2] * up_ref[FFN_HALO - 1:FFN_HALO - 1 + FFN_TM, cols]
        return y + cw_ref[2:3] * up_ref[FFN_HALO:FFN_HALO + FFN_TM, cols]

    a = conv(slice(0, FFN_TF), cwa_ref, cba_ref)
    b = conv(slice(FFN_TF, 2 * FFN_TF), cwb_ref, cbb_ref)
    gated = (jax.nn.silu(a) * b).astype(BF16)
    acc_ref[...] += _dot(gated, wd_ref[...])

    @pl.when(f == pl.num_programs(1) - 1)
    def _():
        o_ref[...] = x1_ref[...] + g2_ref[...] * acc_ref[...]


def _ffn(h2, x1, g2, w_up, conv_w, conv_b, w_down, seq):
    n = h2.shape[0]
    tm, tf = FFN_TM, FFN_TF
    nf = D_FF // tf
    tiles_per_batch = seq // tm
    halo_blocks = tm // FFN_HALO
    kern = functools.partial(_ffn_kernel, tiles_per_batch=tiles_per_batch)
    return pl.pallas_call(
        kern,
        out_shape=jax.ShapeDtypeStruct((n, D_MODEL), F32),
        grid=(n // tm, nf),
        in_specs=[
            pl.BlockSpec((tm, D_MODEL), lambda i, f: (i, 0)),
            pl.BlockSpec((FFN_HALO, D_MODEL), lambda i, f: (jnp.maximum(i * halo_blocks - 1, 0), 0)),
            pl.BlockSpec((D_MODEL, tf), lambda i, f: (0, f)),
            pl.BlockSpec((D_MODEL, tf), lambda i, f: (0, f + nf)),
            pl.BlockSpec((CONV_W, tf), lambda i, f: (0, f)),
            pl.BlockSpec((CONV_W, tf), lambda i, f: (0, f + nf)),
            pl.BlockSpec((1, tf), lambda i, f: (0, f)),
            pl.BlockSpec((1, tf), lambda i, f: (0, f + nf)),
            pl.BlockSpec((tf, D_MODEL), lambda i, f: (f, 0)),
            pl.BlockSpec((tm, D_MODEL), lambda i, f: (i, 0)),
            pl.BlockSpec((None, 1, D_MODEL), lambda i, f: (i // tiles_per_batch, 0, 0)),
        ],
        out_specs=pl.BlockSpec((tm, D_MODEL), lambda i, f: (i, 0)),
        scratch_shapes=[
            pltpu.VMEM((tm + FFN_HALO, D_MODEL), BF16),
            pltpu.VMEM((tm + FFN_HALO, 2 * tf), F32),
            pltpu.VMEM((tm, D_MODEL), F32),
        ],
        compiler_params=pltpu.CompilerParams(
            dimension_semantics=("parallel", "arbitrary"), vmem_limit_bytes=VMEM_LIMIT),
        name="conv_ffn",
    )(h2, h2, w_up, w_up, conv_w, conv_w, conv_b.reshape(1, 2 * D_FF), conv_b.reshape(1, 2 * D_FF),
      w_down, x1, g2)


def _fused_in_weight(w_in):
    sizes = [A_WIDTH, A_WIDTH, Q_WIDTH, KV_WIDTH, KV_WIDTH, IDX_HEADS * IDX_DIM, IDX_DIM, IDX_HEADS, D_MODEL, D_MODEL]
    offs = [0]
    for s in sizes:
        offs.append(offs[-1] + s)
    u, va, q, k, vb, qi, ki, wi, ga, gb = [w_in[:, offs[t]:offs[t + 1]] for t in range(len(sizes))]
    z = lambda ncol: jnp.zeros((D_MODEL, ncol), w_in.dtype)
    misc = jnp.concatenate([k, vb, ki, z(IDX_DIM), z(IDX_DIM), ki, wi, z(LANES - IDX_HEADS), z(LANES)], axis=1)
    w_cat = jnp.concatenate([ga, gb, u, va, q, misc, qi], axis=1)
    assert w_cat.shape[1] == PROJ_COLS and misc.shape[1] == PROJ_TN
    return w_cat.astype(BF16)


def kernel(x, c, w_ada, b_ada, norm1_g, w_in, v_norm_g, w_spatial, b_spatial, q_norm_g, k_norm_g, w_proj_a,
           w_proj_b, w_out, norm2_g, w_up, conv_w, conv_b, w_down):
    bsz, seq, _ = x.shape
    n = bsz * seq
    x2 = x.reshape(n, D_MODEL)

    mod = _ada(c, w_ada, b_ada)
    sh1, sc1, g1, sh2, sc2, g2 = [mod[:, t * D_MODEL:(t + 1) * D_MODEL].reshape(bsz, 1, D_MODEL) for t in range(6)]

    proj = _proj(x2, sc1, sh1, norm1_g, _fused_in_weight(w_in), v_norm_g, q_norm_g, k_norm_g, seq)
    yb = _attention(proj, bsz, seq)
    x1, h2 = _merge(proj, yb, x2, g1, sc2, sh2, norm2_g, w_spatial.astype(BF16), b_spatial.T,
                    w_proj_a.astype(BF16), w_proj_b.astype(BF16), w_out.astype(BF16), seq)
    out = _ffn(h2, x1, g2, w_up.astype(BF16), conv_w, conv_b, w_down.astype(BF16), seq)
    return out.reshape(bsz, seq, D_MODEL)
```

```python
import functools
import math

import jax
import jax.numpy as jnp
from jax import lax
from jax.experimental import pallas as pl
from jax.experimental.pallas import tpu as pltpu

F32 = jnp.float32
BF16 = jnp.bfloat16

D_MODEL = 2048
CHUNK = 128
A_GROUPS = 8
A_GROUP_DIM = 128
A_WIDTH = A_GROUPS * A_GROUP_DIM
N_HEADS = 8
HEAD_DIM = 128
N_KV = 2
Q_WIDTH = N_HEADS * HEAD_DIM
KV_WIDTH = N_KV * HEAD_DIM
IDX_HEADS = 16
IDX_DIM = 64
TOPK_MAX = 256
D_FF = 5632
CONV_W = 3
EPS = 1e-6

LANES = 128
VMEM_LIMIT = 56 * 1024 * 1024

PROJ_TN = 1024
COL_GA, COL_GB, COL_U, COL_VA, COL_Q, COL_MISC, COL_QI = 0, 2048, 4096, 5120, 6144, 7168, 8192
PROJ_COLS = 9216
MISC_K, MISC_V, MISC_KI_EVEN, MISC_KI_ODD, MISC_WI = 0, 256, 512, 640, 768

ATT_T = 128
ATT_CH = 256
ATT_SPAN = 256
ATT_ONES_ROWS = 16
ATT_SAFE_SHIFT = 40.0
ATT_BOUND_SLACK = 1.02
NEG_BIG = -1e30
INT_MIN = -(2 ** 31)
KEY_NEG_INF = -2139095041
MOST_NEG_F32 = -3.4028234663852886e38


def _dot(a, b):
    return jnp.dot(a, b, preferred_element_type=F32)


def _dot_nt(a, b):
    return lax.dot_general(a, b, (((1,), (1,)), ((), ())), preferred_element_type=F32)


def _ada_kernel(c_ref, w_ref, b_ref, o_ref):
    cs = jax.nn.silu(c_ref[...]).astype(BF16)
    o_ref[...] = _dot(cs, w_ref[...].astype(BF16)) + b_ref[...]


def _ada(c, w_ada, b_ada):
    bsz = c.shape[0]
    n = w_ada.shape[1]
    tn = 1024
    return pl.pallas_call(
        _ada_kernel,
        out_shape=jax.ShapeDtypeStruct((bsz, n), F32),
        grid=(n // tn,),
        in_specs=[
            pl.BlockSpec((bsz, D_MODEL), lambda j: (0, 0)),
            pl.BlockSpec((D_MODEL, tn), lambda j: (0, j)),
            pl.BlockSpec((1, tn), lambda j: (0, j)),
        ],
        out_specs=pl.BlockSpec((bsz, tn), lambda j: (0, j)),
        compiler_params=pltpu.CompilerParams(
            dimension_semantics=("arbitrary",), vmem_limit_bytes=VMEM_LIMIT),
        name="ada_mod",
    )(c, w_ada, b_ada.reshape(1, n))


def _head_rms(x, g, scale):
    r = lax.rsqrt(jnp.mean(x * x, axis=-1, keepdims=True) + EPS)
    y = x * r * g
    return y if scale is None else y * scale


def _proj_kernel(x_ref, sc_ref, sh_ref, g_ref, w_ref, vg_ref, qg_ref, kg_ref, o_ref, h_ref):
    j = pl.program_id(1)

    def acc():
        return _dot(h_ref[...], w_ref[...])

    @pl.when(j == 0)
    def _():
        x = x_ref[...]
        r = lax.rsqrt(jnp.mean(x * x, axis=-1, keepdims=True) + EPS)
        h = ((x * r * g_ref[...]) * (1.0 + sc_ref[...]) + sh_ref[...]).astype(BF16)
        h_ref[...] = h
        o_ref[...] = jax.nn.sigmoid(_dot(h, w_ref[...])).astype(BF16)

    @pl.when((j > 0) & (j < COL_U // PROJ_TN))
    def _():
        o_ref[...] = jax.nn.sigmoid(acc()).astype(BF16)

    @pl.when(j == COL_U // PROJ_TN)
    def _():
        o_ref[...] = jax.nn.gelu(acc()).astype(BF16)

    @pl.when(j == COL_VA // PROJ_TN)
    def _():
        v = jax.nn.gelu(acc())
        r = lax.rsqrt(jnp.mean(v * v, axis=-1, keepdims=True) + EPS)
        o_ref[...] = (v * r * vg_ref[...]).astype(BF16)

    @pl.when(j == COL_Q // PROJ_TN)
    def _():
        qscale = (HEAD_DIM ** -0.5) * math.log2(math.e)
        a = acc()
        for h in range(N_HEADS):
            sl = slice(h * HEAD_DIM, (h + 1) * HEAD_DIM)
            o_ref[:, sl] = _head_rms(a[:, sl], qg_ref[...], qscale).astype(BF16)

    @pl.when(j == COL_MISC // PROJ_TN)
    def _():
        a = acc()
        for h in range(N_KV):
            sl = slice(MISC_K + h * HEAD_DIM, MISC_K + (h + 1) * HEAD_DIM)
            o_ref[:, sl] = _head_rms(a[:, sl], kg_ref[...], None).astype(BF16)
        o_ref[:, MISC_V:] = a[:, MISC_V:].astype(BF16)

    @pl.when(j == COL_QI // PROJ_TN)
    def _():
        o_ref[...] = acc().astype(BF16)


def _proj(x2, sc1, sh1, norm1_g, w_cat, v_norm_g, q_norm_g, k_norm_g, seq):
    n = x2.shape[0]
    tm = 1024
    tiles_per_batch = seq // tm
    mod_spec = pl.BlockSpec((None, 1, D_MODEL), lambda i, j: (i // tiles_per_batch, 0, 0))
    return pl.pallas_call(
        _proj_kernel,
        out_shape=jax.ShapeDtypeStruct((n, PROJ_COLS), BF16),
        grid=(n // tm, PROJ_COLS // PROJ_TN),
        in_specs=[
            pl.BlockSpec((tm, D_MODEL), lambda i, j: (i, 0)),
            mod_spec, mod_spec,
            pl.BlockSpec((1, D_MODEL), lambda i, j: (0, 0)),
            pl.BlockSpec((D_MODEL, PROJ_TN), lambda i, j: (0, j)),
            pl.BlockSpec((1, A_WIDTH), lambda i, j: (0, 0)),
            pl.BlockSpec((1, HEAD_DIM), lambda i, j: (0, 0)),
            pl.BlockSpec((1, HEAD_DIM), lambda i, j: (0, 0)),
        ],
        out_specs=pl.BlockSpec((tm, PROJ_TN), lambda i, j: (i, j)),
        scratch_shapes=[pltpu.VMEM((tm, D_MODEL), BF16)],
        compiler_params=pltpu.CompilerParams(
            dimension_semantics=("parallel", "arbitrary"), vmem_limit_bytes=VMEM_LIMIT),
        name="proj_in",
    )(x2, sc1, sh1, norm1_g.reshape(1, D_MODEL), w_cat, v_norm_g.reshape(1, A_WIDTH),
      q_norm_g.reshape(1, HEAD_DIM), k_norm_g.reshape(1, HEAD_DIM))


def _sublane_tree(x, op):
    r, c = x.shape
    x = x.reshape(r // 64, 8, 8, c)
    y = x[0]
    for t in range(1, r // 64):
        y = op(y, x[t])
    z = op(op(y[0], y[1]), op(y[2], y[3]))
    return op(z, op(op(y[4], y[5]), op(y[6], y[7])))


def _attn_block(s_eff, i, q_ref, misc_ref, wi_ref, qi_ref, o_ref, vt_ref, kmax_ref, score_ref, bias_ref, s_ref, p_ref,
                topk):
    nch = s_eff // ATT_CH

    idx_scale = IDX_DIM ** -0.5 * IDX_HEADS ** -0.5
    w_t = wi_ref[...].astype(F32).T * idx_scale
    rhs = [jnp.concatenate([qi_ref[:, 256 * r:256 * r + 128], qi_ref[:, 256 * r + 128:256 * r + 256]], axis=0)
           for r in range(IDX_HEADS // 4)]
    q_pos = i * ATT_T + lax.broadcasted_iota(jnp.int32, (ATT_CH, ATT_T), 1)
    row_iota = lax.broadcasted_iota(jnp.int32, (ATT_CH, ATT_T), 0)

    for c in range(nch):
        rows = slice(c * ATT_CH, (c + 1) * ATT_CH)
        k_even = misc_ref[rows, MISC_KI_EVEN:MISC_KI_EVEN + LANES]
        k_odd = misc_ref[rows, MISC_KI_ODD:MISC_KI_ODD + LANES]
        acc = [None, None]
        for r in range(IDX_HEADS // 4):
            l_even = _dot_nt(k_even, rhs[r])
            l_odd = _dot_nt(k_odd, rhs[r])
            terms = (w_t[4 * r:4 * r + 1] * jnp.maximum(l_even[:, :ATT_T], 0.0),
                     w_t[4 * r + 1:4 * r + 2] * jnp.maximum(l_odd[:, :ATT_T], 0.0),
                     w_t[4 * r + 2:4 * r + 3] * jnp.maximum(l_even[:, ATT_T:], 0.0),
                     w_t[4 * r + 3:4 * r + 4] * jnp.maximum(l_odd[:, ATT_T:], 0.0))
            for t, term in enumerate(terms):
                acc[t % 2] = term if acc[t % 2] is None else acc[t % 2] + term
        score = acc[0] + acc[1]
        if (c + 1) * ATT_CH > s_eff - ATT_SPAN:
            score = jnp.where(c * ATT_CH + row_iota <= q_pos, score, -jnp.inf)
        score_ref[rows, :] = score

    def count_ge(thr_f):
        tot = None
        for c in range(nch):
            ones = jnp.where(score_ref[c * ATT_CH:(c + 1) * ATT_CH, :] >= thr_f, 1.0, 0.0)
            part = _sublane_tree(ones, jnp.add)
            tot = part if tot is None else tot + part
        return jnp.sum(tot, axis=0, keepdims=True)

    def code_to_float(code):
        key = code ^ INT_MIN
        return lax.bitcast_convert_type(key ^ ((key >> 31) & 0x7FFFFFFF), F32)

    def bs_body(it, carry):
        code, n_ge = carry
        cand = code | lax.shift_left(jnp.int32(1), 31 - it)
        cnt = count_ge(code_to_float(cand))
        ok = cnt >= float(topk)
        return jnp.where(ok, cand, code), jnp.where(ok, cnt, n_ge)

    code, n_ge = lax.fori_loop(0, 32, bs_body, (jnp.zeros((1, ATT_T), jnp.int32),
                                                jnp.full((1, ATT_T), float(s_eff), F32)))
    thr_f = code_to_float(jnp.maximum(code ^ INT_MIN, KEY_NEG_INF + 1) ^ INT_MIN)
    excess = jnp.max(jnp.where(thr_f > MOST_NEG_F32, n_ge - float(topk), 0.0))

    @pl.when(excess <= 0.0)
    def _():
        for c in range(nch):
            rows = slice(c * ATT_CH, (c + 1) * ATT_CH)
            bias_ref[rows, :] = jnp.where(score_ref[rows, :] >= thr_f, 0.0, NEG_BIG)

    @pl.when(excess > 0.0)
    def _():
        n_tie = None
        for c in range(nch):
            ones = jnp.where(score_ref[c * ATT_CH:(c + 1) * ATT_CH, :] == thr_f, 1.0, 0.0)
            part = _sublane_tree(ones, jnp.add)
            n_tie = part if n_tie is None else n_tie + part
        n_tie = jnp.sum(n_tie, axis=0, keepdims=True)
        need = float(topk) - (n_ge - n_tie)
        lower = (lax.broadcasted_iota(jnp.int32, (ATT_CH, ATT_CH), 1)
                 <= lax.broadcasted_iota(jnp.int32, (ATT_CH, ATT_CH), 0))
        tril = jnp.where(lower, 1.0, 0.0).astype(BF16)
        run = jnp.zeros((1, ATT_T), F32)
        for c in range(nch):
            rows = slice(c * ATT_CH, (c + 1) * ATT_CH)
            sc = score_ref[rows, :]
            tie = jnp.where(sc == thr_f, 1.0, 0.0)
            rank = _dot(tril, tie.astype(BF16)) + run
            keep = (sc > thr_f) | ((sc == thr_f) & (rank <= need))
            bias_ref[rows, :] = jnp.where(keep, 0.0, NEG_BIG)
            run = run + jnp.sum(_sublane_tree(tie, jnp.add), axis=0, keepdims=True)

    def head_pair(p):
        g = (2 * p) // (N_HEADS // N_KV)
        q_pair = jnp.concatenate([q_ref[:, 256 * p:256 * p + 128], q_ref[:, 256 * p + 128:256 * p + 256]], axis=0)
        return g, q_pair

    def finish(p, g, l=None):
        out = _dot(vt_ref[g, :, 0:s_eff], p_ref[0:s_eff, :])
        l = out[HEAD_DIM:HEAD_DIM + 1] if l is None else l
        out = out[:HEAD_DIM] / l
        o_ref[:, 256 * p:256 * p + 128] = out[:, :ATT_T].T.astype(BF16)
        o_ref[:, 256 * p + 128:256 * p + 256] = out[:, ATT_T:].T.astype(BF16)

    ones_rows = jnp.ones((8, HEAD_DIM), BF16)
    shifts = []
    for p in range(N_HEADS // 2):
        g, q_pair = head_pair(p)
        q2 = q_pair.astype(F32)
        qn2 = _dot_nt(ones_rows, (q2 * q2).astype(BF16))[0:1]
        kmax = kmax_ref[g, 0:1, :]
        shifts.append(jnp.sqrt(qn2) * (jnp.concatenate([kmax, kmax], axis=1) * ATT_BOUND_SLACK))
    worst_shift = jnp.max(jnp.concatenate(shifts, axis=0))

    @pl.when(worst_shift <= ATT_SAFE_SHIFT)
    def _():
        for p in range(N_HEADS // 2):
            g, q_pair = head_pair(p)
            for c in range(nch):
                rows = slice(c * ATT_CH, (c + 1) * ATT_CH)
                bias = bias_ref[rows, :]
                s = _dot_nt(misc_ref[rows, MISC_K + g * HEAD_DIM:MISC_K + (g + 1) * HEAD_DIM], q_pair)
                s = s + jnp.concatenate([bias, bias], axis=1) - shifts[p]
                p_ref[rows, :] = jnp.exp2(s).astype(BF16)
            finish(p, g)

    @pl.when(jnp.logical_not(worst_shift <= ATT_SAFE_SHIFT))
    def _():
        for p in range(N_HEADS // 2):
            g, q_pair = head_pair(p)
            mx = None
            for c in range(nch):
                rows = slice(c * ATT_CH, (c + 1) * ATT_CH)
                bias = bias_ref[rows, :]
                s = _dot_nt(misc_ref[rows, MISC_K + g * HEAD_DIM:MISC_K + (g + 1) * HEAD_DIM], q_pair)
                s = s + jnp.concatenate([bias, bias], axis=1)
                s_ref[rows, :] = s
                part = _sublane_tree(s, jnp.maximum)
                mx = part if mx is None else jnp.maximum(mx, part)
            m = jnp.max(mx, axis=0, keepdims=True)
            ls = None
            for c in range(nch):
                rows = slice(c * ATT_CH, (c + 1) * ATT_CH)
                pm = jnp.exp2(s_ref[rows, :] - m)
                p_ref[rows, :] = pm.astype(BF16)
                part = _sublane_tree(pm, jnp.add)
                ls = part if ls is None else ls + part
            finish(p, g, jnp.sum(ls, axis=0, keepdims=True))


def _attn_kernel(q_ref, misc_ref, wi_ref, qi_ref, o_ref, vt_ref, kmax_ref, score_ref, bias_ref, s_ref, p_ref,
                 *, seq, topk):
    i = pl.program_id(1)

    @pl.when(i == 0)
    def _():
        for g in range(N_KV):
            for c in range(seq // ATT_CH):
                rows = slice(c * ATT_CH, (c + 1) * ATT_CH)
                vg = misc_ref[rows, MISC_V + g * HEAD_DIM:MISC_V + (g + 1) * HEAD_DIM]
                vt_ref[g, 0:HEAD_DIM, rows] = vg.astype(F32).T.astype(BF16)
            vt_ref[g, HEAD_DIM:, :] = jnp.ones((ATT_ONES_ROWS, seq), BF16)
            kg = misc_ref[:, MISC_K + g * HEAD_DIM:MISC_K + (g + 1) * HEAD_DIM].astype(F32)
            k_norm2 = jnp.max(jnp.sum(kg * kg, axis=1, keepdims=True))
            kmax_ref[g] = jnp.full((8, LANES), jnp.sqrt(k_norm2), F32)

    n_span = (i * ATT_T + ATT_T + ATT_SPAN - 1) // ATT_SPAN
    for ns in range(1, seq // ATT_SPAN + 1):
        @pl.when(n_span == ns)
        def _(ns=ns):
            _attn_block(ns * ATT_SPAN, i, q_ref, misc_ref, wi_ref, qi_ref, o_ref, vt_ref, kmax_ref, score_ref,
                        bias_ref, s_ref, p_ref, topk)


def _attention(proj, bsz, seq):
    n = proj.shape[0]
    nblk = seq // ATT_T
    topk = min(TOPK_MAX, seq // 4)
    kern = functools.partial(_attn_kernel, seq=seq, topk=topk)
    return pl.pallas_call(
        kern,
        out_shape=jax.ShapeDtypeStruct((n, Q_WIDTH), BF16),
        grid=(bsz, nblk),
        in_specs=[
            pl.BlockSpec((ATT_T, Q_WIDTH), lambda b, i: (b * nblk + i, COL_Q // Q_WIDTH)),
            pl.BlockSpec((seq, PROJ_TN), lambda b, i: (b, COL_MISC // PROJ_TN)),
            pl.BlockSpec((ATT_T, LANES), lambda b, i: (b * nblk + i, (COL_MISC + MISC_WI) // LANES)),
            pl.BlockSpec((ATT_T, IDX_HEADS * IDX_DIM), lambda b, i: (b * nblk + i, COL_QI // (IDX_HEADS * IDX_DIM))),
        ],
        out_specs=pl.BlockSpec((ATT_T, Q_WIDTH), lambda b, i: (b * nblk + i, 0)),
        scratch_shapes=[
            pltpu.VMEM((N_KV, HEAD_DIM + ATT_ONES_ROWS, seq), BF16),
            pltpu.VMEM((N_KV, 8, LANES), F32),
            pltpu.VMEM((seq, ATT_T), F32),
            pltpu.VMEM((seq, ATT_T), F32),
            pltpu.VMEM((seq, 2 * ATT_T), F32),
            pltpu.VMEM((seq, 2 * ATT_T), BF16),
        ],
        compiler_params=pltpu.CompilerParams(
            dimension_semantics=("arbitrary", "arbitrary"), vmem_limit_bytes=VMEM_LIMIT),
        name="sparse_attn",
    )(proj, proj, proj, proj)


def _merge_kernel(ga_ref, gb_ref, u_ref, v_ref, yb_ref, x_ref, g1_ref, sc2_ref, sh2_ref, n2g_ref,
                  wsp_ref, bsp_ref, wpa_ref, wpb_ref, wout_ref, x1_ref, h2_ref, ya_ref, *, tm):
    row = lax.broadcasted_iota(jnp.int32, (CHUNK, CHUNK), 0)
    col = lax.broadcasted_iota(jnp.int32, (CHUNK, CHUNK), 1)
    causal = col <= row
    bsp = bsp_ref[...]
    for g in range(A_GROUPS):
        wm = jnp.where(causal, wsp_ref[g], jnp.zeros((), BF16))
        gcols = slice(g * A_GROUP_DIM, (g + 1) * A_GROUP_DIM)
        for ci in range(tm // CHUNK):
            rows = slice(ci * CHUNK, (ci + 1) * CHUNK)
            mixed = _dot(wm, v_ref[rows, gcols]) + bsp[:, g:g + 1]
            ya_ref[rows, gcols] = (u_ref[rows, gcols].astype(F32) * mixed).astype(BF16)

    a = _dot(ya_ref[...], wpa_ref[...])
    b = _dot(yb_ref[...], wpb_ref[...])
    merged = ga_ref[...].astype(F32) * a + gb_ref[...].astype(F32) * b
    o = _dot(merged.astype(BF16), wout_ref[...])
    x1 = x_ref[...] + g1_ref[...] * o
    x1_ref[...] = x1
    r = lax.rsqrt(jnp.mean(x1 * x1, axis=-1, keepdims=True) + EPS)
    h2 = (x1 * r * n2g_ref[...]) * (1.0 + sc2_ref[...]) + sh2_ref[...]
    h2_ref[...] = h2.astype(BF16)


def _merge(proj, yb, x2, g1, sc2, sh2, norm2_g, wsp, bsp_t, wpa, wpb, wout, seq):
    n = x2.shape[0]
    tm = 256
    tiles_per_batch = seq // tm
    mod_spec = pl.BlockSpec((None, 1, D_MODEL), lambda i: (i // tiles_per_batch, 0, 0))
    const2 = lambda i: (0, 0)
    kern = functools.partial(_merge_kernel, tm=tm)
    return pl.pallas_call(
        kern,
        out_shape=(jax.ShapeDtypeStruct((n, D_MODEL), F32), jax.ShapeDtypeStruct((n, D_MODEL), BF16)),
        grid=(n // tm,),
        in_specs=[
            pl.BlockSpec((tm, D_MODEL), lambda i: (i, COL_GA // D_MODEL)),
            pl.BlockSpec((tm, D_MODEL), lambda i: (i, COL_GB // D_MODEL)),
            pl.BlockSpec((tm, A_WIDTH), lambda i: (i, COL_U // A_WIDTH)),
            pl.BlockSpec((tm, A_WIDTH), lambda i: (i, COL_VA // A_WIDTH)),
            pl.BlockSpec((tm, Q_WIDTH), lambda i: (i, 0)),
            pl.BlockSpec((tm, D_MODEL), lambda i: (i, 0)),
            mod_spec, mod_spec, mod_spec,
            pl.BlockSpec((1, D_MODEL), const2),
            pl.BlockSpec((A_GROUPS, CHUNK, CHUNK), lambda i: (0, 0, 0)),
            pl.BlockSpec((CHUNK, A_GROUPS), const2),
            pl.BlockSpec((A_WIDTH, D_MODEL), const2, pipeline_mode=pl.Buffered(1)),
            pl.BlockSpec((Q_WIDTH, D_MODEL), const2, pipeline_mode=pl.Buffered(1)),
            pl.BlockSpec((D_MODEL, D_MODEL), const2, pipeline_mode=pl.Buffered(1)),
        ],
        out_specs=(pl.BlockSpec((tm, D_MODEL), lambda i: (i, 0)),
                   pl.BlockSpec((tm, D_MODEL), lambda i: (i, 0))),
        scratch_shapes=[pltpu.VMEM((tm, A_WIDTH), BF16)],
        compiler_params=pltpu.CompilerParams(
            dimension_semantics=("parallel",), vmem_limit_bytes=VMEM_LIMIT),
        name="merge_out",
    )(proj, proj, proj, proj, yb, x2, g1, sc2, sh2, norm2_g.reshape(1, D_MODEL), wsp, bsp_t, wpa, wpb, wout)


FFN_TM = 512
FFN_TF = 512
FFN_HALO = 16


def _ffn_kernel(h_ref, hprev_ref, wa_ref, wb_ref, cwa_ref, cwb_ref, cba_ref, cbb_ref, wd_ref, x1_ref, g2_ref,
                o_ref, hh_ref, upa_ref, upb_ref, acc_ref, *, tiles_per_batch):
    i = pl.program_id(0)
    f = pl.program_id(1)

    @pl.when(f == 0)
    def _():
        first = (i % tiles_per_batch) == 0

        @pl.when(first)
        def _():
            hh_ref[:FFN_HALO] = jnp.zeros((FFN_HALO, D_MODEL), BF16)

        @pl.when(jnp.logical_not(first))
        def _():
            hh_ref[:FFN_HALO] = hprev_ref[...]

        hh_ref[FFN_HALO:] = h_ref[...]
        acc_ref[...] = jnp.zeros_like(acc_ref)

    def conv_up(w_ref, cw_ref, cb_ref, up_ref):
        up_ref[...] = _dot(hh_ref[...], w_ref[...])
        y = cb_ref[...] + cw_ref[0:1] * up_ref[FFN_HALO - 2:FFN_HALO - 2 + FFN_TM, :]
        y = y + cw_ref[1:2] * up_ref[FFN_HALO - 1:FFN_HALO - 1 + FFN_TM, :]
        return y + cw_ref[2:3] * up_ref[FFN_HALO:FFN_HALO + FFN_TM, :]

    a = conv_up(wa_ref, cwa_ref, cba_ref, upa_ref)
    b = conv_up(wb_ref, cwb_ref, cbb_ref, upb_ref)
    gated = (jax.nn.silu(a) * b).astype(BF16)
    acc_ref[...] += _dot(gated, wd_ref[...])

    @pl.when(f == pl.num_programs(1) - 1)
    def _():
        o_ref[...] = x1_ref[...] + g2_ref[...] * acc_ref[...]


def _ffn(h2, x1, g2, w_up, conv_w, conv_b, w_down, seq):
    n = h2.shape[0]
    tm, tf = FFN_TM, FFN_TF
    nf = D_FF // tf
    tiles_per_batch = seq // tm
    halo_blocks = tm // FFN_HALO
    kern = functools.partial(_ffn_kernel, tiles_per_batch=tiles_per_batch)
    return pl.pallas_call(
        kern,
        out_shape=jax.ShapeDtypeStruct((n, D_MODEL), F32),
        grid=(n // tm, nf),
        in_specs=[
            pl.BlockSpec((tm, D_MODEL), lambda i, f: (i, 0)),
            pl.BlockSpec((FFN_HALO, D_MODEL), lambda i, f: (jnp.maximum(i * halo_blocks - 1, 0), 0)),
            pl.BlockSpec((D_MODEL, tf), lambda i, f: (0, f)),
            pl.BlockSpec((D_MODEL, tf), lambda i, f: (0, f + nf)),
            pl.BlockSpec((CONV_W, tf), lambda i, f: (0, f)),
            pl.BlockSpec((CONV_W, tf), lambda i, f: (0, f + nf)),
            pl.BlockSpec((1, tf), lambda i, f: (0, f)),
            pl.BlockSpec((1, tf), lambda i, f: (0, f + nf)),
            pl.BlockSpec((tf, D_MODEL), lambda i, f: (f, 0)),
            pl.BlockSpec((tm, D_MODEL), lambda i, f: (i, 0)),
            pl.BlockSpec((None, 1, D_MODEL), lambda i, f: (i // tiles_per_batch, 0, 0)),
        ],
        out_specs=pl.BlockSpec((tm, D_MODEL), lambda i, f: (i, 0)),
        scratch_shapes=[
            pltpu.VMEM((tm + FFN_HALO, D_MODEL), BF16),
            pltpu.VMEM((tm + FFN_HALO, tf), F32),
            pltpu.VMEM((tm + FFN_HALO, tf), F32),
            pltpu.VMEM((tm, D_MODEL), F32),
        ],
        compiler_params=pltpu.CompilerParams(
            dimension_semantics=("parallel", "arbitrary"), vmem_limit_bytes=VMEM_LIMIT),
        name="conv_ffn",
    )(h2, h2, w_up, w_up, conv_w, conv_w, conv_b.reshape(1, 2 * D_FF), conv_b.reshape(1, 2 * D_FF),
      w_down, x1, g2)


def _fused_in_weight(w_in):
    sizes = [A_WIDTH, A_WIDTH, Q_WIDTH, KV_WIDTH, KV_WIDTH, IDX_HEADS * IDX_DIM, IDX_DIM, IDX_HEADS, D_MODEL, D_MODEL]
    offs = [0]
    for s in sizes:
        offs.append(offs[-1] + s)
    u, va, q, k, vb, qi, ki, wi, ga, gb = [w_in[:, offs[t]:offs[t + 1]] for t in range(len(sizes))]
    z = lambda ncol: jnp.zeros((D_MODEL, ncol), w_in.dtype)
    misc = jnp.concatenate([k, vb, ki, z(IDX_DIM), z(IDX_DIM), ki, wi, z(LANES - IDX_HEADS), z(LANES)], axis=1)
    w_cat = jnp.concatenate([ga, gb, u, va, q, misc, qi], axis=1)
    assert w_cat.shape[1] == PROJ_COLS and misc.shape[1] == PROJ_TN
    return w_cat.astype(BF16)


def kernel(x, c, w_ada, b_ada, norm1_g, w_in, v_norm_g, w_spatial, b_spatial, q_norm_g, k_norm_g, w_proj_a,
           w_proj_b, w_out, norm2_g, w_up, conv_w, conv_b, w_down):
    bsz, seq, _ = x.shape
    n = bsz * seq
    x2 = x.reshape(n, D_MODEL)

    mod = _ada(c, w_ada, b_ada)
    sh1, sc1, g1, sh2, sc2, g2 = [mod[:, t * D_MODEL:(t + 1) * D_MODEL].reshape(bsz, 1, D_MODEL) for t in range(6)]

    proj = _proj(x2, sc1, sh1, norm1_g, _fused_in_weight(w_in), v_norm_g, q_norm_g, k_norm_g, seq)
    yb = _attention(proj, bsz, seq)
    x1, h2 = _merge(proj, yb, x2, g1, sc2, sh2, norm2_g, w_spatial.astype(BF16), b_spatial.T,
                    w_proj_a.astype(BF16), w_proj_b.astype(BF16), w_out.astype(BF16), seq)
    out = _ffn(h2, x1, g2, w_up.astype(BF16), conv_w, conv_b, w_down.astype(BF16), seq)
    return out.reshape(bsz, seq, D_MODEL)
```

```python
import functools
import math

import jax
import jax.numpy as jnp
from jax import lax
from jax.experimental import pallas as pl
from jax.experimental.pallas import tpu as pltpu

F32 = jnp.float32
BF16 = jnp.bfloat16

D_MODEL = 2048
CHUNK = 128
A_GROUPS = 8
A_GROUP_DIM = 128
A_WIDTH = A_GROUPS * A_GROUP_DIM
N_HEADS = 8
HEAD_DIM = 128
N_KV = 2
Q_WIDTH = N_HEADS * HEAD_DIM
KV_WIDTH = N_KV * HEAD_DIM
IDX_HEADS = 16
IDX_DIM = 64
TOPK_MAX = 256
D_FF = 5632
CONV_W = 3
EPS = 1e-6

LANES = 128
VMEM_LIMIT = 56 * 1024 * 1024

PROJ_TN = 1024
COL_GA, COL_GB, COL_U, COL_VA, COL_Q, COL_MISC, COL_QI = 0, 2048, 4096, 5120, 6144, 7168, 8192
PROJ_COLS = 9216
MISC_K, MISC_V, MISC_KI_EVEN, MISC_KI_ODD, MISC_WI = 0, 256, 512, 640, 768

ATT_T = 128
ATT_CH = 256
ATT_SPAN = 256
ATT_ONES_ROWS = 16
ATT_SAFE_SHIFT = 40.0
ATT_BOUND_SLACK = 1.02
NEG_BIG = -1e30
INT_MIN = -(2 ** 31)
KEY_NEG_INF = -2139095041
MOST_NEG_F32 = -3.4028234663852886e38


def _dot(a, b):
    return jnp.dot(a, b, preferred_element_type=F32)


def _dot_nt(a, b):
    return lax.dot_general(a, b, (((1,), (1,)), ((), ())), preferred_element_type=F32)


def _ada_kernel(c_ref, w_ref, b_ref, o_ref):
    cs = jax.nn.silu(c_ref[...]).astype(BF16)
    o_ref[...] = _dot(cs, w_ref[...].astype(BF16)) + b_ref[...]


def _ada(c, w_ada, b_ada):
    bsz = c.shape[0]
    n = w_ada.shape[1]
    tn = 1024
    return pl.pallas_call(
        _ada_kernel,
        out_shape=jax.ShapeDtypeStruct((bsz, n), F32),
        grid=(n // tn,),
        in_specs=[
            pl.BlockSpec((bsz, D_MODEL), lambda j: (0, 0)),
            pl.BlockSpec((D_MODEL, tn), lambda j: (0, j)),
            pl.BlockSpec((1, tn), lambda j: (0, j)),
        ],
        out_specs=pl.BlockSpec((bsz, tn), lambda j: (0, j)),
        compiler_params=pltpu.CompilerParams(
            dimension_semantics=("arbitrary",), vmem_limit_bytes=VMEM_LIMIT),
        name="ada_mod",
    )(c, w_ada, b_ada.reshape(1, n))


def _head_rms(x, g, scale):
    r = lax.rsqrt(jnp.mean(x * x, axis=-1, keepdims=True) + EPS)
    y = x * r * g
    return y if scale is None else y * scale


def _proj_kernel(x_ref, sc_ref, sh_ref, g_ref, w_ref, vg_ref, qg_ref, kg_ref, o_ref, h_ref):
    j = pl.program_id(1)

    def acc():
        return _dot(h_ref[...], w_ref[...])

    @pl.when(j == 0)
    def _():
        x = x_ref[...]
        r = lax.rsqrt(jnp.mean(x * x, axis=-1, keepdims=True) + EPS)
        h = ((x * r * g_ref[...]) * (1.0 + sc_ref[...]) + sh_ref[...]).astype(BF16)
        h_ref[...] = h
        o_ref[...] = jax.nn.sigmoid(_dot(h, w_ref[...])).astype(BF16)

    @pl.when((j > 0) & (j < COL_U // PROJ_TN))
    def _():
        o_ref[...] = jax.nn.sigmoid(acc()).astype(BF16)

    @pl.when(j == COL_U // PROJ_TN)
    def _():
        o_ref[...] = jax.nn.gelu(acc()).astype(BF16)

    @pl.when(j == COL_VA // PROJ_TN)
    def _():
        v = jax.nn.gelu(acc())
        r = lax.rsqrt(jnp.mean(v * v, axis=-1, keepdims=True) + EPS)
        o_ref[...] = (v * r * vg_ref[...]).astype(BF16)

    @pl.when(j == COL_Q // PROJ_TN)
    def _():
        qscale = (HEAD_DIM ** -0.5) * math.log2(math.e)
        a = acc()
        for h in range(N_HEADS):
            sl = slice(h * HEAD_DIM, (h + 1) * HEAD_DIM)
            o_ref[:, sl] = _head_rms(a[:, sl], qg_ref[...], qscale).astype(BF16)

    @pl.when(j == COL_MISC // PROJ_TN)
    def _():
        a = acc()
        for h in range(N_KV):
            sl = slice(MISC_K + h * HEAD_DIM, MISC_K + (h + 1) * HEAD_DIM)
            o_ref[:, sl] = _head_rms(a[:, sl], kg_ref[...], None).astype(BF16)
        o_ref[:, MISC_V:] = a[:, MISC_V:].astype(BF16)

    @pl.when(j == COL_QI // PROJ_TN)
    def _():
        o_ref[...] = acc().astype(BF16)


def _proj(x2, sc1, sh1, norm1_g, w_cat, v_norm_g, q_norm_g, k_norm_g, seq):
    n = x2.shape[0]
    tm = 1024
    tiles_per_batch = seq // tm
    mod_spec = pl.BlockSpec((None, 1, D_MODEL), lambda i, j: (i // tiles_per_batch, 0, 0))
    return pl.pallas_call(
        _proj_kernel,
        out_shape=jax.ShapeDtypeStruct((n, PROJ_COLS), BF16),
        grid=(n // tm, PROJ_COLS // PROJ_TN),
        in_specs=[
            pl.BlockSpec((tm, D_MODEL), lambda i, j: (i, 0)),
            mod_spec, mod_spec,
            pl.BlockSpec((1, D_MODEL), lambda i, j: (0, 0)),
            pl.BlockSpec((D_MODEL, PROJ_TN), lambda i, j: (0, j)),
            pl.BlockSpec((1, A_WIDTH), lambda i, j: (0, 0)),
            pl.BlockSpec((1, HEAD_DIM), lambda i, j: (0, 0)),
            pl.BlockSpec((1, HEAD_DIM), lambda i, j: (0, 0)),
        ],
        out_specs=pl.BlockSpec((tm, PROJ_TN), lambda i, j: (i, j)),
        scratch_shapes=[pltpu.VMEM((tm, D_MODEL), BF16)],
        compiler_params=pltpu.CompilerParams(
            dimension_semantics=("parallel", "arbitrary"), vmem_limit_bytes=VMEM_LIMIT),
        name="proj_in",
    )(x2, sc1, sh1, norm1_g.reshape(1, D_MODEL), w_cat, v_norm_g.reshape(1, A_WIDTH),
      q_norm_g.reshape(1, HEAD_DIM), k_norm_g.reshape(1, HEAD_DIM))


def _sublane_tree(x, op):
    r, c = x.shape
    x = x.reshape(r // 64, 8, 8, c)
    y = x[0]
    for t in range(1, r // 64):
        y = op(y, x[t])
    z = op(op(y[0], y[1]), op(y[2], y[3]))
    return op(z, op(op(y[4], y[5]), op(y[6], y[7])))


def _attn_block(s_eff, i, shifts, fast, q_ref, misc_ref, wi_ref, qi_ref, o_ref, vt_ref, score_ref, bias_ref, p_ref,
                topk):
    nch = s_eff // ATT_CH

    idx_scale = IDX_DIM ** -0.5 * IDX_HEADS ** -0.5
    w_t = wi_ref[...].astype(F32).T * idx_scale
    rhs = [jnp.concatenate([qi_ref[:, 256 * r:256 * r + 128], qi_ref[:, 256 * r + 128:256 * r + 256]], axis=0)
           for r in range(IDX_HEADS // 4)]
    q_pos = i * ATT_T + lax.broadcasted_iota(jnp.int32, (ATT_CH, ATT_T), 1)
    row_iota = lax.broadcasted_iota(jnp.int32, (ATT_CH, ATT_T), 0)

    for c in range(nch):
        rows = slice(c * ATT_CH, (c + 1) * ATT_CH)
        k_even = misc_ref[rows, MISC_KI_EVEN:MISC_KI_EVEN + LANES]
        k_odd = misc_ref[rows, MISC_KI_ODD:MISC_KI_ODD + LANES]
        acc = [None, None]
        for r in range(IDX_HEADS // 4):
            l_even = _dot_nt(k_even, rhs[r])
            l_odd = _dot_nt(k_odd, rhs[r])
            terms = (w_t[4 * r:4 * r + 1] * jnp.maximum(l_even[:, :ATT_T], 0.0),
                     w_t[4 * r + 1:4 * r + 2] * jnp.maximum(l_odd[:, :ATT_T], 0.0),
                     w_t[4 * r + 2:4 * r + 3] * jnp.maximum(l_even[:, ATT_T:], 0.0),
                     w_t[4 * r + 3:4 * r + 4] * jnp.maximum(l_odd[:, ATT_T:], 0.0))
            for t, term in enumerate(terms):
                acc[t % 2] = term if acc[t % 2] is None else acc[t % 2] + term
        score = acc[0] + acc[1]
        if (c + 1) * ATT_CH > s_eff - ATT_SPAN:
            score = jnp.where(c * ATT_CH + row_iota <= q_pos, score, -jnp.inf)
        score_ref[rows, :] = score

    def count_ge(thr_f):
        tot = None
        for c in range(nch):
            ones = jnp.where(score_ref[c * ATT_CH:(c + 1) * ATT_CH, :] >= thr_f, 1.0, 0.0)
            part = _sublane_tree(ones, jnp.add)
            tot = part if tot is None else tot + part
        return jnp.sum(tot, axis=0, keepdims=True)

    def code_to_float(code):
        key = code ^ INT_MIN
        return lax.bitcast_convert_type(key ^ ((key >> 31) & 0x7FFFFFFF), F32)

    def bs_body(it, carry):
        code, n_ge = carry
        cand = code | lax.shift_left(jnp.int32(1), 31 - it)
        cnt = count_ge(code_to_float(cand))
        ok = cnt >= float(topk)
        return jnp.where(ok, cand, code), jnp.where(ok, cnt, n_ge)

    code, n_ge = lax.fori_loop(0, 32, bs_body, (jnp.zeros((1, ATT_T), jnp.int32),
                                                jnp.full((1, ATT_T), float(s_eff), F32)))
    thr_f = code_to_float(jnp.maximum(code ^ INT_MIN, KEY_NEG_INF + 1) ^ INT_MIN)
    excess = jnp.max(jnp.where(thr_f > MOST_NEG_F32, n_ge - float(topk), 0.0))

    @pl.when(excess <= 0.0)
    def _():
        for c in range(nch):
            rows = slice(c * ATT_CH, (c + 1) * ATT_CH)
            bias_ref[rows, :] = jnp.where(score_ref[rows, :] >= thr_f, 0.0, NEG_BIG)

    @pl.when(excess > 0.0)
    def _():
        n_tie = None
        for c in range(nch):
            ones = jnp.where(score_ref[c * ATT_CH:(c + 1) * ATT_CH, :] == thr_f, 1.0, 0.0)
            part = _sublane_tree(ones, jnp.add)
            n_tie = part if n_tie is None else n_tie + part
        n_tie = jnp.sum(n_tie, axis=0, keepdims=True)
        need = float(topk) - (n_ge - n_tie)
        lower = (lax.broadcasted_iota(jnp.int32, (ATT_CH, ATT_CH), 1)
                 <= lax.broadcasted_iota(jnp.int32, (ATT_CH, ATT_CH), 0))
        tril = jnp.where(lower, 1.0, 0.0).astype(BF16)
        run = jnp.zeros((1, ATT_T), F32)
        for c in range(nch):
            rows = slice(c * ATT_CH, (c + 1) * ATT_CH)
            sc = score_ref[rows, :]
            tie = jnp.where(sc == thr_f, 1.0, 0.0)
            rank = _dot(tril, tie.astype(BF16)) + run
            keep = (sc > thr_f) | ((sc == thr_f) & (rank <= need))
            bias_ref[rows, :] = jnp.where(keep, 0.0, NEG_BIG)
            run = run + jnp.sum(_sublane_tree(tie, jnp.add), axis=0, keepdims=True)

    @pl.when(fast)
    def _():
        for p in range(N_HEADS // 2):
            g, q_pair = _head_pair(q_ref, p)
            for c in range(nch):
                rows = slice(c * ATT_CH, (c + 1) * ATT_CH)
                bias = bias_ref[rows, :]
                s = _dot_nt(misc_ref[rows, MISC_K + g * HEAD_DIM:MISC_K + (g + 1) * HEAD_DIM], q_pair)
                s = s + jnp.concatenate([bias, bias], axis=1) - shifts[p]
                p_ref[rows, :] = jnp.exp2(s).astype(BF16)
            out = _dot(vt_ref[g, :, 0:s_eff], p_ref[0:s_eff, :])
            _store_heads(o_ref, p, out[:HEAD_DIM] / out[HEAD_DIM:HEAD_DIM + 1])


def _head_pair(q_ref, p):
    g = (2 * p) // (N_HEADS // N_KV)
    return g, jnp.concatenate([q_ref[:, 256 * p:256 * p + 128], q_ref[:, 256 * p + 128:256 * p + 256]], axis=0)


def _store_heads(o_ref, p, out):
    o_ref[:, 256 * p:256 * p + 128] = out[:, :ATT_T].T.astype(BF16)
    o_ref[:, 256 * p + 128:256 * p + 256] = out[:, ATT_T:].T.astype(BF16)


def _softmax_shifts(q_ref, kmax_ref):
    ones_rows = jnp.ones((8, HEAD_DIM), BF16)
    shifts = []
    for p in range(N_HEADS // 2):
        g, q_pair = _head_pair(q_ref, p)
        q2 = q_pair.astype(F32)
        qn2 = _dot_nt(ones_rows, (q2 * q2).astype(BF16))[0:1]
        kmax = kmax_ref[g, 0:1, :]
        shifts.append(jnp.sqrt(qn2) * (jnp.concatenate([kmax, kmax], axis=1) * ATT_BOUND_SLACK))
    return shifts, jnp.max(jnp.concatenate(shifts, axis=0)) <= ATT_SAFE_SHIFT


def _attn_exact_max(nch, q_ref, misc_ref, o_ref, vt_ref, bias_ref, s_ref, p_ref, seq):
    def zero_tail(c, carry):
        p_ref[pl.ds(pl.multiple_of(c * ATT_CH, ATT_CH), ATT_CH), :] = jnp.zeros((ATT_CH, 2 * ATT_T), BF16)
        return carry

    lax.fori_loop(nch, seq // ATT_CH, zero_tail, 0)
    for p in range(N_HEADS // 2):
        g, q_pair = _head_pair(q_ref, p)

        def logits(c, mx, g=g, q_pair=q_pair):
            rows = pl.ds(pl.multiple_of(c * ATT_CH, ATT_CH), ATT_CH)
            bias = bias_ref[rows, :]
            s = _dot_nt(misc_ref[rows, MISC_K + g * HEAD_DIM:MISC_K + (g + 1) * HEAD_DIM], q_pair)
            s = s + jnp.concatenate([bias, bias], axis=1)
            s_ref[rows, :] = s
            return jnp.maximum(mx, _sublane_tree(s, jnp.maximum))

        mx = lax.fori_loop(0, nch, logits, jnp.full((8, 2 * ATT_T), -jnp.inf, F32))
        m = jnp.max(mx, axis=0, keepdims=True)

        def probs(c, ls, m=m):
            rows = pl.ds(pl.multiple_of(c * ATT_CH, ATT_CH), ATT_CH)
            pm = jnp.exp2(s_ref[rows, :] - m)
            p_ref[rows, :] = pm.astype(BF16)
            return ls + _sublane_tree(pm, jnp.add)

        ls = lax.fori_loop(0, nch, probs, jnp.zeros((8, 2 * ATT_T), F32))
        out = _dot(vt_ref[g], p_ref[...])
        _store_heads(o_ref, p, out[:HEAD_DIM] / jnp.sum(ls, axis=0, keepdims=True))


def _attn_kernel(q_ref, misc_ref, wi_ref, qi_ref, o_ref, vt_ref, kmax_ref, score_ref, bias_ref, s_ref, p_ref,
                 *, seq, topk):
    i = pl.program_id(1)

    @pl.when(i == 0)
    def _():
        for g in range(N_KV):
            for c in range(seq // ATT_CH):
                rows = slice(c * ATT_CH, (c + 1) * ATT_CH)
                vg = misc_ref[rows, MISC_V + g * HEAD_DIM:MISC_V + (g + 1) * HEAD_DIM]
                vt_ref[g, 0:HEAD_DIM, rows] = vg.astype(F32).T.astype(BF16)
            vt_ref[g, HEAD_DIM:, :] = jnp.ones((ATT_ONES_ROWS, seq), BF16)
            kg = misc_ref[:, MISC_K + g * HEAD_DIM:MISC_K + (g + 1) * HEAD_DIM].astype(F32)
            k_norm2 = jnp.max(jnp.sum(kg * kg, axis=1, keepdims=True))
            kmax_ref[g] = jnp.full((8, LANES), jnp.sqrt(k_norm2), F32)

    n_span = (i * ATT_T + ATT_T + ATT_SPAN - 1) // ATT_SPAN
    shifts, fast = _softmax_shifts(q_ref, kmax_ref)
    for ns in range(1, seq // ATT_SPAN + 1):
        @pl.when(n_span == ns)
        def _(ns=ns):
            _attn_block(ns * ATT_SPAN, i, shifts, fast, q_ref, misc_ref, wi_ref, qi_ref, o_ref, vt_ref, score_ref,
                        bias_ref, p_ref, topk)

    @pl.when(jnp.logical_not(fast))
    def _():
        _attn_exact_max(n_span * (ATT_SPAN // ATT_CH), q_ref, misc_ref, o_ref, vt_ref, bias_ref, s_ref, p_ref, seq)


def _attention(proj, bsz, seq):
    n = proj.shape[0]
    nblk = seq // ATT_T
    topk = min(TOPK_MAX, seq // 4)
    kern = functools.partial(_attn_kernel, seq=seq, topk=topk)
    return pl.pallas_call(
        kern,
        out_shape=jax.ShapeDtypeStruct((n, Q_WIDTH), BF16),
        grid=(bsz, nblk),
        in_specs=[
            pl.BlockSpec((ATT_T, Q_WIDTH), lambda b, i: (b * nblk + i, COL_Q // Q_WIDTH)),
            pl.BlockSpec((seq, PROJ_TN), lambda b, i: (b, COL_MISC // PROJ_TN)),
            pl.BlockSpec((ATT_T, LANES), lambda b, i: (b * nblk + i, (COL_MISC + MISC_WI) // LANES)),
            pl.BlockSpec((ATT_T, IDX_HEADS * IDX_DIM), lambda b, i: (b * nblk + i, COL_QI // (IDX_HEADS * IDX_DIM))),
        ],
        out_specs=pl.BlockSpec((ATT_T, Q_WIDTH), lambda b, i: (b * nblk + i, 0)),
        scratch_shapes=[
            pltpu.VMEM((N_KV, HEAD_DIM + ATT_ONES_ROWS, seq), BF16),
            pltpu.VMEM((N_KV, 8, LANES), F32),
            pltpu.VMEM((seq, ATT_T), F32),
            pltpu.VMEM((seq, ATT_T), F32),
            pltpu.VMEM((seq, 2 * ATT_T), F32),
            pltpu.VMEM((seq, 2 * ATT_T), BF16),
        ],
        compiler_params=pltpu.CompilerParams(
            dimension_semantics=("arbitrary", "arbitrary"), vmem_limit_bytes=VMEM_LIMIT),
        name="sparse_attn",
    )(proj, proj, proj, proj)


def _merge_kernel(ga_ref, gb_ref, u_ref, v_ref, yb_ref, x_ref, g1_ref, sc2_ref, sh2_ref, n2g_ref,
                  wsp_ref, bsp_ref, wpa_ref, wpb_ref, wout_ref, x1_ref, h2_ref, ya_ref, *, tm):
    row = lax.broadcasted_iota(jnp.int32, (CHUNK, CHUNK), 0)
    col = lax.broadcasted_iota(jnp.int32, (CHUNK, CHUNK), 1)
    causal = col <= row
    bsp = bsp_ref[...]
    for g in range(A_GROUPS):
        wm = jnp.where(causal, wsp_ref[g], jnp.zeros((), BF16))
        gcols = slice(g * A_GROUP_DIM, (g + 1) * A_GROUP_DIM)
        for ci in range(tm // CHUNK):
            rows = slice(ci * CHUNK, (ci + 1) * CHUNK)
            mixed = _dot(wm, v_ref[rows, gcols]) + bsp[:, g:g + 1]
            ya_ref[rows, gcols] = (u_ref[rows, gcols].astype(F32) * mixed).astype(BF16)

    a = _dot(ya_ref[...], wpa_ref[...])
    b = _dot(yb_ref[...], wpb_ref[...])
    merged = ga_ref[...].astype(F32) * a + gb_ref[...].astype(F32) * b
    o = _dot(merged.astype(BF16), wout_ref[...])
    x1 = x_ref[...] + g1_ref[...] * o
    x1_ref[...] = x1
    r = lax.rsqrt(jnp.mean(x1 * x1, axis=-1, keepdims=True) + EPS)
    h2 = (x1 * r * n2g_ref[...]) * (1.0 + sc2_ref[...]) + sh2_ref[...]
    h2_ref[...] = h2.astype(BF16)


def _merge(proj, yb, x2, g1, sc2, sh2, norm2_g, wsp, bsp_t, wpa, wpb, wout, seq):
    n = x2.shape[0]
    tm = 256
    tiles_per_batch = seq // tm
    mod_spec = pl.BlockSpec((None, 1, D_MODEL), lambda i: (i // tiles_per_batch, 0, 0))
    const2 = lambda i: (0, 0)
    kern = functools.partial(_merge_kernel, tm=tm)
    return pl.pallas_call(
        kern,
        out_shape=(jax.ShapeDtypeStruct((n, D_MODEL), F32), jax.ShapeDtypeStruct((n, D_MODEL), BF16)),
        grid=(n // tm,),
        in_specs=[
            pl.BlockSpec((tm, D_MODEL), lambda i: (i, COL_GA // D_MODEL)),
            pl.BlockSpec((tm, D_MODEL), lambda i: (i, COL_GB // D_MODEL)),
            pl.BlockSpec((tm, A_WIDTH), lambda i: (i, COL_U // A_WIDTH)),
            pl.BlockSpec((tm, A_WIDTH), lambda i: (i, COL_VA // A_WIDTH)),
            pl.BlockSpec((tm, Q_WIDTH), lambda i: (i, 0)),
            pl.BlockSpec((tm, D_MODEL), lambda i: (i, 0)),
            mod_spec, mod_spec, mod_spec,
            pl.BlockSpec((1, D_MODEL), const2),
            pl.BlockSpec((A_GROUPS, CHUNK, CHUNK), lambda i: (0, 0, 0)),
            pl.BlockSpec((CHUNK, A_GROUPS), const2),
            pl.BlockSpec((A_WIDTH, D_MODEL), const2, pipeline_mode=pl.Buffered(1)),
            pl.BlockSpec((Q_WIDTH, D_MODEL), const2, pipeline_mode=pl.Buffered(1)),
            pl.BlockSpec((D_MODEL, D_MODEL), const2, pipeline_mode=pl.Buffered(1)),
        ],
        out_specs=(pl.BlockSpec((tm, D_MODEL), lambda i: (i, 0)),
                   pl.BlockSpec((tm, D_MODEL), lambda i: (i, 0))),
        scratch_shapes=[pltpu.VMEM((tm, A_WIDTH), BF16)],
        compiler_params=pltpu.CompilerParams(
            dimension_semantics=("parallel",), vmem_limit_bytes=VMEM_LIMIT),
        name="merge_out",
    )(proj, proj, proj, proj, yb, x2, g1, sc2, sh2, norm2_g.reshape(1, D_MODEL), wsp, bsp_t, wpa, wpb, wout)


FFN_TM = 512
FFN_TF = 512
FFN_HALO = 16


def _ffn_kernel(h_ref, hprev_ref, wa_ref, wb_ref, cwa_ref, cwb_ref, cba_ref, cbb_ref, wd_ref, x1_ref, g2_ref,
                o_ref, hh_ref, up_ref, acc_ref, *, tiles_per_batch):
    i = pl.program_id(0)
    f = pl.program_id(1)

    @pl.when(f == 0)
    def _():
        first = (i % tiles_per_batch) == 0

        @pl.when(first)
        def _():
            hh_ref[:FFN_HALO] = jnp.zeros((FFN_HALO, D_MODEL), BF16)

        @pl.when(jnp.logical_not(first))
        def _():
            hh_ref[:FFN_HALO] = hprev_ref[...]

        hh_ref[FFN_HALO:] = h_ref[...]
        acc_ref[...] = jnp.zeros_like(acc_ref)

    hh = hh_ref[...]
    up_ref[:, :FFN_TF] = _dot(hh, wa_ref[...])
    up_ref[:, FFN_TF:] = _dot(hh, wb_ref[...])

    def conv(cols, cw_ref, cb_ref):
        y = cb_ref[...] + cw_ref[0:1] * up_ref[FFN_HALO - 2:FFN_HALO - 2 + FFN_TM, cols]
        y = y + cw_ref[1:2] * up_ref[FFN_HALO - 1:FFN_HALO - 1 + FFN_TM, cols]
        return y + cw_ref[2:3] * up_ref[FFN_HALO:FFN_HALO + FFN_TM, cols]

    a = conv(slice(0, FFN_TF), cwa_ref, cba_ref)
    b = conv(slice(FFN_TF, 2 * FFN_TF), cwb_ref, cbb_ref)
    gated = (jax.nn.silu(a) * b).astype(BF16)
    acc_ref[...] += _dot(gated, wd_ref[...])

    @pl.when(f == pl.num_programs(1) - 1)
    def _():
        o_ref[...] = x1_ref[...] + g2_ref[...] * acc_ref[...]


def _ffn(h2, x1, g2, w_up, conv_w, conv_b, w_down, seq):
    n = h2.shape[0]
    tm, tf = FFN_TM, FFN_TF
    nf = D_FF // tf
    tiles_per_batch = seq // tm
    halo_blocks = tm // FFN_HALO
    kern = functools.partial(_ffn_kernel, tiles_per_batch=tiles_per_batch)
    return pl.pallas_call(
        kern,
        out_shape=jax.ShapeDtypeStruct((n, D_MODEL), F32),
        grid=(n // tm, nf),
        in_specs=[
            pl.BlockSpec((tm, D_MODEL), lambda i, f: (i, 0)),
            pl.BlockSpec((FFN_HALO, D_MODEL), lambda i, f: (jnp.maximum(i * halo_blocks - 1, 0), 0)),
            pl.BlockSpec((D_MODEL, tf), lambda i, f: (0, f)),
            pl.BlockSpec((D_MODEL, tf), lambda i, f: (0, f + nf)),
            pl.BlockSpec((CONV_W, tf), lambda i, f: (0, f)),
            pl.BlockSpec((CONV_W, tf), lambda i, f: (0, f + nf)),
            pl.BlockSpec((1, tf), lambda i, f: (0, f)),
            pl.BlockSpec((1, tf), lambda i, f: (0, f + nf)),
            pl.BlockSpec((tf, D_MODEL), lambda i, f: (f, 0)),
            pl.BlockSpec((tm, D_MODEL), lambda i, f: (i, 0)),
            pl.BlockSpec((None, 1, D_MODEL), lambda i, f: (i // tiles_per_batch, 0, 0)),
        ],
        out_specs=pl.BlockSpec((tm, D_MODEL), lambda i, f: (i, 0)),
        scratch_shapes=[
            pltpu.VMEM((tm + FFN_HALO, D_MODEL), BF16),
            pltpu.VMEM((tm + FFN_HALO, 2 * tf), F32),
            pltpu.VMEM((tm, D_MODEL), F32),
        ],
        compiler_params=pltpu.CompilerParams(
            dimension_semantics=("parallel", "arbitrary"), vmem_limit_bytes=VMEM_LIMIT),
        name="conv_ffn",
    )(h2, h2, w_up, w_up, conv_w, conv_w, conv_b.reshape(1, 2 * D_FF), conv_b.reshape(1, 2 * D_FF),
      w_down, x1, g2)


def _fused_in_weight(w_in):
    sizes = [A_WIDTH, A_WIDTH, Q_WIDTH, KV_WIDTH, KV_WIDTH, IDX_HEADS * IDX_DIM, IDX_DIM, IDX_HEADS, D_MODEL, D_MODEL]
    offs = [0]
    for s in sizes:
        offs.append(offs[-1] + s)
    u, va, q, k, vb, qi, ki, wi, ga, gb = [w_in[:, offs[t]:offs[t + 1]] for t in range(len(sizes))]
    z = lambda ncol: jnp.zeros((D_MODEL, ncol), w_in.dtype)
    misc = jnp.concatenate([k, vb, ki, z(IDX_DIM), z(IDX_DIM), ki, wi, z(LANES - IDX_HEADS), z(LANES)], axis=1)
    w_cat = jnp.concatenate([ga, gb, u, va, q, misc, qi], axis=1)
    assert w_cat.shape[1] == PROJ_COLS and misc.shape[1] == PROJ_TN
    return w_cat.astype(BF16)


def kernel(x, c, w_ada, b_ada, norm1_g, w_in, v_norm_g, w_spatial, b_spatial, q_norm_g, k_norm_g, w_proj_a,
           w_proj_b, w_out, norm2_g, w_up, conv_w, conv_b, w_down):
    bsz, seq, _ = x.shape
    n = bsz * seq
    x2 = x.reshape(n, D_MODEL)

    mod = _ada(c, w_ada, b_ada)
    sh1, sc1, g1, sh2, sc2, g2 = [mod[:, t * D_MODEL:(t + 1) * D_MODEL].reshape(bsz, 1, D_MODEL) for t in range(6)]

    proj = _proj(x2, sc1, sh1, norm1_g, _fused_in_weight(w_in), v_norm_g, q_norm_g, k_norm_g, seq)
    yb = _attention(proj, bsz, seq)
    x1, h2 = _merge(proj, yb, x2, g1, sc2, sh2, norm2_g, w_spatial.astype(BF16), b_spatial.T,
                    w_proj_a.astype(BF16), w_proj_b.astype(BF16), w_out.astype(BF16), seq)
    out = _ffn(h2, x1, g2, w_up.astype(BF16), conv_w, conv_b, w_down.astype(BF16), seq)
    return out.reshape(bsz, seq, D_MODEL)
```

```python
import functools
import math

import jax
import jax.numpy as jnp
from jax import lax
from jax.experimental import pallas as pl
from jax.experimental.pallas import tpu as pltpu

F32 = jnp.float32
BF16 = jnp.bfloat16

D_MODEL = 2048
CHUNK = 128
A_GROUPS = 8
A_GROUP_DIM = 128
A_WIDTH = A_GROUPS * A_GROUP_DIM
N_HEADS = 8
HEAD_DIM = 128
N_KV = 2
Q_WIDTH = N_HEADS * HEAD_DIM
KV_WIDTH = N_KV * HEAD_DIM
IDX_HEADS = 16
IDX_DIM = 64
TOPK_MAX = 256
D_FF = 5632
CONV_W = 3
EPS = 1e-6

LANES = 128
VMEM_LIMIT = 56 * 1024 * 1024

PROJ_TN = 1024
COL_GA, COL_GB, COL_U, COL_VA, COL_Q, COL_MISC, COL_QI = 0, 2048, 4096, 5120, 6144, 7168, 8192
PROJ_COLS = 9216
MISC_K, MISC_V, MISC_KI_EVEN, MISC_KI_ODD, MISC_WI = 0, 256, 512, 640, 768

ATT_T = 128
ATT_CH = 256
ATT_SPAN = 256
ATT_ONES_ROWS = 16
ATT_SAFE_SHIFT = 40.0
ATT_BOUND_SLACK = 1.02
NEG_BIG = -1e30
INT_MIN = -(2 ** 31)
KEY_NEG_INF = -2139095041
MOST_NEG_F32 = -3.4028234663852886e38


def _dot(a, b):
    return jnp.dot(a, b, preferred_element_type=F32)


def _dot_nt(a, b):
    return lax.dot_general(a, b, (((1,), (1,)), ((), ())), preferred_element_type=F32)


def _ada_kernel(c_ref, w_ref, b_ref, o_ref):
    cs = jax.nn.silu(c_ref[...]).astype(BF16)
    o_ref[...] = _dot(cs, w_ref[...].astype(BF16)) + b_ref[...]


def _ada(c, w_ada, b_ada):
    bsz = c.shape[0]
    n = w_ada.shape[1]
    tn = 1024
    return pl.pallas_call(
        _ada_kernel,
        out_shape=jax.ShapeDtypeStruct((bsz, n), F32),
        grid=(n // tn,),
        in_specs=[
            pl.BlockSpec((bsz, D_MODEL), lambda j: (0, 0)),
            pl.BlockSpec((D_MODEL, tn), lambda j: (0, j)),
            pl.BlockSpec((1, tn), lambda j: (0, j)),
        ],
        out_specs=pl.BlockSpec((bsz, tn), lambda j: (0, j)),
        compiler_params=pltpu.CompilerParams(
            dimension_semantics=("arbitrary",), vmem_limit_bytes=VMEM_LIMIT),
        name="ada_mod",
    )(c, w_ada, b_ada.reshape(1, n))


def _sigmoid(x):
    return 0.5 * jnp.tanh(0.5 * x) + 0.5


def _head_rms(x, g, scale):
    r = lax.rsqrt(jnp.mean(x * x, axis=-1, keepdims=True) + EPS)
    y = x * r * g
    return y if scale is None else y * scale


def _proj_kernel(x_ref, sc_ref, sh_ref, g_ref, w_ref, vg_ref, qg_ref, kg_ref, o_ref, h_ref):
    j = pl.program_id(1)

    def acc():
        return _dot(h_ref[...], w_ref[...])

    @pl.when(j == 0)
    def _():
        x = x_ref[...]
        r = lax.rsqrt(jnp.mean(x * x, axis=-1, keepdims=True) + EPS)
        h = ((x * r * g_ref[...]) * (1.0 + sc_ref[...]) + sh_ref[...]).astype(BF16)
        h_ref[...] = h
        o_ref[...] = _sigmoid(_dot(h, w_ref[...])).astype(BF16)

    @pl.when((j > 0) & (j < COL_U // PROJ_TN))
    def _():
        o_ref[...] = _sigmoid(acc()).astype(BF16)

    @pl.when(j == COL_U // PROJ_TN)
    def _():
        o_ref[...] = jax.nn.gelu(acc()).astype(BF16)

    @pl.when(j == COL_VA // PROJ_TN)
    def _():
        v = jax.nn.gelu(acc())
        r = lax.rsqrt(jnp.mean(v * v, axis=-1, keepdims=True) + EPS)
        o_ref[...] = (v * r * vg_ref[...]).astype(BF16)

    @pl.when(j == COL_Q // PROJ_TN)
    def _():
        qscale = (HEAD_DIM ** -0.5) * math.log2(math.e)
        a = acc()
        for h in range(N_HEADS):
            sl = slice(h * HEAD_DIM, (h + 1) * HEAD_DIM)
            o_ref[:, sl] = _head_rms(a[:, sl], qg_ref[...], qscale).astype(BF16)

    @pl.when(j == COL_MISC // PROJ_TN)
    def _():
        a = acc()
        for h in range(N_KV):
            sl = slice(MISC_K + h * HEAD_DIM, MISC_K + (h + 1) * HEAD_DIM)
            o_ref[:, sl] = _head_rms(a[:, sl], kg_ref[...], None).astype(BF16)
        o_ref[:, MISC_V:] = a[:, MISC_V:].astype(BF16)

    @pl.when(j == COL_QI // PROJ_TN)
    def _():
        o_ref[...] = acc().astype(BF16)


def _proj(x2, sc1, sh1, norm1_g, w_cat, v_norm_g, q_norm_g, k_norm_g, seq):
    n = x2.shape[0]
    tm = 1024
    tiles_per_batch = seq // tm
    mod_spec = pl.BlockSpec((None, 1, D_MODEL), lambda i, j: (i // tiles_per_batch, 0, 0))
    return pl.pallas_call(
        _proj_kernel,
        out_shape=jax.ShapeDtypeStruct((n, PROJ_COLS), BF16),
        grid=(n // tm, PROJ_COLS // PROJ_TN),
        in_specs=[
            pl.BlockSpec((tm, D_MODEL), lambda i, j: (i, 0)),
            mod_spec, mod_spec,
            pl.BlockSpec((1, D_MODEL), lambda i, j: (0, 0)),
            pl.BlockSpec((D_MODEL, PROJ_TN), lambda i, j: (0, j)),
            pl.BlockSpec((1, A_WIDTH), lambda i, j: (0, 0)),
            pl.BlockSpec((1, HEAD_DIM), lambda i, j: (0, 0)),
            pl.BlockSpec((1, HEAD_DIM), lambda i, j: (0, 0)),
        ],
        out_specs=pl.BlockSpec((tm, PROJ_TN), lambda i, j: (i, j)),
        scratch_shapes=[pltpu.VMEM((tm, D_MODEL), BF16)],
        compiler_params=pltpu.CompilerParams(
            dimension_semantics=("parallel", "arbitrary"), vmem_limit_bytes=VMEM_LIMIT),
        name="proj_in",
    )(x2, sc1, sh1, norm1_g.reshape(1, D_MODEL), w_cat, v_norm_g.reshape(1, A_WIDTH),
      q_norm_g.reshape(1, HEAD_DIM), k_norm_g.reshape(1, HEAD_DIM))


def _sublane_tree(x, op):
    r, c = x.shape
    x = x.reshape(r // 64, 8, 8, c)
    y = x[0]
    for t in range(1, r // 64):
        y = op(y, x[t])
    z = op(op(y[0], y[1]), op(y[2], y[3]))
    return op(z, op(op(y[4], y[5]), op(y[6], y[7])))


def _attn_block(s_eff, i, shifts, fast, q_ref, misc_ref, wi_ref, qi_ref, o_ref, vt_ref, score_ref, bias_ref, p_ref,
                topk):
    nch = s_eff // ATT_CH

    idx_scale = IDX_DIM ** -0.5 * IDX_HEADS ** -0.5
    w_t = wi_ref[...].astype(F32).T * idx_scale
    rhs = [jnp.concatenate([qi_ref[:, 256 * r:256 * r + 128], qi_ref[:, 256 * r + 128:256 * r + 256]], axis=0)
           for r in range(IDX_HEADS // 4)]
    q_pos = i * ATT_T + lax.broadcasted_iota(jnp.int32, (ATT_CH, ATT_T), 1)
    row_iota = lax.broadcasted_iota(jnp.int32, (ATT_CH, ATT_T), 0)

    for c in range(nch):
        rows = slice(c * ATT_CH, (c + 1) * ATT_CH)
        k_even = misc_ref[rows, MISC_KI_EVEN:MISC_KI_EVEN + LANES]
        k_odd = misc_ref[rows, MISC_KI_ODD:MISC_KI_ODD + LANES]
        acc = [None, None]
        for r in range(IDX_HEADS // 4):
            l_even = _dot_nt(k_even, rhs[r])
            l_odd = _dot_nt(k_odd, rhs[r])
            terms = (w_t[4 * r:4 * r + 1] * jnp.maximum(l_even[:, :ATT_T], 0.0),
                     w_t[4 * r + 1:4 * r + 2] * jnp.maximum(l_odd[:, :ATT_T], 0.0),
                     w_t[4 * r + 2:4 * r + 3] * jnp.maximum(l_even[:, ATT_T:], 0.0),
                     w_t[4 * r + 3:4 * r + 4] * jnp.maximum(l_odd[:, ATT_T:], 0.0))
            for t, term in enumerate(terms):
                acc[t % 2] = term if acc[t % 2] is None else acc[t % 2] + term
        score = acc[0] + acc[1]
        if (c + 1) * ATT_CH > s_eff - ATT_SPAN:
            score = jnp.where(c * ATT_CH + row_iota <= q_pos, score, -jnp.inf)
        score_ref[rows, :] = score

    def count_ge(thr_f):
        tot = None
        for c in range(nch):
            ones = jnp.where(score_ref[c * ATT_CH:(c + 1) * ATT_CH, :] >= thr_f, 1.0, 0.0)
            part = _sublane_tree(ones, jnp.add)
            tot = part if tot is None else tot + part
        return jnp.sum(tot, axis=0, keepdims=True)

    def code_to_float(code):
        key = code ^ INT_MIN
        return lax.bitcast_convert_type(key ^ ((key >> 31) & 0x7FFFFFFF), F32)

    def bs_body(it, carry):
        code, n_ge = carry
        cand = code | lax.shift_left(jnp.int32(1), 31 - it)
        cnt = count_ge(code_to_float(cand))
        ok = cnt >= float(topk)
        return jnp.where(ok, cand, code), jnp.where(ok, cnt, n_ge)

    code, n_ge = lax.fori_loop(0, 32, bs_body, (jnp.zeros((1, ATT_T), jnp.int32),
                                                jnp.full((1, ATT_T), float(s_eff), F32)))
    thr_f = code_to_float(jnp.maximum(code ^ INT_MIN, KEY_NEG_INF + 1) ^ INT_MIN)
    excess = jnp.max(jnp.where(thr_f > MOST_NEG_F32, n_ge - float(topk), 0.0))

    @pl.when(excess <= 0.0)
    def _():
        for c in range(nch):
            rows = slice(c * ATT_CH, (c + 1) * ATT_CH)
            bias_ref[rows, :] = jnp.where(score_ref[rows, :] >= thr_f, 0.0, NEG_BIG)

    @pl.when(excess > 0.0)
    def _():
        n_tie = None
        for c in range(nch):
            ones = jnp.where(score_ref[c * ATT_CH:(c + 1) * ATT_CH, :] == thr_f, 1.0, 0.0)
            part = _sublane_tree(ones, jnp.add)
            n_tie = part if n_tie is None else n_tie + part
        n_tie = jnp.sum(n_tie, axis=0, keepdims=True)
        need = float(topk) - (n_ge - n_tie)
        lower = (lax.broadcasted_iota(jnp.int32, (ATT_CH, ATT_CH), 1)
                 <= lax.broadcasted_iota(jnp.int32, (ATT_CH, ATT_CH), 0))
        tril = jnp.where(lower, 1.0, 0.0).astype(BF16)
        run = jnp.zeros((1, ATT_T), F32)
        for c in range(nch):
            rows = slice(c * ATT_CH, (c + 1) * ATT_CH)
            sc = score_ref[rows, :]
            tie = jnp.where(sc == thr_f, 1.0, 0.0)
            rank = _dot(tril, tie.astype(BF16)) + run
            keep = (sc > thr_f) | ((sc == thr_f) & (rank <= need))
            bias_ref[rows, :] = jnp.where(keep, 0.0, NEG_BIG)
            run = run + jnp.sum(_sublane_tree(tie, jnp.add), axis=0, keepdims=True)

    @pl.when(fast)
    def _():
        for p in range(N_HEADS // 2):
            g, q_pair = _head_pair(q_ref, p)
            for c in range(nch):
                rows = slice(c * ATT_CH, (c + 1) * ATT_CH)
                bias = bias_ref[rows, :]
                s = _dot_nt(misc_ref[rows, MISC_K + g * HEAD_DIM:MISC_K + (g + 1) * HEAD_DIM], q_pair)
                s = s + jnp.concatenate([bias, bias], axis=1) - shifts[p]
                p_ref[rows, :] = jnp.exp2(s).astype(BF16)
            out = _dot(vt_ref[g, :, 0:s_eff], p_ref[0:s_eff, :])
            _store_heads(o_ref, p, out[:HEAD_DIM] / out[HEAD_DIM:HEAD_DIM + 1])


def _head_pair(q_ref, p):
    g = (2 * p) // (N_HEADS // N_KV)
    return g, jnp.concatenate([q_ref[:, 256 * p:256 * p + 128], q_ref[:, 256 * p + 128:256 * p + 256]], axis=0)


def _store_heads(o_ref, p, out):
    o_ref[:, 256 * p:256 * p + 128] = out[:, :ATT_T].T.astype(BF16)
    o_ref[:, 256 * p + 128:256 * p + 256] = out[:, ATT_T:].T.astype(BF16)


def _softmax_shifts(q_ref, kmax_ref):
    ones_rows = jnp.ones((8, HEAD_DIM), BF16)
    shifts = []
    for p in range(N_HEADS // 2):
        g, q_pair = _head_pair(q_ref, p)
        q2 = q_pair.astype(F32)
        qn2 = _dot_nt(ones_rows, (q2 * q2).astype(BF16))[0:1]
        kmax = kmax_ref[g, 0:1, :]
        shifts.append(jnp.sqrt(qn2) * (jnp.concatenate([kmax, kmax], axis=1) * ATT_BOUND_SLACK))
    return shifts, jnp.max(jnp.concatenate(shifts, axis=0)) <= ATT_SAFE_SHIFT


def _attn_exact_max(nch, q_ref, misc_ref, o_ref, vt_ref, bias_ref, s_ref, p_ref, seq):
    def zero_tail(c, carry):
        p_ref[pl.ds(pl.multiple_of(c * ATT_CH, ATT_CH), ATT_CH), :] = jnp.zeros((ATT_CH, 2 * ATT_T), BF16)
        return carry

    lax.fori_loop(nch, seq // ATT_CH, zero_tail, 0)
    for p in range(N_HEADS // 2):
        g, q_pair = _head_pair(q_ref, p)

        def logits(c, mx, g=g, q_pair=q_pair):
            rows = pl.ds(pl.multiple_of(c * ATT_CH, ATT_CH), ATT_CH)
            bias = bias_ref[rows, :]
            s = _dot_nt(misc_ref[rows, MISC_K + g * HEAD_DIM:MISC_K + (g + 1) * HEAD_DIM], q_pair)
            s = s + jnp.concatenate([bias, bias], axis=1)
            s_ref[rows, :] = s
            return jnp.maximum(mx, _sublane_tree(s, jnp.maximum))

        mx = lax.fori_loop(0, nch, logits, jnp.full((8, 2 * ATT_T), -jnp.inf, F32))
        m = jnp.max(mx, axis=0, keepdims=True)

        def probs(c, ls, m=m):
            rows = pl.ds(pl.multiple_of(c * ATT_CH, ATT_CH), ATT_CH)
            pm = jnp.exp2(s_ref[rows, :] - m)
            p_ref[rows, :] = pm.astype(BF16)
            return ls + _sublane_tree(pm, jnp.add)

        ls = lax.fori_loop(0, nch, probs, jnp.zeros((8, 2 * ATT_T), F32))
        out = _dot(vt_ref[g], p_ref[...])
        _store_heads(o_ref, p, out[:HEAD_DIM] / jnp.sum(ls, axis=0, keepdims=True))


def _attn_kernel(q_ref, misc_ref, wi_ref, qi_ref, o_ref, vt_ref, kmax_ref, score_ref, bias_ref, s_ref, p_ref,
                 *, seq, topk):
    i = pl.program_id(1)

    @pl.when(i == 0)
    def _():
        for g in range(N_KV):
            for c in range(seq // ATT_CH):
                rows = slice(c * ATT_CH, (c + 1) * ATT_CH)
                vg = misc_ref[rows, MISC_V + g * HEAD_DIM:MISC_V + (g + 1) * HEAD_DIM]
                vt_ref[g, 0:HEAD_DIM, rows] = vg.astype(F32).T.astype(BF16)
            vt_ref[g, HEAD_DIM:, :] = jnp.ones((ATT_ONES_ROWS, seq), BF16)
            kg = misc_ref[:, MISC_K + g * HEAD_DIM:MISC_K + (g + 1) * HEAD_DIM].astype(F32)
            k_norm2 = jnp.max(jnp.sum(kg * kg, axis=1, keepdims=True))
            kmax_ref[g] = jnp.full((8, LANES), jnp.sqrt(k_norm2), F32)

    n_span = (i * ATT_T + ATT_T + ATT_SPAN - 1) // ATT_SPAN
    shifts, fast = _softmax_shifts(q_ref, kmax_ref)
    for ns in range(1, seq // ATT_SPAN + 1):
        @pl.when(n_span == ns)
        def _(ns=ns):
            _attn_block(ns * ATT_SPAN, i, shifts, fast, q_ref, misc_ref, wi_ref, qi_ref, o_ref, vt_ref, score_ref,
                        bias_ref, p_ref, topk)

    @pl.when(jnp.logical_not(fast))
    def _():
        _attn_exact_max(n_span * (ATT_SPAN // ATT_CH), q_ref, misc_ref, o_ref, vt_ref, bias_ref, s_ref, p_ref, seq)


def _attention(proj, bsz, seq):
    n = proj.shape[0]
    nblk = seq // ATT_T
    topk = min(TOPK_MAX, seq // 4)
    kern = functools.partial(_attn_kernel, seq=seq, topk=topk)
    return pl.pallas_call(
        kern,
        out_shape=jax.ShapeDtypeStruct((n, Q_WIDTH), BF16),
        grid=(bsz, nblk),
        in_specs=[
            pl.BlockSpec((ATT_T, Q_WIDTH), lambda b, i: (b * nblk + i, COL_Q // Q_WIDTH)),
            pl.BlockSpec((seq, PROJ_TN), lambda b, i: (b, COL_MISC // PROJ_TN)),
            pl.BlockSpec((ATT_T, LANES), lambda b, i: (b * nblk + i, (COL_MISC + MISC_WI) // LANES)),
            pl.BlockSpec((ATT_T, IDX_HEADS * IDX_DIM), lambda b, i: (b * nblk + i, COL_QI // (IDX_HEADS * IDX_DIM))),
        ],
        out_specs=pl.BlockSpec((ATT_T, Q_WIDTH), lambda b, i: (b * nblk + i, 0)),
        scratch_shapes=[
            pltpu.VMEM((N_KV, HEAD_DIM + ATT_ONES_ROWS, seq), BF16),
            pltpu.VMEM((N_KV, 8, LANES), F32),
            pltpu.VMEM((seq, ATT_T), F32),
            pltpu.VMEM((seq, ATT_T), F32),
            pltpu.VMEM((seq, 2 * ATT_T), F32),
            pltpu.VMEM((seq, 2 * ATT_T), BF16),
        ],
        compiler_params=pltpu.CompilerParams(
            dimension_semantics=("arbitrary", "arbitrary"), vmem_limit_bytes=VMEM_LIMIT),
        name="sparse_attn",
    )(proj, proj, proj, proj)


def _merge_kernel(ga_ref, gb_ref, u_ref, v_ref, yb_ref, x_ref, g1_ref, sc2_ref, sh2_ref, n2g_ref,
                  wsp_ref, bsp_ref, wpa_ref, wpb_ref, wout_ref, x1_ref, h2_ref, ya_ref, *, tm):
    row = lax.broadcasted_iota(jnp.int32, (CHUNK, CHUNK), 0)
    col = lax.broadcasted_iota(jnp.int32, (CHUNK, CHUNK), 1)
    causal = col <= row
    bsp = bsp_ref[...]
    for g in range(A_GROUPS):
        wm = jnp.where(causal, wsp_ref[g], jnp.zeros((), BF16))
        gcols = slice(g * A_GROUP_DIM, (g + 1) * A_GROUP_DIM)
        for ci in range(tm // CHUNK):
            rows = slice(ci * CHUNK, (ci + 1) * CHUNK)
            mixed = _dot(wm, v_ref[rows, gcols]) + bsp[:, g:g + 1]
            ya_ref[rows, gcols] = (u_ref[rows, gcols].astype(F32) * mixed).astype(BF16)

    a = _dot(ya_ref[...], wpa_ref[...])
    b = _dot(yb_ref[...], wpb_ref[...])
    merged = ga_ref[...].astype(F32) * a + gb_ref[...].astype(F32) * b
    o = _dot(merged.astype(BF16), wout_ref[...])
    x1 = x_ref[...] + g1_ref[...] * o
    x1_ref[...] = x1
    r = lax.rsqrt(jnp.mean(x1 * x1, axis=-1, keepdims=True) + EPS)
    h2 = (x1 * r * n2g_ref[...]) * (1.0 + sc2_ref[...]) + sh2_ref[...]
    h2_ref[...] = h2.astype(BF16)


def _merge(proj, yb, x2, g1, sc2, sh2, norm2_g, wsp, bsp_t, wpa, wpb, wout, seq):
    n = x2.shape[0]
    tm = 256
    tiles_per_batch = seq // tm
    mod_spec = pl.BlockSpec((None, 1, D_MODEL), lambda i: (i // tiles_per_batch, 0, 0))
    const2 = lambda i: (0, 0)
    kern = functools.partial(_merge_kernel, tm=tm)
    return pl.pallas_call(
        kern,
        out_shape=(jax.ShapeDtypeStruct((n, D_MODEL), F32), jax.ShapeDtypeStruct((n, D_MODEL), BF16)),
        grid=(n // tm,),
        in_specs=[
            pl.BlockSpec((tm, D_MODEL), lambda i: (i, COL_GA // D_MODEL)),
            pl.BlockSpec((tm, D_MODEL), lambda i: (i, COL_GB // D_MODEL)),
            pl.BlockSpec((tm, A_WIDTH), lambda i: (i, COL_U // A_WIDTH)),
            pl.BlockSpec((tm, A_WIDTH), lambda i: (i, COL_VA // A_WIDTH)),
            pl.BlockSpec((tm, Q_WIDTH), lambda i: (i, 0)),
            pl.BlockSpec((tm, D_MODEL), lambda i: (i, 0)),
            mod_spec, mod_spec, mod_spec,
            pl.BlockSpec((1, D_MODEL), const2),
            pl.BlockSpec((A_GROUPS, CHUNK, CHUNK), lambda i: (0, 0, 0)),
            pl.BlockSpec((CHUNK, A_GROUPS), const2),
            pl.BlockSpec((A_WIDTH, D_MODEL), const2, pipeline_mode=pl.Buffered(1)),
            pl.BlockSpec((Q_WIDTH, D_MODEL), const2, pipeline_mode=pl.Buffered(1)),
            pl.BlockSpec((D_MODEL, D_MODEL), const2, pipeline_mode=pl.Buffered(1)),
        ],
        out_specs=(pl.BlockSpec((tm, D_MODEL), lambda i: (i, 0)),
                   pl.BlockSpec((tm, D_MODEL), lambda i: (i, 0))),
        scratch_shapes=[pltpu.VMEM((tm, A_WIDTH), BF16)],
        compiler_params=pltpu.CompilerParams(
            dimension_semantics=("parallel",), vmem_limit_bytes=VMEM_LIMIT),
        name="merge_out",
    )(proj, proj, proj, proj, yb, x2, g1, sc2, sh2, norm2_g.reshape(1, D_MODEL), wsp, bsp_t, wpa, wpb, wout)


FFN_TM = 512
FFN_TF = 512
FFN_HALO = 16


def _ffn_kernel(h_ref, hprev_ref, wa_ref, wb_ref, cwa_ref, cwb_ref, cba_ref, cbb_ref, wd_ref, x1_ref, g2_ref,
                o_ref, hh_ref, up_ref, acc_ref, *, tiles_per_batch):
    i = pl.program_id(0)
    f = pl.program_id(1)

    @pl.when(f == 0)
    def _():
        first = (i % tiles_per_batch) == 0

        @pl.when(first)
        def _():
            hh_ref[:FFN_HALO] = jnp.zeros((FFN_HALO, D_MODEL), BF16)

        @pl.when(jnp.logical_not(first))
        def _():
            hh_ref[:FFN_HALO] = hprev_ref[...]

        hh_ref[FFN_HALO:] = h_ref[...]
        acc_ref[...] = jnp.zeros_like(acc_ref)

    hh = hh_ref[...]
    up_ref[:, :FFN_TF] = _dot(hh, wa_ref[...])
    up_ref[:, FFN_TF:] = _dot(hh, wb_ref[...])

    def conv(cols, cw_ref, cb_ref):
        y = cb_ref[...] + cw_ref[0:1] * up_ref[FFN_HALO - 2:FFN_HALO - 2 + FFN_TM, cols]
        y = y + cw_ref[1:2] * up_ref[FFN_HALO - 1:FFN_HALO - 1 + FFN_TM, cols]
        return y + cw_ref[2:3] * up_ref[FFN_HALO:FFN_HALO + FFN_TM, cols]

    a = conv(slice(0, FFN_TF), cwa_ref, cba_ref)
    b = conv(slice(FFN_TF, 2 * FFN_TF), cwb_ref, cbb_ref)
    gated = (jax.nn.silu(a) * b).astype(BF16)
    acc_ref[...] += _dot(gated, wd_ref[...])

    @pl.when(f == pl.num_programs(1) - 1)
    def _():
        o_ref[...] = x1_ref[...] + g2_ref[...] * acc_ref[...]


def _ffn(h2, x1, g2, w_up, conv_w, conv_b, w_down, seq):
    n = h2.shape[0]
    tm, tf = FFN_TM, FFN_TF
    nf = D_FF // tf
    tiles_per_batch = seq // tm
    halo_blocks = tm // FFN_HALO
    kern = functools.partial(_ffn_kernel, tiles_per_batch=tiles_per_batch)
    return pl.pallas_call(
        kern,
        out_shape=jax.ShapeDtypeStruct((n, D_MODEL), F32),
        grid=(n // tm, nf),
        in_specs=[
            pl.BlockSpec((tm, D_MODEL), lambda i, f: (i, 0)),
            pl.BlockSpec((FFN_HALO, D_MODEL), lambda i, f: (jnp.maximum(i * halo_blocks - 1, 0), 0)),
            pl.BlockSpec((D_MODEL, tf), lambda i, f: (0, f)),
            pl.BlockSpec((D_MODEL, tf), lambda i, f: (0, f + nf)),
            pl.BlockSpec((CONV_W, tf), lambda i, f: (0, f)),
            pl.BlockSpec((CONV_W, tf), lambda i, f: (0, f + nf)),
            pl.BlockSpec((1, tf), lambda i, f: (0, f)),
            pl.BlockSpec((1, tf), lambda i, f: (0, f + nf)),
            pl.BlockSpec((tf, D_MODEL), lambda i, f: (f, 0)),
            pl.BlockSpec((tm, D_MODEL), lambda i, f: (i, 0)),
            pl.BlockSpec((None, 1, D_MODEL), lambda i, f: (i // tiles_per_batch, 0, 0)),
        ],
        out_specs=pl.BlockSpec((tm, D_MODEL), lambda i, f: (i, 0)),
        scratch_shapes=[
            pltpu.VMEM((tm + FFN_HALO, D_MODEL), BF16),
            pltpu.VMEM((tm + FFN_HALO, 2 * tf), F32),
            pltpu.VMEM((tm, D_MODEL), F32),
        ],
        compiler_params=pltpu.CompilerParams(
            dimension_semantics=("parallel", "arbitrary"), vmem_limit_bytes=VMEM_LIMIT),
        name="conv_ffn",
    )(h2, h2, w_up, w_up, conv_w, conv_w, conv_b.reshape(1, 2 * D_FF), conv_b.reshape(1, 2 * D_FF),
      w_down, x1, g2)


def _fused_in_weight(w_in):
    sizes = [A_WIDTH, A_WIDTH, Q_WIDTH, KV_WIDTH, KV_WIDTH, IDX_HEADS * IDX_DIM, IDX_DIM, IDX_HEADS, D_MODEL, D_MODEL]
    offs = [0]
    for s in sizes:
        offs.append(offs[-1] + s)
    w16 = w_in.astype(BF16)
    u, va, q, k, vb, qi, ki, wi, ga, gb = [w16[:, offs[t]:offs[t + 1]] for t in range(len(sizes))]
    z = lambda ncol: jnp.zeros((D_MODEL, ncol), BF16)
    misc = jnp.concatenate([k, vb, ki, z(IDX_DIM), z(IDX_DIM), ki, wi, z(LANES - IDX_HEADS), z(LANES)], axis=1)
    w_cat = jnp.concatenate([ga, gb, u, va, q, misc, qi], axis=1)
    assert w_cat.shape[1] == PROJ_COLS and misc.shape[1] == PROJ_TN
    return w_cat


def kernel(x, c, w_ada, b_ada, norm1_g, w_in, v_norm_g, w_spatial, b_spatial, q_norm_g, k_norm_g, w_proj_a,
           w_proj_b, w_out, norm2_g, w_up, conv_w, conv_b, w_down):
    bsz, seq, _ = x.shape
    n = bsz * seq
    x2 = x.reshape(n, D_MODEL)

    mod = _ada(c, w_ada, b_ada)
    sh1, sc1, g1, sh2, sc2, g2 = [mod[:, t * D_MODEL:(t + 1) * D_MODEL].reshape(bsz, 1, D_MODEL) for t in range(6)]

    proj = _proj(x2, sc1, sh1, norm1_g, _fused_in_weight(w_in), v_norm_g, q_norm_g, k_norm_g, seq)
    yb = _attention(proj, bsz, seq)
    x1, h2 = _merge(proj, yb, x2, g1, sc2, sh2, norm2_g, w_spatial.astype(BF16), b_spatial.T,
                    w_proj_a.astype(BF16), w_proj_b.astype(BF16), w_out.astype(BF16), seq)
    out = _ffn(h2, x1, g2, w_up.astype(BF16), conv_w, conv_b, w_down.astype(BF16), seq)
    return out.reshape(bsz, seq, D_MODEL)
```

```python
import functools
import math

import jax
import jax.numpy as jnp
from jax import lax
from jax.experimental import pallas as pl
from jax.experimental.pallas import tpu as pltpu

F32 = jnp.float32
BF16 = jnp.bfloat16

D_MODEL = 2048
CHUNK = 128
A_GROUPS = 8
A_GROUP_DIM = 128
A_WIDTH = A_GROUPS * A_GROUP_DIM
N_HEADS = 8
HEAD_DIM = 128
N_KV = 2
Q_WIDTH = N_HEADS * HEAD_DIM
KV_WIDTH = N_KV * HEAD_DIM
IDX_HEADS = 16
IDX_DIM = 64
TOPK_MAX = 256
D_FF = 5632
CONV_W = 3
EPS = 1e-6

LANES = 128
VMEM_LIMIT = 56 * 1024 * 1024

PROJ_TN = 1024
COL_GA, COL_GB, COL_U, COL_VA, COL_Q, COL_MISC, COL_QI = 0, 2048, 4096, 5120, 6144, 7168, 8192
PROJ_COLS = 9216
MISC_K, MISC_V, MISC_KI_EVEN, MISC_KI_ODD, MISC_WI = 0, 256, 512, 640, 768

ATT_T = 128
ATT_CH = 256
ATT_SPAN = 256
ATT_ONES_ROWS = 16
ATT_SAFE_SHIFT = 40.0
ATT_BOUND_SLACK = 1.02
NEG_BIG = -1e30
INT_MIN = -(2 ** 31)
KEY_NEG_INF = -2139095041
MOST_NEG_F32 = -3.4028234663852886e38


def _dot(a, b):
    return jnp.dot(a, b, preferred_element_type=F32)


def _dot_nt(a, b):
    return lax.dot_general(a, b, (((1,), (1,)), ((), ())), preferred_element_type=F32)


def _ada_kernel(c_ref, w_ref, b_ref, o_ref):
    cs = jax.nn.silu(c_ref[...]).astype(BF16)
    o_ref[...] = _dot(cs, w_ref[...].astype(BF16)) + b_ref[...]


def _ada(c, w_ada, b_ada):
    bsz = c.shape[0]
    n = w_ada.shape[1]
    tn = 1024
    return pl.pallas_call(
        _ada_kernel,
        out_shape=jax.ShapeDtypeStruct((bsz, n), F32),
        grid=(n // tn,),
        in_specs=[
            pl.BlockSpec((bsz, D_MODEL), lambda j: (0, 0)),
            pl.BlockSpec((D_MODEL, tn), lambda j: (0, j)),
            pl.BlockSpec((1, tn), lambda j: (0, j)),
        ],
        out_specs=pl.BlockSpec((bsz, tn), lambda j: (0, j)),
        compiler_params=pltpu.CompilerParams(
            dimension_semantics=("arbitrary",), vmem_limit_bytes=VMEM_LIMIT),
        name="ada_mod",
    )(c, w_ada, b_ada.reshape(1, n))


def _sigmoid(x):
    return 0.5 * jnp.tanh(0.5 * x) + 0.5


def _head_rms(x, g, scale):
    r = lax.rsqrt(jnp.mean(x * x, axis=-1, keepdims=True) + EPS)
    y = x * r * g
    return y if scale is None else y * scale


def _proj_kernel(x_ref, sc_ref, sh_ref, g_ref, w_ref, vg_ref, qg_ref, kg_ref, o_ref, h_ref):
    j = pl.program_id(1)

    def acc():
        return _dot(h_ref[...], w_ref[...])

    @pl.when(j == 0)
    def _():
        x = x_ref[...]
        r = lax.rsqrt(jnp.mean(x * x, axis=-1, keepdims=True) + EPS)
        h = ((x * r * g_ref[...]) * (1.0 + sc_ref[...]) + sh_ref[...]).astype(BF16)
        h_ref[...] = h
        o_ref[...] = _sigmoid(_dot(h, w_ref[...])).astype(BF16)

    @pl.when((j > 0) & (j < COL_U // PROJ_TN))
    def _():
        o_ref[...] = _sigmoid(acc()).astype(BF16)

    @pl.when(j == COL_U // PROJ_TN)
    def _():
        o_ref[...] = jax.nn.gelu(acc()).astype(BF16)

    @pl.when(j == COL_VA // PROJ_TN)
    def _():
        v = jax.nn.gelu(acc())
        r = lax.rsqrt(jnp.mean(v * v, axis=-1, keepdims=True) + EPS)
        o_ref[...] = (v * r * vg_ref[...]).astype(BF16)

    @pl.when(j == COL_Q // PROJ_TN)
    def _():
        qscale = (HEAD_DIM ** -0.5) * math.log2(math.e)
        a = acc()
        for h in range(N_HEADS):
            sl = slice(h * HEAD_DIM, (h + 1) * HEAD_DIM)
            o_ref[:, sl] = _head_rms(a[:, sl], qg_ref[...], qscale).astype(BF16)

    @pl.when(j == COL_MISC // PROJ_TN)
    def _():
        a = acc()
        for h in range(N_KV):
            sl = slice(MISC_K + h * HEAD_DIM, MISC_K + (h + 1) * HEAD_DIM)
            o_ref[:, sl] = _head_rms(a[:, sl], kg_ref[...], None).astype(BF16)
        o_ref[:, MISC_V:] = a[:, MISC_V:].astype(BF16)

    @pl.when(j == COL_QI // PROJ_TN)
    def _():
        o_ref[...] = acc().astype(BF16)


def _proj(x2, sc1, sh1, norm1_g, w_cat, v_norm_g, q_norm_g, k_norm_g, seq):
    n = x2.shape[0]
    tm = 1024
    tiles_per_batch = seq // tm
    mod_spec = pl.BlockSpec((None, 1, D_MODEL), lambda i, j: (i // tiles_per_batch, 0, 0))
    return pl.pallas_call(
        _proj_kernel,
        out_shape=jax.ShapeDtypeStruct((n, PROJ_COLS), BF16),
        grid=(n // tm, PROJ_COLS // PROJ_TN),
        in_specs=[
            pl.BlockSpec((tm, D_MODEL), lambda i, j: (i, 0)),
            mod_spec, mod_spec,
            pl.BlockSpec((1, D_MODEL), lambda i, j: (0, 0)),
            pl.BlockSpec((D_MODEL, PROJ_TN), lambda i, j: (0, j)),
            pl.BlockSpec((1, A_WIDTH), lambda i, j: (0, 0)),
            pl.BlockSpec((1, HEAD_DIM), lambda i, j: (0, 0)),
            pl.BlockSpec((1, HEAD_DIM), lambda i, j: (0, 0)),
        ],
        out_specs=pl.BlockSpec((tm, PROJ_TN), lambda i, j: (i, j)),
        scratch_shapes=[pltpu.VMEM((tm, D_MODEL), BF16)],
        compiler_params=pltpu.CompilerParams(
            dimension_semantics=("parallel", "arbitrary"), vmem_limit_bytes=VMEM_LIMIT),
        name="proj_in",
    )(x2, sc1, sh1, norm1_g.reshape(1, D_MODEL), w_cat, v_norm_g.reshape(1, A_WIDTH),
      q_norm_g.reshape(1, HEAD_DIM), k_norm_g.reshape(1, HEAD_DIM))


def _sublane_tree(x, op):
    r, c = x.shape
    x = x.reshape(r // 64, 8, 8, c)
    y = x[0]
    for t in range(1, r // 64):
        y = op(y, x[t])
    z = op(op(y[0], y[1]), op(y[2], y[3]))
    return op(z, op(op(y[4], y[5]), op(y[6], y[7])))


def _attn_block(s_eff, i, shifts, fast, q_ref, misc_ref, wi_ref, qi_ref, o_ref, vt_ref, score_ref, bias_ref, p_ref,
                topk):
    nch = s_eff // ATT_CH

    idx_scale = IDX_DIM ** -0.5 * IDX_HEADS ** -0.5
    w_t = wi_ref[...].astype(F32).T * idx_scale
    rhs = [jnp.concatenate([qi_ref[:, 256 * r:256 * r + 128], qi_ref[:, 256 * r + 128:256 * r + 256]], axis=0)
           for r in range(IDX_HEADS // 4)]
    q_pos = i * ATT_T + lax.broadcasted_iota(jnp.int32, (ATT_CH, ATT_T), 1)
    row_iota = lax.broadcasted_iota(jnp.int32, (ATT_CH, ATT_T), 0)

    for c in range(nch):
        rows = slice(c * ATT_CH, (c + 1) * ATT_CH)
        k_even = misc_ref[rows, MISC_KI_EVEN:MISC_KI_EVEN + LANES]
        k_odd = misc_ref[rows, MISC_KI_ODD:MISC_KI_ODD + LANES]
        acc = [None, None]
        for r in range(IDX_HEADS // 4):
            l_even = _dot_nt(k_even, rhs[r])
            l_odd = _dot_nt(k_odd, rhs[r])
            terms = (w_t[4 * r:4 * r + 1] * jnp.maximum(l_even[:, :ATT_T], 0.0),
                     w_t[4 * r + 1:4 * r + 2] * jnp.maximum(l_odd[:, :ATT_T], 0.0),
                     w_t[4 * r + 2:4 * r + 3] * jnp.maximum(l_even[:, ATT_T:], 0.0),
                     w_t[4 * r + 3:4 * r + 4] * jnp.maximum(l_odd[:, ATT_T:], 0.0))
            for t, term in enumerate(terms):
                acc[t % 2] = term if acc[t % 2] is None else acc[t % 2] + term
        score = acc[0] + acc[1]
        if (c + 1) * ATT_CH > s_eff - ATT_SPAN:
            score = jnp.where(c * ATT_CH + row_iota <= q_pos, score, -jnp.inf)
        score_ref[rows, :] = score

    def count_ge(thr_f):
        tot = None
        for c in range(nch):
            ones = jnp.where(score_ref[c * ATT_CH:(c + 1) * ATT_CH, :] >= thr_f, 1.0, 0.0)
            part = _sublane_tree(ones, jnp.add)
            tot = part if tot is None else tot + part
        return jnp.sum(tot, axis=0, keepdims=True)

    def code_to_float(code):
        key = code ^ INT_MIN
        return lax.bitcast_convert_type(key ^ ((key >> 31) & 0x7FFFFFFF), F32)

    def bs_body(it, carry):
        code, n_ge = carry
        cand = code | lax.shift_left(jnp.int32(1), 31 - it)
        cnt = count_ge(code_to_float(cand))
        ok = cnt >= float(topk)
        return jnp.where(ok, cand, code), jnp.where(ok, cnt, n_ge)

    code, n_ge = lax.fori_loop(0, 32, bs_body, (jnp.zeros((1, ATT_T), jnp.int32),
                                                jnp.full((1, ATT_T), float(s_eff), F32)))
    thr_f = code_to_float(jnp.maximum(code ^ INT_MIN, KEY_NEG_INF + 1) ^ INT_MIN)
    excess = jnp.max(jnp.where(thr_f > MOST_NEG_F32, n_ge - float(topk), 0.0))

    @pl.when(excess <= 0.0)
    def _():
        for c in range(nch):
            rows = slice(c * ATT_CH, (c + 1) * ATT_CH)
            bias_ref[rows, :] = jnp.where(score_ref[rows, :] >= thr_f, 0.0, NEG_BIG)

    @pl.when(excess > 0.0)
    def _():
        n_tie = None
        for c in range(nch):
            ones = jnp.where(score_ref[c * ATT_CH:(c + 1) * ATT_CH, :] == thr_f, 1.0, 0.0)
            part = _sublane_tree(ones, jnp.add)
            n_tie = part if n_tie is None else n_tie + part
        n_tie = jnp.sum(n_tie, axis=0, keepdims=True)
        need = float(topk) - (n_ge - n_tie)
        lower = (lax.broadcasted_iota(jnp.int32, (ATT_CH, ATT_CH), 1)
                 <= lax.broadcasted_iota(jnp.int32, (ATT_CH, ATT_CH), 0))
        tril = jnp.where(lower, 1.0, 0.0).astype(BF16)
        run = jnp.zeros((1, ATT_T), F32)
        for c in range(nch):
            rows = slice(c * ATT_CH, (c + 1) * ATT_CH)
            sc = score_ref[rows, :]
            tie = jnp.where(sc == thr_f, 1.0, 0.0)
            rank = _dot(tril, tie.astype(BF16)) + run
            keep = (sc > thr_f) | ((sc == thr_f) & (rank <= need))
            bias_ref[rows, :] = jnp.where(keep, 0.0, NEG_BIG)
            run = run + jnp.sum(_sublane_tree(tie, jnp.add), axis=0, keepdims=True)

    @pl.when(fast)
    def _():
        for p in range(N_HEADS // 2):
            g, q_pair = _head_pair(q_ref, p)
            for c in range(nch):
                rows = slice(c * ATT_CH, (c + 1) * ATT_CH)
                bias = bias_ref[rows, :]
                s = _dot_nt(misc_ref[rows, MISC_K + g * HEAD_DIM:MISC_K + (g + 1) * HEAD_DIM], q_pair)
                s = s + jnp.concatenate([bias, bias], axis=1) - shifts[p]
                p_ref[rows, :] = jnp.exp2(s).astype(BF16)
            out = _dot(vt_ref[g, :, 0:s_eff], p_ref[0:s_eff, :])
            _store_heads(o_ref, p, out[:HEAD_DIM] / out[HEAD_DIM:HEAD_DIM + 1])


def _head_pair(q_ref, p):
    g = (2 * p) // (N_HEADS // N_KV)
    return g, jnp.concatenate([q_ref[:, 256 * p:256 * p + 128], q_ref[:, 256 * p + 128:256 * p + 256]], axis=0)


def _store_heads(o_ref, p, out):
    o_ref[:, 256 * p:256 * p + 128] = out[:, :ATT_T].T.astype(BF16)
    o_ref[:, 256 * p + 128:256 * p + 256] = out[:, ATT_T:].T.astype(BF16)


def _softmax_shifts(q_ref, kmax_ref):
    ones_rows = jnp.ones((8, HEAD_DIM), BF16)
    shifts = []
    for p in range(N_HEADS // 2):
        g, q_pair = _head_pair(q_ref, p)
        q2 = q_pair.astype(F32)
        qn2 = _dot_nt(ones_rows, (q2 * q2).astype(BF16))[0:1]
        kmax = kmax_ref[g, 0:1, :]
        shifts.append(jnp.sqrt(qn2) * (jnp.concatenate([kmax, kmax], axis=1) * ATT_BOUND_SLACK))
    return shifts, jnp.max(jnp.concatenate(shifts, axis=0)) <= ATT_SAFE_SHIFT


def _attn_exact_max(nch, q_ref, misc_ref, o_ref, vt_ref, bias_ref, s_ref, p_ref, seq):
    def zero_tail(c, carry):
        p_ref[pl.ds(pl.multiple_of(c * ATT_CH, ATT_CH), ATT_CH), :] = jnp.zeros((ATT_CH, 2 * ATT_T), BF16)
        return carry

    lax.fori_loop(nch, seq // ATT_CH, zero_tail, 0)
    for p in range(N_HEADS // 2):
        g, q_pair = _head_pair(q_ref, p)

        def logits(c, mx, g=g, q_pair=q_pair):
            rows = pl.ds(pl.multiple_of(c * ATT_CH, ATT_CH), ATT_CH)
            bias = bias_ref[rows, :]
            s = _dot_nt(misc_ref[rows, MISC_K + g * HEAD_DIM:MISC_K + (g + 1) * HEAD_DIM], q_pair)
            s = s + jnp.concatenate([bias, bias], axis=1)
            s_ref[rows, :] = s
            return jnp.maximum(mx, _sublane_tree(s, jnp.maximum))

        mx = lax.fori_loop(0, nch, logits, jnp.full((8, 2 * ATT_T), -jnp.inf, F32))
        m = jnp.max(mx, axis=0, keepdims=True)

        def probs(c, ls, m=m):
            rows = pl.ds(pl.multiple_of(c * ATT_CH, ATT_CH), ATT_CH)
            pm = jnp.exp2(s_ref[rows, :] - m)
            p_ref[rows, :] = pm.astype(BF16)
            return ls + _sublane_tree(pm, jnp.add)

        ls = lax.fori_loop(0, nch, probs, jnp.zeros((8, 2 * ATT_T), F32))
        out = _dot(vt_ref[g], p_ref[...])
        _store_heads(o_ref, p, out[:HEAD_DIM] / jnp.sum(ls, axis=0, keepdims=True))


def _attn_kernel(q_ref, misc_ref, wi_ref, qi_ref, o_ref, vt_ref, kmax_ref, score_ref, bias_ref, s_ref, p_ref,
                 *, seq, topk):
    i = pl.program_id(1)

    @pl.when(i == 0)
    def _():
        for g in range(N_KV):
            for c in range(seq // ATT_CH):
                rows = slice(c * ATT_CH, (c + 1) * ATT_CH)
                vg = misc_ref[rows, MISC_V + g * HEAD_DIM:MISC_V + (g + 1) * HEAD_DIM]
                vt_ref[g, 0:HEAD_DIM, rows] = vg.astype(F32).T.astype(BF16)
            vt_ref[g, HEAD_DIM:, :] = jnp.ones((ATT_ONES_ROWS, seq), BF16)
            kg = misc_ref[:, MISC_K + g * HEAD_DIM:MISC_K + (g + 1) * HEAD_DIM].astype(F32)
            k_norm2 = jnp.max(jnp.sum(kg * kg, axis=1, keepdims=True))
            kmax_ref[g] = jnp.full((8, LANES), jnp.sqrt(k_norm2), F32)

    n_span = (i * ATT_T + ATT_T + ATT_SPAN - 1) // ATT_SPAN
    shifts, fast = _softmax_shifts(q_ref, kmax_ref)
    for ns in range(1, seq // ATT_SPAN + 1):
        @pl.when(n_span == ns)
        def _(ns=ns):
            _attn_block(ns * ATT_SPAN, i, shifts, fast, q_ref, misc_ref, wi_ref, qi_ref, o_ref, vt_ref, score_ref,
                        bias_ref, p_ref, topk)

    @pl.when(jnp.logical_not(fast))
    def _():
        _attn_exact_max(n_span * (ATT_SPAN // ATT_CH), q_ref, misc_ref, o_ref, vt_ref, bias_ref, s_ref, p_ref, seq)


def _attention(proj, bsz, seq):
    n = proj.shape[0]
    nblk = seq // ATT_T
    topk = min(TOPK_MAX, seq // 4)
    kern = functools.partial(_attn_kernel, seq=seq, topk=topk)
    return pl.pallas_call(
        kern,
        out_shape=jax.ShapeDtypeStruct((n, Q_WIDTH), BF16),
        grid=(bsz, nblk),
        in_specs=[
            pl.BlockSpec((ATT_T, Q_WIDTH), lambda b, i: (b * nblk + i, COL_Q // Q_WIDTH)),
            pl.BlockSpec((seq, PROJ_TN), lambda b, i: (b, COL_MISC // PROJ_TN)),
            pl.BlockSpec((ATT_T, LANES), lambda b, i: (b * nblk + i, (COL_MISC + MISC_WI) // LANES)),
            pl.BlockSpec((ATT_T, IDX_HEADS * IDX_DIM), lambda b, i: (b * nblk + i, COL_QI // (IDX_HEADS * IDX_DIM))),
        ],
        out_specs=pl.BlockSpec((ATT_T, Q_WIDTH), lambda b, i: (b * nblk + i, 0)),
        scratch_shapes=[
            pltpu.VMEM((N_KV, HEAD_DIM + ATT_ONES_ROWS, seq), BF16),
            pltpu.VMEM((N_KV, 8, LANES), F32),
            pltpu.VMEM((seq, ATT_T), F32),
            pltpu.VMEM((seq, ATT_T), F32),
            pltpu.VMEM((seq, 2 * ATT_T), F32),
            pltpu.VMEM((seq, 2 * ATT_T), BF16),
        ],
        compiler_params=pltpu.CompilerParams(
            dimension_semantics=("arbitrary", "arbitrary"), vmem_limit_bytes=VMEM_LIMIT),
        name="sparse_attn",
    )(proj, proj, proj, proj)


def _merge_kernel(ga_ref, gb_ref, u_ref, v_ref, yb_ref, x_ref, g1_ref, sc2_ref, sh2_ref, n2g_ref,
                  wsp_ref, bsp_ref, wpa_ref, wpb_ref, wout_ref, x1_ref, h2_ref, ya_ref, *, tm):
    row = lax.broadcasted_iota(jnp.int32, (CHUNK, CHUNK), 0)
    col = lax.broadcasted_iota(jnp.int32, (CHUNK, CHUNK), 1)
    causal = col <= row
    bsp = bsp_ref[...]
    for g in range(A_GROUPS):
        wm = jnp.where(causal, wsp_ref[g], jnp.zeros((), BF16))
        gcols = slice(g * A_GROUP_DIM, (g + 1) * A_GROUP_DIM)
        for ci in range(tm // CHUNK):
            rows = slice(ci * CHUNK, (ci + 1) * CHUNK)
            mixed = _dot(wm, v_ref[rows, gcols]) + bsp[:, g:g + 1]
            ya_ref[rows, gcols] = (u_ref[rows, gcols].astype(F32) * mixed).astype(BF16)

    a = _dot(ya_ref[...], wpa_ref[...])
    b = _dot(yb_ref[...], wpb_ref[...])
    merged = ga_ref[...].astype(F32) * a + gb_ref[...].astype(F32) * b
    o = _dot(merged.astype(BF16), wout_ref[...])
    x1 = x_ref[...] + g1_ref[...] * o
    x1_ref[...] = x1
    r = lax.rsqrt(jnp.mean(x1 * x1, axis=-1, keepdims=True) + EPS)
    h2 = (x1 * r * n2g_ref[...]) * (1.0 + sc2_ref[...]) + sh2_ref[...]
    h2_ref[...] = h2.astype(BF16)


def _merge(proj, yb, x2, g1, sc2, sh2, norm2_g, wsp, bsp_t, wpa, wpb, wout, seq):
    n = x2.shape[0]
    tm = 256
    tiles_per_batch = seq // tm
    mod_spec = pl.BlockSpec((None, 1, D_MODEL), lambda i: (i // tiles_per_batch, 0, 0))
    const2 = lambda i: (0, 0)
    kern = functools.partial(_merge_kernel, tm=tm)
    return pl.pallas_call(
        kern,
        out_shape=(jax.ShapeDtypeStruct((n, D_MODEL), F32), jax.ShapeDtypeStruct((n, D_MODEL), BF16)),
        grid=(n // tm,),
        in_specs=[
            pl.BlockSpec((tm, D_MODEL), lambda i: (i, COL_GA // D_MODEL)),
            pl.BlockSpec((tm, D_MODEL), lambda i: (i, COL_GB // D_MODEL)),
            pl.BlockSpec((tm, A_WIDTH), lambda i: (i, COL_U // A_WIDTH)),
            pl.BlockSpec((tm, A_WIDTH), lambda i: (i, COL_VA // A_WIDTH)),
            pl.BlockSpec((tm, Q_WIDTH), lambda i: (i, 0)),
            pl.BlockSpec((tm, D_MODEL), lambda i: (i, 0)),
            mod_spec, mod_spec, mod_spec,
            pl.BlockSpec((1, D_MODEL), const2),
            pl.BlockSpec((A_GROUPS, CHUNK, CHUNK), lambda i: (0, 0, 0)),
            pl.BlockSpec((CHUNK, A_GROUPS), const2),
            pl.BlockSpec((A_WIDTH, D_MODEL), const2, pipeline_mode=pl.Buffered(1)),
            pl.BlockSpec((Q_WIDTH, D_MODEL), const2, pipeline_mode=pl.Buffered(1)),
            pl.BlockSpec((D_MODEL, D_MODEL), const2, pipeline_mode=pl.Buffered(1)),
        ],
        out_specs=(pl.BlockSpec((tm, D_MODEL), lambda i: (i, 0)),
                   pl.BlockSpec((tm, D_MODEL), lambda i: (i, 0))),
        scratch_shapes=[pltpu.VMEM((tm, A_WIDTH), BF16)],
        compiler_params=pltpu.CompilerParams(
            dimension_semantics=("parallel",), vmem_limit_bytes=VMEM_LIMIT),
        name="merge_out",
    )(proj, proj, proj, proj, yb, x2, g1, sc2, sh2, norm2_g.reshape(1, D_MODEL), wsp, bsp_t, wpa, wpb, wout)


FFN_TM = 512
FFN_TF = 512
FFN_HALO = 8


def _ffn_kernel(h_ref, wa_ref, wb_ref, cwa_ref, cwb_ref, cba_ref, cbb_ref, wd_ref, x1_ref, g2_ref,
                o_ref, up_ref, tail_ref, acc_ref, *, tiles_per_batch):
    i = pl.program_id(0)
    f = pl.program_id(1)
    first = (i % tiles_per_batch) == 0

    @pl.when(first)
    def _():
        up_ref[:FFN_HALO] = jnp.zeros((FFN_HALO, 2 * FFN_TF), F32)

    @pl.when(jnp.logical_not(first))
    def _():
        up_ref[:FFN_HALO] = tail_ref[f]

    @pl.when(f == 0)
    def _():
        acc_ref[...] = jnp.zeros_like(acc_ref)

    h = h_ref[...]
    up_ref[FFN_HALO:, :FFN_TF] = _dot(h, wa_ref[...])
    up_ref[FFN_HALO:, FFN_TF:] = _dot(h, wb_ref[...])

    def conv(cols, cw_ref, cb_ref):
        y = cb_ref[...] + cw_ref[0:1] * up_ref[FFN_HALO - 2:FFN_HALO - 2 + FFN_TM, cols]
        y = y + cw_ref[1:2] * up_ref[FFN_HALO - 1:FFN_HALO - 1 + FFN_TM, cols]
        return y + cw_ref[2:3] * up_ref[FFN_HALO:FFN_HALO + FFN_TM, cols]

    a = conv(slice(0, FFN_TF), cwa_ref, cba_ref)
    b = conv(slice(FFN_TF, 2 * FFN_TF), cwb_ref, cbb_ref)
    gated = (jax.nn.silu(a) * b).astype(BF16)
    acc_ref[...] += _dot(gated, wd_ref[...])
    tail_ref[f] = up_ref[FFN_TM:FFN_TM + FFN_HALO]

    @pl.when(f == pl.num_programs(1) - 1)
    def _():
        o_ref[...] = x1_ref[...] + g2_ref[...] * acc_ref[...]


def _ffn(h2, x1, g2, w_up, conv_w, conv_b, w_down, seq):
    n = h2.shape[0]
    tm, tf = FFN_TM, FFN_TF
    nf = D_FF // tf
    tiles_per_batch = seq // tm
    kern = functools.partial(_ffn_kernel, tiles_per_batch=tiles_per_batch)
    return pl.pallas_call(
        kern,
        out_shape=jax.ShapeDtypeStruct((n, D_MODEL), F32),
        grid=(n // tm, nf),
        in_specs=[
            pl.BlockSpec((tm, D_MODEL), lambda i, f: (i, 0)),
            pl.BlockSpec((D_MODEL, tf), lambda i, f: (0, f)),
            pl.BlockSpec((D_MODEL, tf), lambda i, f: (0, f + nf)),
            pl.BlockSpec((CONV_W, tf), lambda i, f: (0, f)),
            pl.BlockSpec((CONV_W, tf), lambda i, f: (0, f + nf)),
            pl.BlockSpec((1, tf), lambda i, f: (0, f)),
            pl.BlockSpec((1, tf), lambda i, f: (0, f + nf)),
            pl.BlockSpec((tf, D_MODEL), lambda i, f: (f, 0)),
            pl.BlockSpec((tm, D_MODEL), lambda i, f: (i, 0)),
            pl.BlockSpec((None, 1, D_MODEL), lambda i, f: (i // tiles_per_batch, 0, 0)),
        ],
        out_specs=pl.BlockSpec((tm, D_MODEL), lambda i, f: (i, 0)),
        scratch_shapes=[
            pltpu.VMEM((FFN_HALO + tm, 2 * tf), F32),
            pltpu.VMEM((nf, FFN_HALO, 2 * tf), F32),
            pltpu.VMEM((tm, D_MODEL), F32),
        ],
        compiler_params=pltpu.CompilerParams(
            dimension_semantics=("arbitrary", "arbitrary"), vmem_limit_bytes=VMEM_LIMIT),
        name="conv_ffn",
    )(h2, w_up, w_up, conv_w, conv_w, conv_b.reshape(1, 2 * D_FF), conv_b.reshape(1, 2 * D_FF), w_down, x1, g2)


def _fused_in_weight(w_in):
    sizes = [A_WIDTH, A_WIDTH, Q_WIDTH, KV_WIDTH, KV_WIDTH, IDX_HEADS * IDX_DIM, IDX_DIM, IDX_HEADS, D_MODEL, D_MODEL]
    offs = [0]
    for s in sizes:
        offs.append(offs[-1] + s)
    w16 = w_in.astype(BF16)
    u, va, q, k, vb, qi, ki, wi, ga, gb = [w16[:, offs[t]:offs[t + 1]] for t in range(len(sizes))]
    z = lambda ncol: jnp.zeros((D_MODEL, ncol), BF16)
    misc = jnp.concatenate([k, vb, ki, z(IDX_DIM), z(IDX_DIM), ki, wi, z(LANES - IDX_HEADS), z(LANES)], axis=1)
    w_cat = jnp.concatenate([ga, gb, u, va, q, misc, qi], axis=1)
    assert w_cat.shape[1] == PROJ_COLS and misc.shape[1] == PROJ_TN
    return w_cat


def kernel(x, c, w_ada, b_ada, norm1_g, w_in, v_norm_g, w_spatial, b_spatial, q_norm_g, k_norm_g, w_proj_a,
           w_proj_b, w_out, norm2_g, w_up, conv_w, conv_b, w_down):
    bsz, seq, _ = x.shape
    n = bsz * seq
    x2 = x.reshape(n, D_MODEL)

    mod = _ada(c, w_ada, b_ada)
    sh1, sc1, g1, sh2, sc2, g2 = [mod[:, t * D_MODEL:(t + 1) * D_MODEL].reshape(bsz, 1, D_MODEL) for t in range(6)]

    proj = _proj(x2, sc1, sh1, norm1_g, _fused_in_weight(w_in), v_norm_g, q_norm_g, k_norm_g, seq)
    yb = _attention(proj, bsz, seq)
    x1, h2 = _merge(proj, yb, x2, g1, sc2, sh2, norm2_g, w_spatial.astype(BF16), b_spatial.T,
                    w_proj_a.astype(BF16), w_proj_b.astype(BF16), w_out.astype(BF16), seq)
    out = _ffn(h2, x1, g2, w_up.astype(BF16), conv_w, conv_b, w_down.astype(BF16), seq)
    return out.reshape(bsz, seq, D_MODEL)
```

```python
import functools
import math

import jax
import jax.numpy as jnp
from jax import lax
from jax.experimental import pallas as pl
from jax.experimental.pallas import tpu as pltpu

F32 = jnp.float32
BF16 = jnp.bfloat16

D_MODEL = 2048
CHUNK = 128
A_GROUPS = 8
A_GROUP_DIM = 128
A_WIDTH = A_GROUPS * A_GROUP_DIM
N_HEADS = 8
HEAD_DIM = 128
N_KV = 2
Q_WIDTH = N_HEADS * HEAD_DIM
KV_WIDTH = N_KV * HEAD_DIM
IDX_HEADS = 16
IDX_DIM = 64
TOPK_MAX = 256
D_FF = 5632
CONV_W = 3
EPS = 1e-6

LANES = 128
VMEM_LIMIT = 56 * 1024 * 1024

PROJ_TN = 1024
COL_GA, COL_GB, COL_U, COL_VA, COL_Q, COL_MISC, COL_QI = 0, 2048, 4096, 5120, 6144, 7168, 8192
PROJ_COLS = 9216
MISC_K, MISC_V, MISC_KI_EVEN, MISC_KI_ODD, MISC_WI = 0, 256, 512, 640, 768

ATT_T = 128
ATT_CH = 256
ATT_SPAN = 256
ATT_ONES_ROWS = 16
ATT_SAFE_SHIFT = 40.0
ATT_BOUND_SLACK = 1.02
NEG_BIG = -1e30
INT_MIN = -(2 ** 31)
KEY_NEG_INF = -2139095041
MOST_NEG_F32 = -3.4028234663852886e38


def _dot(a, b):
    return jnp.dot(a, b, preferred_element_type=F32)


def _dot_nt(a, b):
    return lax.dot_general(a, b, (((1,), (1,)), ((), ())), preferred_element_type=F32)


def _ada_kernel(c_ref, w_ref, b_ref, o_ref):
    cs = jax.nn.silu(c_ref[...]).astype(BF16)
    o_ref[...] = _dot(cs, w_ref[...].astype(BF16)) + b_ref[...]


def _ada(c, w_ada, b_ada):
    bsz = c.shape[0]
    n = w_ada.shape[1]
    tn = 1024
    return pl.pallas_call(
        _ada_kernel,
        out_shape=jax.ShapeDtypeStruct((bsz, n), F32),
        grid=(n // tn,),
        in_specs=[
            pl.BlockSpec((bsz, D_MODEL), lambda j: (0, 0)),
            pl.BlockSpec((D_MODEL, tn), lambda j: (0, j)),
            pl.BlockSpec((1, tn), lambda j: (0, j)),
        ],
        out_specs=pl.BlockSpec((bsz, tn), lambda j: (0, j)),
        compiler_params=pltpu.CompilerParams(
            dimension_semantics=("arbitrary",), vmem_limit_bytes=VMEM_LIMIT),
        name="ada_mod",
    )(c, w_ada, b_ada.reshape(1, n))


def _sigmoid(x):
    return 0.5 * jnp.tanh(0.5 * x) + 0.5


def _head_rms(x, g, scale):
    r = lax.rsqrt(jnp.mean(x * x, axis=-1, keepdims=True) + EPS)
    y = x * r * g
    return y if scale is None else y * scale


def _proj_kernel(x_ref, sc_ref, sh_ref, g_ref, w_ref, vg_ref, qg_ref, kg_ref, o_ref, h_ref):
    j = pl.program_id(1)

    def acc():
        return _dot(h_ref[...], w_ref[...])

    @pl.when(j == 0)
    def _():
        x = x_ref[...]
        r = lax.rsqrt(jnp.mean(x * x, axis=-1, keepdims=True) + EPS)
        h = ((x * r * g_ref[...]) * (1.0 + sc_ref[...]) + sh_ref[...]).astype(BF16)
        h_ref[...] = h
        o_ref[...] = _sigmoid(_dot(h, w_ref[...])).astype(BF16)

    @pl.when((j > 0) & (j < COL_U // PROJ_TN))
    def _():
        o_ref[...] = _sigmoid(acc()).astype(BF16)

    @pl.when(j == COL_U // PROJ_TN)
    def _():
        o_ref[...] = jax.nn.gelu(acc()).astype(BF16)

    @pl.when(j == COL_VA // PROJ_TN)
    def _():
        v = jax.nn.gelu(acc())
        r = lax.rsqrt(jnp.mean(v * v, axis=-1, keepdims=True) + EPS)
        o_ref[...] = (v * r * vg_ref[...]).astype(BF16)

    @pl.when(j == COL_Q // PROJ_TN)
    def _():
        qscale = (HEAD_DIM ** -0.5) * math.log2(math.e)
        a = acc()
        for h in range(N_HEADS):
            sl = slice(h * HEAD_DIM, (h + 1) * HEAD_DIM)
            o_ref[:, sl] = _head_rms(a[:, sl], qg_ref[...], qscale).astype(BF16)

    @pl.when(j == COL_MISC // PROJ_TN)
    def _():
        a = acc()
        for h in range(N_KV):
            sl = slice(MISC_K + h * HEAD_DIM, MISC_K + (h + 1) * HEAD_DIM)
            o_ref[:, sl] = _head_rms(a[:, sl], kg_ref[...], None).astype(BF16)
        o_ref[:, MISC_V:] = a[:, MISC_V:].astype(BF16)

    @pl.when(j == COL_QI // PROJ_TN)
    def _():
        o_ref[...] = acc().astype(BF16)


def _proj(x2, sc1, sh1, norm1_g, w_cat, v_norm_g, q_norm_g, k_norm_g, seq):
    n = x2.shape[0]
    tm = 1024
    tiles_per_batch = seq // tm
    mod_spec = pl.BlockSpec((None, 1, D_MODEL), lambda i, j: (i // tiles_per_batch, 0, 0))
    return pl.pallas_call(
        _proj_kernel,
        out_shape=jax.ShapeDtypeStruct((n, PROJ_COLS), BF16),
        grid=(n // tm, PROJ_COLS // PROJ_TN),
        in_specs=[
            pl.BlockSpec((tm, D_MODEL), lambda i, j: (i, 0)),
            mod_spec, mod_spec,
            pl.BlockSpec((1, D_MODEL), lambda i, j: (0, 0)),
            pl.BlockSpec((D_MODEL, PROJ_TN), lambda i, j: (0, j)),
            pl.BlockSpec((1, A_WIDTH), lambda i, j: (0, 0)),
            pl.BlockSpec((1, HEAD_DIM), lambda i, j: (0, 0)),
            pl.BlockSpec((1, HEAD_DIM), lambda i, j: (0, 0)),
        ],
        out_specs=pl.BlockSpec((tm, PROJ_TN), lambda i, j: (i, j)),
        scratch_shapes=[pltpu.VMEM((tm, D_MODEL), BF16)],
        compiler_params=pltpu.CompilerParams(
            dimension_semantics=("parallel", "arbitrary"), vmem_limit_bytes=VMEM_LIMIT),
        name="proj_in",
    )(x2, sc1, sh1, norm1_g.reshape(1, D_MODEL), w_cat, v_norm_g.reshape(1, A_WIDTH),
      q_norm_g.reshape(1, HEAD_DIM), k_norm_g.reshape(1, HEAD_DIM))


def _sublane_tree(x, op):
    r, c = x.shape
    x = x.reshape(r // 64, 8, 8, c)
    y = x[0]
    for t in range(1, r // 64):
        y = op(y, x[t])
    z = op(op(y[0], y[1]), op(y[2], y[3]))
    return op(z, op(op(y[4], y[5]), op(y[6], y[7])))


def _attn_block(s_eff, i, shifts, fast, q_ref, misc_ref, wi_ref, qi_ref, o_ref, vt_ref, score_ref, bias_ref, p_ref,
                topk):
    nch = s_eff // ATT_CH

    idx_scale = IDX_DIM ** -0.5 * IDX_HEADS ** -0.5
    w_t = wi_ref[...].astype(F32).T * idx_scale
    rhs = [jnp.concatenate([qi_ref[:, 256 * r:256 * r + 128], qi_ref[:, 256 * r + 128:256 * r + 256]], axis=0)
           for r in range(IDX_HEADS // 4)]
    q_pos = i * ATT_T + lax.broadcasted_iota(jnp.int32, (ATT_CH, ATT_T), 1)
    row_iota = lax.broadcasted_iota(jnp.int32, (ATT_CH, ATT_T), 0)

    for c in range(nch):
        rows = slice(c * ATT_CH, (c + 1) * ATT_CH)
        k_even = misc_ref[rows, MISC_KI_EVEN:MISC_KI_EVEN + LANES]
        k_odd = misc_ref[rows, MISC_KI_ODD:MISC_KI_ODD + LANES]
        acc = [None, None]
        for r in range(IDX_HEADS // 4):
            l_even = _dot_nt(k_even, rhs[r])
            l_odd = _dot_nt(k_odd, rhs[r])
            terms = (w_t[4 * r:4 * r + 1] * jnp.maximum(l_even[:, :ATT_T], 0.0),
                     w_t[4 * r + 1:4 * r + 2] * jnp.maximum(l_odd[:, :ATT_T], 0.0),
                     w_t[4 * r + 2:4 * r + 3] * jnp.maximum(l_even[:, ATT_T:], 0.0),
                     w_t[4 * r + 3:4 * r + 4] * jnp.maximum(l_odd[:, ATT_T:], 0.0))
            for t, term in enumerate(terms):
                acc[t % 2] = term if acc[t % 2] is None else acc[t % 2] + term
        score = acc[0] + acc[1]
        if (c + 1) * ATT_CH > s_eff - ATT_SPAN:
            score = jnp.where(c * ATT_CH + row_iota <= q_pos, score, -jnp.inf)
        score_ref[rows, :] = score

    def count_ge(thr_f):
        tot = None
        for c in range(nch):
            ones = jnp.where(score_ref[c * ATT_CH:(c + 1) * ATT_CH, :] >= thr_f, 1.0, 0.0)
            part = _sublane_tree(ones, jnp.add)
            tot = part if tot is None else tot + part
        return jnp.sum(tot, axis=0, keepdims=True)

    def code_to_float(code):
        key = code ^ INT_MIN
        return lax.bitcast_convert_type(key ^ ((key >> 31) & 0x7FFFFFFF), F32)

    def bs_body(it, carry):
        code, n_ge = carry
        cand = code | lax.shift_left(jnp.int32(1), 31 - it)
        cnt = count_ge(code_to_float(cand))
        ok = cnt >= float(topk)
        return jnp.where(ok, cand, code), jnp.where(ok, cnt, n_ge)

    code, n_ge = lax.fori_loop(0, 32, bs_body, (jnp.zeros((1, ATT_T), jnp.int32),
                                                jnp.full((1, ATT_T), float(s_eff), F32)))
    thr_f = code_to_float(jnp.maximum(code ^ INT_MIN, KEY_NEG_INF + 1) ^ INT_MIN)
    excess = jnp.max(jnp.where(thr_f > MOST_NEG_F32, n_ge - float(topk), 0.0))

    @pl.when(excess <= 0.0)
    def _():
        for c in range(nch):
            rows = slice(c * ATT_CH, (c + 1) * ATT_CH)
            bias_ref[rows, :] = jnp.where(score_ref[rows, :] >= thr_f, 0.0, NEG_BIG)

    @pl.when(excess > 0.0)
    def _():
        n_tie = None
        for c in range(nch):
            ones = jnp.where(score_ref[c * ATT_CH:(c + 1) * ATT_CH, :] == thr_f, 1.0, 0.0)
            part = _sublane_tree(ones, jnp.add)
            n_tie = part if n_tie is None else n_tie + part
        n_tie = jnp.sum(n_tie, axis=0, keepdims=True)
        need = float(topk) - (n_ge - n_tie)
        lower = (lax.broadcasted_iota(jnp.int32, (ATT_CH, ATT_CH), 1)
                 <= lax.broadcasted_iota(jnp.int32, (ATT_CH, ATT_CH), 0))
        tril = jnp.where(lower, 1.0, 0.0).astype(BF16)
        run = jnp.zeros((1, ATT_T), F32)
        for c in range(nch):
            rows = slice(c * ATT_CH, (c + 1) * ATT_CH)
            sc = score_ref[rows, :]
            tie = jnp.where(sc == thr_f, 1.0, 0.0)
            rank = _dot(tril, tie.astype(BF16)) + run
            keep = (sc > thr_f) | ((sc == thr_f) & (rank <= need))
            bias_ref[rows, :] = jnp.where(keep, 0.0, NEG_BIG)
            run = run + jnp.sum(_sublane_tree(tie, jnp.add), axis=0, keepdims=True)

    @pl.when(fast)
    def _():
        for p in range(N_HEADS // 2):
            g, q_pair = _head_pair(q_ref, p)
            for c in range(nch):
                rows = slice(c * ATT_CH, (c + 1) * ATT_CH)
                bias = bias_ref[rows, :]
                s = _dot_nt(misc_ref[rows, MISC_K + g * HEAD_DIM:MISC_K + (g + 1) * HEAD_DIM], q_pair)
                s = s + jnp.concatenate([bias, bias], axis=1) - shifts[p]
                p_ref[rows, :] = jnp.exp2(s).astype(BF16)
            out = _dot(vt_ref[g, :, 0:s_eff], p_ref[0:s_eff, :])
            _store_heads(o_ref, p, out[:HEAD_DIM] / out[HEAD_DIM:HEAD_DIM + 1])


def _head_pair(q_ref, p):
    g = (2 * p) // (N_HEADS // N_KV)
    return g, jnp.concatenate([q_ref[:, 256 * p:256 * p + 128], q_ref[:, 256 * p + 128:256 * p + 256]], axis=0)


def _store_heads(o_ref, p, out):
    o_ref[:, 256 * p:256 * p + 128] = out[:, :ATT_T].T.astype(BF16)
    o_ref[:, 256 * p + 128:256 * p + 256] = out[:, ATT_T:].T.astype(BF16)


def _softmax_shifts(q_ref, kmax_ref):
    ones_rows = jnp.ones((8, HEAD_DIM), BF16)
    shifts = []
    for p in range(N_HEADS // 2):
        g, q_pair = _head_pair(q_ref, p)
        q2 = q_pair.astype(F32)
        qn2 = _dot_nt(ones_rows, (q2 * q2).astype(BF16))[0:1]
        kmax = kmax_ref[g, 0:1, :]
        shifts.append(jnp.sqrt(qn2) * (jnp.concatenate([kmax, kmax], axis=1) * ATT_BOUND_SLACK))
    return shifts, jnp.max(jnp.concatenate(shifts, axis=0)) <= ATT_SAFE_SHIFT


def _attn_exact_max(nch, q_ref, misc_ref, o_ref, vt_ref, bias_ref, s_ref, p_ref, seq):
    def zero_tail(c, carry):
        p_ref[pl.ds(pl.multiple_of(c * ATT_CH, ATT_CH), ATT_CH), :] = jnp.zeros((ATT_CH, 2 * ATT_T), BF16)
        return carry

    lax.fori_loop(nch, seq // ATT_CH, zero_tail, 0)
    for p in range(N_HEADS // 2):
        g, q_pair = _head_pair(q_ref, p)

        def logits(c, mx, g=g, q_pair=q_pair):
            rows = pl.ds(pl.multiple_of(c * ATT_CH, ATT_CH), ATT_CH)
            bias = bias_ref[rows, :]
            s = _dot_nt(misc_ref[rows, MISC_K + g * HEAD_DIM:MISC_K + (g + 1) * HEAD_DIM], q_pair)
            s = s + jnp.concatenate([bias, bias], axis=1)
            s_ref[rows, :] = s
            return jnp.maximum(mx, _sublane_tree(s, jnp.maximum))

        mx = lax.fori_loop(0, nch, logits, jnp.full((8, 2 * ATT_T), -jnp.inf, F32))
        m = jnp.max(mx, axis=0, keepdims=True)

        def probs(c, ls, m=m):
            rows = pl.ds(pl.multiple_of(c * ATT_CH, ATT_CH), ATT_CH)
            pm = jnp.exp2(s_ref[rows, :] - m)
            p_ref[rows, :] = pm.astype(BF16)
            return ls + _sublane_tree(pm, jnp.add)

        ls = lax.fori_loop(0, nch, probs, jnp.zeros((8, 2 * ATT_T), F32))
        out = _dot(vt_ref[g], p_ref[...])
        _store_heads(o_ref, p, out[:HEAD_DIM] / jnp.sum(ls, axis=0, keepdims=True))


def _attn_kernel(q_ref, misc_ref, wi_ref, qi_ref, o_ref, vt_ref, kmax_ref, score_ref, bias_ref, s_ref, p_ref,
                 *, seq, topk):
    i = pl.program_id(1)

    @pl.when(i == 0)
    def _():
        for g in range(N_KV):
            for c in range(seq // ATT_CH):
                rows = slice(c * ATT_CH, (c + 1) * ATT_CH)
                vg = misc_ref[rows, MISC_V + g * HEAD_DIM:MISC_V + (g + 1) * HEAD_DIM]
                vt_ref[g, 0:HEAD_DIM, rows] = vg.astype(F32).T.astype(BF16)
            vt_ref[g, HEAD_DIM:, :] = jnp.ones((ATT_ONES_ROWS, seq), BF16)
            kg = misc_ref[:, MISC_K + g * HEAD_DIM:MISC_K + (g + 1) * HEAD_DIM].astype(F32)
            k_norm2 = jnp.max(jnp.sum(kg * kg, axis=1, keepdims=True))
            kmax_ref[g] = jnp.full((8, LANES), jnp.sqrt(k_norm2), F32)

    n_span = (i * ATT_T + ATT_T + ATT_SPAN - 1) // ATT_SPAN
    shifts, fast = _softmax_shifts(q_ref, kmax_ref)
    for ns in range(1, seq // ATT_SPAN + 1):
        @pl.when(n_span == ns)
        def _(ns=ns):
            _attn_block(ns * ATT_SPAN, i, shifts, fast, q_ref, misc_ref, wi_ref, qi_ref, o_ref, vt_ref, score_ref,
                        bias_ref, p_ref, topk)

    @pl.when(jnp.logical_not(fast))
    def _():
        _attn_exact_max(n_span * (ATT_SPAN // ATT_CH), q_ref, misc_ref, o_ref, vt_ref, bias_ref, s_ref, p_ref, seq)


def _attention(proj, bsz, seq):
    n = proj.shape[0]
    nblk = seq // ATT_T
    topk = min(TOPK_MAX, seq // 4)
    kern = functools.partial(_attn_kernel, seq=seq, topk=topk)
    return pl.pallas_call(
        kern,
        out_shape=jax.ShapeDtypeStruct((n, Q_WIDTH), BF16),
        grid=(bsz, nblk),
        in_specs=[
            pl.BlockSpec((ATT_T, Q_WIDTH), lambda b, i: (b * nblk + i, COL_Q // Q_WIDTH)),
            pl.BlockSpec((seq, PROJ_TN), lambda b, i: (b, COL_MISC // PROJ_TN)),
            pl.BlockSpec((ATT_T, LANES), lambda b, i: (b * nblk + i, (COL_MISC + MISC_WI) // LANES)),
            pl.BlockSpec((ATT_T, IDX_HEADS * IDX_DIM), lambda b, i: (b * nblk + i, COL_QI // (IDX_HEADS * IDX_DIM))),
        ],
        out_specs=pl.BlockSpec((ATT_T, Q_WIDTH), lambda b, i: (b * nblk + i, 0)),
        scratch_shapes=[
            pltpu.VMEM((N_KV, HEAD_DIM + ATT_ONES_ROWS, seq), BF16),
            pltpu.VMEM((N_KV, 8, LANES), F32),
            pltpu.VMEM((seq, ATT_T), F32),
            pltpu.VMEM((seq, ATT_T), F32),
            pltpu.VMEM((seq, 2 * ATT_T), F32),
            pltpu.VMEM((seq, 2 * ATT_T), BF16),
        ],
        compiler_params=pltpu.CompilerParams(
            dimension_semantics=("arbitrary", "arbitrary"), vmem_limit_bytes=VMEM_LIMIT),
        name="sparse_attn",
    )(proj, proj, proj, proj)


def _merge_kernel(ga_ref, gb_ref, u_ref, v_ref, yb_ref, x_ref, g1_ref, sc2_ref, sh2_ref, n2g_ref,
                  wsp_ref, bsp_ref, wpa_ref, wpb_ref, wout_ref, x1_ref, h2_ref, ya_ref, *, tm):
    row = lax.broadcasted_iota(jnp.int32, (CHUNK, CHUNK), 0)
    col = lax.broadcasted_iota(jnp.int32, (CHUNK, CHUNK), 1)
    causal = col <= row
    bsp = bsp_ref[...]
    for g in range(A_GROUPS):
        wm = jnp.where(causal, wsp_ref[g], jnp.zeros((), BF16))
        gcols = slice(g * A_GROUP_DIM, (g + 1) * A_GROUP_DIM)
        for ci in range(tm // CHUNK):
            rows = slice(ci * CHUNK, (ci + 1) * CHUNK)
            mixed = _dot(wm, v_ref[rows, gcols]) + bsp[:, g:g + 1]
            ya_ref[rows, gcols] = (u_ref[rows, gcols].astype(F32) * mixed).astype(BF16)

    a = _dot(ya_ref[...], wpa_ref[...])
    b = _dot(yb_ref[...], wpb_ref[...])
    merged = ga_ref[...].astype(F32) * a + gb_ref[...].astype(F32) * b
    o = _dot(merged.astype(BF16), wout_ref[...])
    x1 = x_ref[...] + g1_ref[...] * o
    x1_ref[...] = x1
    r = lax.rsqrt(jnp.mean(x1 * x1, axis=-1, keepdims=True) + EPS)
    h2 = (x1 * r * n2g_ref[...]) * (1.0 + sc2_ref[...]) + sh2_ref[...]
    h2_ref[...] = h2.astype(BF16)


def _merge(proj, yb, x2, g1, sc2, sh2, norm2_g, wsp, bsp_t, wpa, wpb, wout, seq):
    n = x2.shape[0]
    tm = 256
    tiles_per_batch = seq // tm
    mod_spec = pl.BlockSpec((None, 1, D_MODEL), lambda i: (i // tiles_per_batch, 0, 0))
    const2 = lambda i: (0, 0)
    kern = functools.partial(_merge_kernel, tm=tm)
    return pl.pallas_call(
        kern,
        out_shape=(jax.ShapeDtypeStruct((n, D_MODEL), F32), jax.ShapeDtypeStruct((n, D_MODEL), BF16)),
        grid=(n // tm,),
        in_specs=[
            pl.BlockSpec((tm, D_MODEL), lambda i: (i, COL_GA // D_MODEL)),
            pl.BlockSpec((tm, D_MODEL), lambda i: (i, COL_GB // D_MODEL)),
            pl.BlockSpec((tm, A_WIDTH), lambda i: (i, COL_U // A_WIDTH)),
            pl.BlockSpec((tm, A_WIDTH), lambda i: (i, COL_VA // A_WIDTH)),
            pl.BlockSpec((tm, Q_WIDTH), lambda i: (i, 0)),
            pl.BlockSpec((tm, D_MODEL), lambda i: (i, 0)),
            mod_spec, mod_spec, mod_spec,
            pl.BlockSpec((1, D_MODEL), const2),
            pl.BlockSpec((A_GROUPS, CHUNK, CHUNK), lambda i: (0, 0, 0)),
            pl.BlockSpec((CHUNK, A_GROUPS), const2),
            pl.BlockSpec((A_WIDTH, D_MODEL), const2, pipeline_mode=pl.Buffered(1)),
            pl.BlockSpec((Q_WIDTH, D_MODEL), const2, pipeline_mode=pl.Buffered(1)),
            pl.BlockSpec((D_MODEL, D_MODEL), const2, pipeline_mode=pl.Buffered(1)),
        ],
        out_specs=(pl.BlockSpec((tm, D_MODEL), lambda i: (i, 0)),
                   pl.BlockSpec((tm, D_MODEL), lambda i: (i, 0))),
        scratch_shapes=[pltpu.VMEM((tm, A_WIDTH), BF16)],
        compiler_params=pltpu.CompilerParams(
            dimension_semantics=("parallel",), vmem_limit_bytes=VMEM_LIMIT),
        name="merge_out",
    )(proj, proj, proj, proj, yb, x2, g1, sc2, sh2, norm2_g.reshape(1, D_MODEL), wsp, bsp_t, wpa, wpb, wout)


FFN_TM = 1024
FFN_TF = 512
FFN_HALO = 8


def _ffn_kernel(h_ref, wa_ref, wb_ref, cwa_ref, cwb_ref, cba_ref, cbb_ref, wd_ref, x1_ref, g2_ref,
                o_ref, up_ref, tail_ref, *, tiles_per_batch):
    i = pl.program_id(0)
    f = pl.program_id(1)
    first = (i % tiles_per_batch) == 0

    @pl.when(first)
    def _():
        up_ref[:FFN_HALO] = jnp.zeros((FFN_HALO, 2 * FFN_TF), F32)

    @pl.when(jnp.logical_not(first))
    def _():
        up_ref[:FFN_HALO] = tail_ref[f]

    @pl.when(f == 0)
    def _():
        o_ref[...] = jnp.zeros_like(o_ref)

    h = h_ref[...]
    up_ref[FFN_HALO:, :FFN_TF] = _dot(h, wa_ref[...])
    up_ref[FFN_HALO:, FFN_TF:] = _dot(h, wb_ref[...])

    def conv(cols, cw_ref, cb_ref):
        y = cb_ref[...] + cw_ref[0:1] * up_ref[FFN_HALO - 2:FFN_HALO - 2 + FFN_TM, cols]
        y = y + cw_ref[1:2] * up_ref[FFN_HALO - 1:FFN_HALO - 1 + FFN_TM, cols]
        return y + cw_ref[2:3] * up_ref[FFN_HALO:FFN_HALO + FFN_TM, cols]

    a = conv(slice(0, FFN_TF), cwa_ref, cba_ref)
    b = conv(slice(FFN_TF, 2 * FFN_TF), cwb_ref, cbb_ref)
    gated = (jax.nn.silu(a) * b).astype(BF16)
    o_ref[...] += _dot(gated, wd_ref[...])
    tail_ref[f] = up_ref[FFN_TM:FFN_TM + FFN_HALO]

    @pl.when(f == pl.num_programs(1) - 1)
    def _():
        o_ref[...] = x1_ref[...] + g2_ref[...] * o_ref[...]


def _ffn(h2, x1, g2, w_up, conv_w, conv_b, w_down, seq):
    n = h2.shape[0]
    tm, tf = FFN_TM, FFN_TF
    nf = D_FF // tf
    tiles_per_batch = seq // tm
    kern = functools.partial(_ffn_kernel, tiles_per_batch=tiles_per_batch)
    return pl.pallas_call(
        kern,
        out_shape=jax.ShapeDtypeStruct((n, D_MODEL), F32),
        grid=(n // tm, nf),
        in_specs=[
            pl.BlockSpec((tm, D_MODEL), lambda i, f: (i, 0)),
            pl.BlockSpec((D_MODEL, tf), lambda i, f: (0, f)),
            pl.BlockSpec((D_MODEL, tf), lambda i, f: (0, f + nf)),
            pl.BlockSpec((CONV_W, tf), lambda i, f: (0, f)),
            pl.BlockSpec((CONV_W, tf), lambda i, f: (0, f + nf)),
            pl.BlockSpec((1, tf), lambda i, f: (0, f)),
            pl.BlockSpec((1, tf), lambda i, f: (0, f + nf)),
            pl.BlockSpec((tf, D_MODEL), lambda i, f: (f, 0)),
            pl.BlockSpec((tm, D_MODEL), lambda i, f: (i, 0), pipeline_mode=pl.Buffered(1)),
            pl.BlockSpec((None, 1, D_MODEL), lambda i, f: (i // tiles_per_batch, 0, 0)),
        ],
        out_specs=pl.BlockSpec((tm, D_MODEL), lambda i, f: (i, 0)),
        scratch_shapes=[
            pltpu.VMEM((FFN_HALO + tm, 2 * tf), F32),
            pltpu.VMEM((nf, FFN_HALO, 2 * tf), F32),
        ],
        compiler_params=pltpu.CompilerParams(
            dimension_semantics=("arbitrary", "arbitrary"), vmem_limit_bytes=VMEM_LIMIT),
        name="conv_ffn",
    )(h2, w_up, w_up, conv_w, conv_w, conv_b.reshape(1, 2 * D_FF), conv_b.reshape(1, 2 * D_FF), w_down, x1, g2)


def _fused_in_weight(w_in):
    sizes = [A_WIDTH, A_WIDTH, Q_WIDTH, KV_WIDTH, KV_WIDTH, IDX_HEADS * IDX_DIM, IDX_DIM, IDX_HEADS, D_MODEL, D_MODEL]
    offs = [0]
    for s in sizes:
        offs.append(offs[-1] + s)
    w16 = w_in.astype(BF16)
    u, va, q, k, vb, qi, ki, wi, ga, gb = [w16[:, offs[t]:offs[t + 1]] for t in range(len(sizes))]
    z = lambda ncol: jnp.zeros((D_MODEL, ncol), BF16)
    misc = jnp.concatenate([k, vb, ki, z(IDX_DIM), z(IDX_DIM), ki, wi, z(LANES - IDX_HEADS), z(LANES)], axis=1)
    w_cat = jnp.concatenate([ga, gb, u, va, q, misc, qi], axis=1)
    assert w_cat.shape[1] == PROJ_COLS and misc.shape[1] == PROJ_TN
    return w_cat


def kernel(x, c, w_ada, b_ada, norm1_g, w_in, v_norm_g, w_spatial, b_spatial, q_norm_g, k_norm_g, w_proj_a,
           w_proj_b, w_out, norm2_g, w_up, conv_w, conv_b, w_down):
    bsz, seq, _ = x.shape
    n = bsz * seq
    x2 = x.reshape(n, D_MODEL)

    mod = _ada(c, w_ada, b_ada)
    sh1, sc1, g1, sh2, sc2, g2 = [mod[:, t * D_MODEL:(t + 1) * D_MODEL].reshape(bsz, 1, D_MODEL) for t in range(6)]

    proj = _proj(x2, sc1, sh1, norm1_g, _fused_in_weight(w_in), v_norm_g, q_norm_g, k_norm_g, seq)
    yb = _attention(proj, bsz, seq)
    x1, h2 = _merge(proj, yb, x2, g1, sc2, sh2, norm2_g, w_spatial.astype(BF16), b_spatial.T,
                    w_proj_a.astype(BF16), w_proj_b.astype(BF16), w_out.astype(BF16), seq)
    out = _ffn(h2, x1, g2, w_up.astype(BF16), conv_w, conv_b, w_down.astype(BF16), seq)
    return out.reshape(bsz, seq, D_MODEL)
```

```python
import functools
import math

import jax
import jax.numpy as jnp
from jax import lax
from jax.experimental import pallas as pl
from jax.experimental.pallas import tpu as pltpu

F32 = jnp.float32
BF16 = jnp.bfloat16

D_MODEL = 2048
CHUNK = 128
A_GROUPS = 8
A_GROUP_DIM = 128
A_WIDTH = A_GROUPS * A_GROUP_DIM
N_HEADS = 8
HEAD_DIM = 128
N_KV = 2
Q_WIDTH = N_HEADS * HEAD_DIM
KV_WIDTH = N_KV * HEAD_DIM
IDX_HEADS = 16
IDX_DIM = 64
TOPK_MAX = 256
D_FF = 5632
CONV_W = 3
EPS = 1e-6

LANES = 128
VMEM_LIMIT = 56 * 1024 * 1024

PROJ_TN = 1024
COL_GA, COL_GB, COL_U, COL_VA, COL_Q, COL_MISC, COL_QI = 0, 2048, 4096, 5120, 6144, 7168, 8192
PROJ_COLS = 9216
MISC_K, MISC_V, MISC_KI_EVEN, MISC_KI_ODD, MISC_WI = 0, 256, 512, 640, 768

ATT_T = 128
ATT_CH = 256
ATT_SPAN = 256
ATT_ONES_ROWS = 16
ATT_SAFE_SHIFT = 40.0
ATT_BOUND_SLACK = 1.02
NEG_BIG = -1e30
INT_MIN = -(2 ** 31)
KEY_NEG_INF = -2139095041
MOST_NEG_F32 = -3.4028234663852886e38


def _dot(a, b):
    return jnp.dot(a, b, preferred_element_type=F32)


def _dot_nt(a, b):
    return lax.dot_general(a, b, (((1,), (1,)), ((), ())), preferred_element_type=F32)


def _ada_kernel(c_ref, w_ref, b_ref, o_ref):
    cs = jax.nn.silu(c_ref[...]).astype(BF16)
    o_ref[...] = _dot(cs, w_ref[...].astype(BF16)) + b_ref[...]


def _ada(c, w_ada, b_ada):
    bsz = c.shape[0]
    n = w_ada.shape[1]
    tn = 1024
    return pl.pallas_call(
        _ada_kernel,
        out_shape=jax.ShapeDtypeStruct((bsz, n), F32),
        grid=(n // tn,),
        in_specs=[
            pl.BlockSpec((bsz, D_MODEL), lambda j: (0, 0)),
            pl.BlockSpec((D_MODEL, tn), lambda j: (0, j)),
            pl.BlockSpec((1, tn), lambda j: (0, j)),
        ],
        out_specs=pl.BlockSpec((bsz, tn), lambda j: (0, j)),
        compiler_params=pltpu.CompilerParams(
            dimension_semantics=("arbitrary",), vmem_limit_bytes=VMEM_LIMIT),
        name="ada_mod",
    )(c, w_ada, b_ada.reshape(1, n))


def _sigmoid(x):
    return 0.5 * jnp.tanh(0.5 * x) + 0.5


def _head_rms(x, g, scale):
    r = lax.rsqrt(jnp.mean(x * x, axis=-1, keepdims=True) + EPS)
    y = x * r * g
    return y if scale is None else y * scale


def _proj_kernel(x_ref, sc_ref, sh_ref, g_ref, w_ref, vg_ref, qg_ref, kg_ref, o_ref, h_ref):
    j = pl.program_id(1)

    def acc():
        return _dot(h_ref[...], w_ref[...])

    @pl.when(j == 0)
    def _():
        x = x_ref[...]
        r = lax.rsqrt(jnp.mean(x * x, axis=-1, keepdims=True) + EPS)
        h = ((x * r * g_ref[...]) * (1.0 + sc_ref[...]) + sh_ref[...]).astype(BF16)
        h_ref[...] = h
        o_ref[...] = _sigmoid(_dot(h, w_ref[...])).astype(BF16)

    @pl.when((j > 0) & (j < COL_U // PROJ_TN))
    def _():
        o_ref[...] = _sigmoid(acc()).astype(BF16)

    @pl.when(j == COL_U // PROJ_TN)
    def _():
        o_ref[...] = jax.nn.gelu(acc()).astype(BF16)

    @pl.when(j == COL_VA // PROJ_TN)
    def _():
        v = jax.nn.gelu(acc())
        r = lax.rsqrt(jnp.mean(v * v, axis=-1, keepdims=True) + EPS)
        o_ref[...] = (v * r * vg_ref[...]).astype(BF16)

    @pl.when(j == COL_Q // PROJ_TN)
    def _():
        qscale = (HEAD_DIM ** -0.5) * math.log2(math.e)
        a = acc()
        for h in range(N_HEADS):
            sl = slice(h * HEAD_DIM, (h + 1) * HEAD_DIM)
            o_ref[:, sl] = _head_rms(a[:, sl], qg_ref[...], qscale).astype(BF16)

    @pl.when(j == COL_MISC // PROJ_TN)
    def _():
        a = acc()
        for h in range(N_KV):
            sl = slice(MISC_K + h * HEAD_DIM, MISC_K + (h + 1) * HEAD_DIM)
            o_ref[:, sl] = _head_rms(a[:, sl], kg_ref[...], None).astype(BF16)
        o_ref[:, MISC_V:] = a[:, MISC_V:].astype(BF16)

    @pl.when(j == COL_QI // PROJ_TN)
    def _():
        o_ref[...] = acc().astype(BF16)


def _proj(x2, sc1, sh1, norm1_g, w_cat, v_norm_g, q_norm_g, k_norm_g, seq):
    n = x2.shape[0]
    tm = 1024
    tiles_per_batch = seq // tm
    mod_spec = pl.BlockSpec((None, 1, D_MODEL), lambda i, j: (i // tiles_per_batch, 0, 0))
    return pl.pallas_call(
        _proj_kernel,
        out_shape=jax.ShapeDtypeStruct((n, PROJ_COLS), BF16),
        grid=(n // tm, PROJ_COLS // PROJ_TN),
        in_specs=[
            pl.BlockSpec((tm, D_MODEL), lambda i, j: (i, 0)),
            mod_spec, mod_spec,
            pl.BlockSpec((1, D_MODEL), lambda i, j: (0, 0)),
            pl.BlockSpec((D_MODEL, PROJ_TN), lambda i, j: (0, j)),
            pl.BlockSpec((1, A_WIDTH), lambda i, j: (0, 0)),
            pl.BlockSpec((1, HEAD_DIM), lambda i, j: (0, 0)),
            pl.BlockSpec((1, HEAD_DIM), lambda i, j: (0, 0)),
        ],
        out_specs=pl.BlockSpec((tm, PROJ_TN), lambda i, j: (i, j)),
        scratch_shapes=[pltpu.VMEM((tm, D_MODEL), BF16)],
        compiler_params=pltpu.CompilerParams(
            dimension_semantics=("parallel", "arbitrary"), vmem_limit_bytes=VMEM_LIMIT),
        name="proj_in",
    )(x2, sc1, sh1, norm1_g.reshape(1, D_MODEL), w_cat, v_norm_g.reshape(1, A_WIDTH),
      q_norm_g.reshape(1, HEAD_DIM), k_norm_g.reshape(1, HEAD_DIM))


def _sublane_tree(x, op):
    r, c = x.shape
    x = x.reshape(r // 64, 8, 8, c)
    y = x[0]
    for t in range(1, r // 64):
        y = op(y, x[t])
    z = op(op(y[0], y[1]), op(y[2], y[3]))
    return op(z, op(op(y[4], y[5]), op(y[6], y[7])))


def _attn_block(s_eff, i, shifts, fast, q_ref, misc_ref, wi_ref, qi_ref, o_ref, vt_ref, score_ref, bias_ref, p_ref,
                topk):
    nch = s_eff // ATT_CH

    idx_scale = IDX_DIM ** -0.5 * IDX_HEADS ** -0.5
    w_t = wi_ref[...].astype(F32).T * idx_scale
    rhs = [jnp.concatenate([qi_ref[:, 256 * r:256 * r + 128], qi_ref[:, 256 * r + 128:256 * r + 256]], axis=0)
           for r in range(IDX_HEADS // 4)]
    q_pos = i * ATT_T + lax.broadcasted_iota(jnp.int32, (ATT_CH, ATT_T), 1)
    row_iota = lax.broadcasted_iota(jnp.int32, (ATT_CH, ATT_T), 0)

    for c in range(nch):
        rows = slice(c * ATT_CH, (c + 1) * ATT_CH)
        k_even = misc_ref[rows, MISC_KI_EVEN:MISC_KI_EVEN + LANES]
        k_odd = misc_ref[rows, MISC_KI_ODD:MISC_KI_ODD + LANES]
        acc = [None, None]
        for r in range(IDX_HEADS // 4):
            l_even = _dot_nt(k_even, rhs[r])
            l_odd = _dot_nt(k_odd, rhs[r])
            terms = (w_t[4 * r:4 * r + 1] * jnp.maximum(l_even[:, :ATT_T], 0.0),
                     w_t[4 * r + 1:4 * r + 2] * jnp.maximum(l_odd[:, :ATT_T], 0.0),
                     w_t[4 * r + 2:4 * r + 3] * jnp.maximum(l_even[:, ATT_T:], 0.0),
                     w_t[4 * r + 3:4 * r + 4] * jnp.maximum(l_odd[:, ATT_T:], 0.0))
            for t, term in enumerate(terms):
                acc[t % 2] = term if acc[t % 2] is None else acc[t % 2] + term
        score = acc[0] + acc[1]
        if (c + 1) * ATT_CH > s_eff - ATT_SPAN:
            score = jnp.where(c * ATT_CH + row_iota <= q_pos, score, -jnp.inf)
        score_ref[rows, :] = score

    def count_ge(thr_f):
        tot = None
        for c in range(nch):
            ones = jnp.where(score_ref[c * ATT_CH:(c + 1) * ATT_CH, :] >= thr_f, 1.0, 0.0)
            part = _sublane_tree(ones, jnp.add)
            tot = part if tot is None else tot + part
        return jnp.sum(tot, axis=0, keepdims=True)

    def code_to_float(code):
        key = code ^ INT_MIN
        return lax.bitcast_convert_type(key ^ ((key >> 31) & 0x7FFFFFFF), F32)

    def bs_body(it, carry):
        code, n_ge = carry
        cand = code | lax.shift_left(jnp.int32(1), 31 - it)
        cnt = count_ge(code_to_float(cand))
        ok = cnt >= float(topk)
        return jnp.where(ok, cand, code), jnp.where(ok, cnt, n_ge)

    code, n_ge = lax.fori_loop(0, 32, bs_body, (jnp.zeros((1, ATT_T), jnp.int32),
                                                jnp.full((1, ATT_T), float(s_eff), F32)))
    thr_f = code_to_float(jnp.maximum(code ^ INT_MIN, KEY_NEG_INF + 1) ^ INT_MIN)
    excess = jnp.max(jnp.where(thr_f > MOST_NEG_F32, n_ge - float(topk), 0.0))

    @pl.when(excess <= 0.0)
    def _():
        for c in range(nch):
            rows = slice(c * ATT_CH, (c + 1) * ATT_CH)
            bias_ref[rows, :] = jnp.where(score_ref[rows, :] >= thr_f, 0.0, NEG_BIG)

    @pl.when(excess > 0.0)
    def _():
        n_tie = None
        for c in range(nch):
            ones = jnp.where(score_ref[c * ATT_CH:(c + 1) * ATT_CH, :] == thr_f, 1.0, 0.0)
            part = _sublane_tree(ones, jnp.add)
            n_tie = part if n_tie is None else n_tie + part
        n_tie = jnp.sum(n_tie, axis=0, keepdims=True)
        need = float(topk) - (n_ge - n_tie)
        lower = (lax.broadcasted_iota(jnp.int32, (ATT_CH, ATT_CH), 1)
                 <= lax.broadcasted_iota(jnp.int32, (ATT_CH, ATT_CH), 0))
        tril = jnp.where(lower, 1.0, 0.0).astype(BF16)
        run = jnp.zeros((1, ATT_T), F32)
        for c in range(nch):
            rows = slice(c * ATT_CH, (c + 1) * ATT_CH)
            sc = score_ref[rows, :]
            tie = jnp.where(sc == thr_f, 1.0, 0.0)
            rank = _dot(tril, tie.astype(BF16)) + run
            keep = (sc > thr_f) | ((sc == thr_f) & (rank <= need))
            bias_ref[rows, :] = jnp.where(keep, 0.0, NEG_BIG)
            run = run + jnp.sum(_sublane_tree(tie, jnp.add), axis=0, keepdims=True)

    @pl.when(fast)
    def _():
        pairs = [_head_pair(q_ref, p) for p in range(N_HEADS // 2)]
        for c in range(nch):
            rows = slice(c * ATT_CH, (c + 1) * ATT_CH)
            bias = bias_ref[rows, :]
            bias2 = jnp.concatenate([bias, bias], axis=1)
            for p, (g, q_pair) in enumerate(pairs):
                s = _dot_nt(misc_ref[rows, MISC_K + g * HEAD_DIM:MISC_K + (g + 1) * HEAD_DIM], q_pair)
                p_ref[p, rows, :] = jnp.exp2(s + bias2 - shifts[p]).astype(BF16)
        outs = [_dot(vt_ref[g, :, 0:s_eff], p_ref[p, 0:s_eff, :]) for p, (g, _) in enumerate(pairs)]
        for p, out in enumerate(outs):
            _store_heads(o_ref, p, out[:HEAD_DIM] / out[HEAD_DIM:HEAD_DIM + 1])


def _head_pair(q_ref, p):
    g = (2 * p) // (N_HEADS // N_KV)
    return g, jnp.concatenate([q_ref[:, 256 * p:256 * p + 128], q_ref[:, 256 * p + 128:256 * p + 256]], axis=0)


def _store_heads(o_ref, p, out):
    o_ref[:, 256 * p:256 * p + 128] = out[:, :ATT_T].T.astype(BF16)
    o_ref[:, 256 * p + 128:256 * p + 256] = out[:, ATT_T:].T.astype(BF16)


def _softmax_shifts(q_ref, kmax_ref):
    ones_rows = jnp.ones((8, HEAD_DIM), BF16)
    shifts = []
    for p in range(N_HEADS // 2):
        g, q_pair = _head_pair(q_ref, p)
        q2 = q_pair.astype(F32)
        qn2 = _dot_nt(ones_rows, (q2 * q2).astype(BF16))[0:1]
        kmax = kmax_ref[g, 0:1, :]
        shifts.append(jnp.sqrt(qn2) * (jnp.concatenate([kmax, kmax], axis=1) * ATT_BOUND_SLACK))
    return shifts, jnp.max(jnp.concatenate(shifts, axis=0)) <= ATT_SAFE_SHIFT


def _attn_exact_max(nch, q_ref, misc_ref, o_ref, vt_ref, bias_ref, s_ref, p_ref, seq):
    def zero_tail(c, carry):
        p_ref[0, pl.ds(pl.multiple_of(c * ATT_CH, ATT_CH), ATT_CH), :] = jnp.zeros((ATT_CH, 2 * ATT_T), BF16)
        return carry

    lax.fori_loop(nch, seq // ATT_CH, zero_tail, 0)
    for p in range(N_HEADS // 2):
        g, q_pair = _head_pair(q_ref, p)

        def logits(c, mx, g=g, q_pair=q_pair):
            rows = pl.ds(pl.multiple_of(c * ATT_CH, ATT_CH), ATT_CH)
            bias = bias_ref[rows, :]
            s = _dot_nt(misc_ref[rows, MISC_K + g * HEAD_DIM:MISC_K + (g + 1) * HEAD_DIM], q_pair)
            s = s + jnp.concatenate([bias, bias], axis=1)
            s_ref[rows, :] = s
            return jnp.maximum(mx, _sublane_tree(s, jnp.maximum))

        mx = lax.fori_loop(0, nch, logits, jnp.full((8, 2 * ATT_T), -jnp.inf, F32))
        m = jnp.max(mx, axis=0, keepdims=True)

        def probs(c, ls, m=m):
            rows = pl.ds(pl.multiple_of(c * ATT_CH, ATT_CH), ATT_CH)
            pm = jnp.exp2(s_ref[rows, :] - m)
            p_ref[0, rows, :] = pm.astype(BF16)
            return ls + _sublane_tree(pm, jnp.add)

        ls = lax.fori_loop(0, nch, probs, jnp.zeros((8, 2 * ATT_T), F32))
        out = _dot(vt_ref[g], p_ref[0])
        _store_heads(o_ref, p, out[:HEAD_DIM] / jnp.sum(ls, axis=0, keepdims=True))


def _attn_kernel(q_ref, misc_ref, wi_ref, qi_ref, o_ref, vt_ref, kmax_ref, score_ref, bias_ref, s_ref, p_ref,
                 *, seq, topk):
    i = pl.program_id(1)

    @pl.when(i == 0)
    def _():
        for g in range(N_KV):
            for c in range(seq // ATT_CH):
                rows = slice(c * ATT_CH, (c + 1) * ATT_CH)
                vg = misc_ref[rows, MISC_V + g * HEAD_DIM:MISC_V + (g + 1) * HEAD_DIM]
                vt_ref[g, 0:HEAD_DIM, rows] = vg.astype(F32).T.astype(BF16)
            vt_ref[g, HEAD_DIM:, :] = jnp.ones((ATT_ONES_ROWS, seq), BF16)
            kg = misc_ref[:, MISC_K + g * HEAD_DIM:MISC_K + (g + 1) * HEAD_DIM].astype(F32)
            k_norm2 = jnp.max(jnp.sum(kg * kg, axis=1, keepdims=True))
            kmax_ref[g] = jnp.full((8, LANES), jnp.sqrt(k_norm2), F32)

    n_span = (i * ATT_T + ATT_T + ATT_SPAN - 1) // ATT_SPAN
    shifts, fast = _softmax_shifts(q_ref, kmax_ref)
    for ns in range(1, seq // ATT_SPAN + 1):
        @pl.when(n_span == ns)
        def _(ns=ns):
            _attn_block(ns * ATT_SPAN, i, shifts, fast, q_ref, misc_ref, wi_ref, qi_ref, o_ref, vt_ref, score_ref,
                        bias_ref, p_ref, topk)

    @pl.when(jnp.logical_not(fast))
    def _():
        _attn_exact_max(n_span * (ATT_SPAN // ATT_CH), q_ref, misc_ref, o_ref, vt_ref, bias_ref, s_ref, p_ref, seq)


def _attention(proj, bsz, seq):
    n = proj.shape[0]
    nblk = seq // ATT_T
    topk = min(TOPK_MAX, seq // 4)
    kern = functools.partial(_attn_kernel, seq=seq, topk=topk)
    return pl.pallas_call(
        kern,
        out_shape=jax.ShapeDtypeStruct((n, Q_WIDTH), BF16),
        grid=(bsz, nblk),
        in_specs=[
            pl.BlockSpec((ATT_T, Q_WIDTH), lambda b, i: (b * nblk + i, COL_Q // Q_WIDTH)),
            pl.BlockSpec((seq, PROJ_TN), lambda b, i: (b, COL_MISC // PROJ_TN)),
            pl.BlockSpec((ATT_T, LANES), lambda b, i: (b * nblk + i, (COL_MISC + MISC_WI) // LANES)),
            pl.BlockSpec((ATT_T, IDX_HEADS * IDX_DIM), lambda b, i: (b * nblk + i, COL_QI // (IDX_HEADS * IDX_DIM))),
        ],
        out_specs=pl.BlockSpec((ATT_T, Q_WIDTH), lambda b, i: (b * nblk + i, 0)),
        scratch_shapes=[
            pltpu.VMEM((N_KV, HEAD_DIM + ATT_ONES_ROWS, seq), BF16),
            pltpu.VMEM((N_KV, 8, LANES), F32),
            pltpu.VMEM((seq, ATT_T), F32),
            pltpu.VMEM((seq, ATT_T), F32),
            pltpu.VMEM((seq, 2 * ATT_T), F32),
            pltpu.VMEM((N_HEADS // 2, seq, 2 * ATT_T), BF16),
        ],
        compiler_params=pltpu.CompilerParams(
            dimension_semantics=("arbitrary", "arbitrary"), vmem_limit_bytes=VMEM_LIMIT),
        name="sparse_attn",
    )(proj, proj, proj, proj)


def _merge_kernel(ga_ref, gb_ref, u_ref, v_ref, yb_ref, x_ref, g1_ref, sc2_ref, sh2_ref, n2g_ref,
                  wsp_ref, bsp_ref, wpa_ref, wpb_ref, wout_ref, x1_ref, h2_ref, ya_ref, *, tm):
    row = lax.broadcasted_iota(jnp.int32, (CHUNK, CHUNK), 0)
    col = lax.broadcasted_iota(jnp.int32, (CHUNK, CHUNK), 1)
    causal = col <= row
    bsp = bsp_ref[...]
    for g in range(A_GROUPS):
        wm = jnp.where(causal, wsp_ref[g], jnp.zeros((), BF16))
        gcols = slice(g * A_GROUP_DIM, (g + 1) * A_GROUP_DIM)
        for ci in range(tm // CHUNK):
            rows = slice(ci * CHUNK, (ci + 1) * CHUNK)
            mixed = _dot(wm, v_ref[rows, gcols]) + bsp[:, g:g + 1]
            ya_ref[rows, gcols] = (u_ref[rows, gcols].astype(F32) * mixed).astype(BF16)

    a = _dot(ya_ref[...], wpa_ref[...])
    b = _dot(yb_ref[...], wpb_ref[...])
    merged = ga_ref[...].astype(F32) * a + gb_ref[...].astype(F32) * b
    o = _dot(merged.astype(BF16), wout_ref[...])
    x1 = x_ref[...] + g1_ref[...] * o
    x1_ref[...] = x1
    r = lax.rsqrt(jnp.mean(x1 * x1, axis=-1, keepdims=True) + EPS)
    h2 = (x1 * r * n2g_ref[...]) * (1.0 + sc2_ref[...]) + sh2_ref[...]
    h2_ref[...] = h2.astype(BF16)


def _merge(proj, yb, x2, g1, sc2, sh2, norm2_g, wsp, bsp_t, wpa, wpb, wout, seq):
    n = x2.shape[0]
    tm = 256
    tiles_per_batch = seq // tm
    mod_spec = pl.BlockSpec((None, 1, D_MODEL), lambda i: (i // tiles_per_batch, 0, 0))
    const2 = lambda i: (0, 0)
    kern = functools.partial(_merge_kernel, tm=tm)
    return pl.pallas_call(
        kern,
        out_shape=(jax.ShapeDtypeStruct((n, D_MODEL), F32), jax.ShapeDtypeStruct((n, D_MODEL), BF16)),
        grid=(n // tm,),
        in_specs=[
            pl.BlockSpec((tm, D_MODEL), lambda i: (i, COL_GA // D_MODEL)),
            pl.BlockSpec((tm, D_MODEL), lambda i: (i, COL_GB // D_MODEL)),
            pl.BlockSpec((tm, A_WIDTH), lambda i: (i, COL_U // A_WIDTH)),
            pl.BlockSpec((tm, A_WIDTH), lambda i: (i, COL_VA // A_WIDTH)),
            pl.BlockSpec((tm, Q_WIDTH), lambda i: (i, 0)),
            pl.BlockSpec((tm, D_MODEL), lambda i: (i, 0)),
            mod_spec, mod_spec, mod_spec,
            pl.BlockSpec((1, D_MODEL), const2),
            pl.BlockSpec((A_GROUPS, CHUNK, CHUNK), lambda i: (0, 0, 0)),
            pl.BlockSpec((CHUNK, A_GROUPS), const2),
            pl.BlockSpec((A_WIDTH, D_MODEL), const2, pipeline_mode=pl.Buffered(1)),
            pl.BlockSpec((Q_WIDTH, D_MODEL), const2, pipeline_mode=pl.Buffered(1)),
            pl.BlockSpec((D_MODEL, D_MODEL), const2, pipeline_mode=pl.Buffered(1)),
        ],
        out_specs=(pl.BlockSpec((tm, D_MODEL), lambda i: (i, 0)),
                   pl.BlockSpec((tm, D_MODEL), lambda i: (i, 0))),
        scratch_shapes=[pltpu.VMEM((tm, A_WIDTH), BF16)],
        compiler_params=pltpu.CompilerParams(
            dimension_semantics=("parallel",), vmem_limit_bytes=VMEM_LIMIT),
        name="merge_out",
    )(proj, proj, proj, proj, yb, x2, g1, sc2, sh2, norm2_g.reshape(1, D_MODEL), wsp, bsp_t, wpa, wpb, wout)


FFN_TM = 512
FFN_TF = 512
FFN_HALO = 8


def _ffn_kernel(h_ref, wa_ref, wb_ref, cwa_ref, cwb_ref, cba_ref, cbb_ref, wd_ref, x1_ref, g2_ref,
                o_ref, up_ref, tail_ref, acc_ref, *, tiles_per_batch):
    i = pl.program_id(0)
    f = pl.program_id(1)
    first = (i % tiles_per_batch) == 0

    @pl.when(first)
    def _():
        up_ref[:FFN_HALO] = jnp.zeros((FFN_HALO, 2 * FFN_TF), F32)

    @pl.when(jnp.logical_not(first))
    def _():
        up_ref[:FFN_HALO] = tail_ref[f]

    @pl.when(f == 0)
    def _():
        acc_ref[...] = jnp.zeros_like(acc_ref)

    h = h_ref[...]
    up_ref[FFN_HALO:, :FFN_TF] = _dot(h, wa_ref[...])
    up_ref[FFN_HALO:, FFN_TF:] = _dot(h, wb_ref[...])

    def conv(cols, cw_ref, cb_ref):
        y = cb_ref[...] + cw_ref[0:1] * up_ref[FFN_HALO - 2:FFN_HALO - 2 + FFN_TM, cols]
        y = y + cw_ref[1:2] * up_ref[FFN_HALO - 1:FFN_HALO - 1 + FFN_TM, cols]
        return y + cw_ref[2:3] * up_ref[FFN_HALO:FFN_HALO + FFN_TM, cols]

    a = conv(slice(0, FFN_TF), cwa_ref, cba_ref)
    b = conv(slice(FFN_TF, 2 * FFN_TF), cwb_ref, cbb_ref)
    gated = (jax.nn.silu(a) * b).astype(BF16)
    acc_ref[...] += _dot(gated, wd_ref[...])
    tail_ref[f] = up_ref[FFN_TM:FFN_TM + FFN_HALO]

    @pl.when(f == pl.num_programs(1) - 1)
    def _():
        o_ref[...] = x1_ref[...] + g2_ref[...] * acc_ref[...]


def _ffn(h2, x1, g2, w_up, conv_w, conv_b, w_down, seq):
    n = h2.shape[0]
    tm, tf = FFN_TM, FFN_TF
    nf = D_FF // tf
    tiles_per_batch = seq // tm
    kern = functools.partial(_ffn_kernel, tiles_per_batch=tiles_per_batch)
    return pl.pallas_call(
        kern,
        out_shape=jax.ShapeDtypeStruct((n, D_MODEL), F32),
        grid=(n // tm, nf),
        in_specs=[
            pl.BlockSpec((tm, D_MODEL), lambda i, f: (i, 0)),
            pl.BlockSpec((D_MODEL, tf), lambda i, f: (0, f)),
            pl.BlockSpec((D_MODEL, tf), lambda i, f: (0, f + nf)),
            pl.BlockSpec((CONV_W, tf), lambda i, f: (0, f)),
            pl.BlockSpec((CONV_W, tf), lambda i, f: (0, f + nf)),
            pl.BlockSpec((1, tf), lambda i, f: (0, f)),
            pl.BlockSpec((1, tf), lambda i, f: (0, f + nf)),
            pl.BlockSpec((tf, D_MODEL), lambda i, f: (f, 0)),
            pl.BlockSpec((tm, D_MODEL), lambda i, f: (i, 0)),
            pl.BlockSpec((None, 1, D_MODEL), lambda i, f: (i // tiles_per_batch, 0, 0)),
        ],
        out_specs=pl.BlockSpec((tm, D_MODEL), lambda i, f: (i, 0)),
        scratch_shapes=[
            pltpu.VMEM((FFN_HALO + tm, 2 * tf), F32),
            pltpu.VMEM((nf, FFN_HALO, 2 * tf), F32),
            pltpu.VMEM((tm, D_MODEL), F32),
        ],
        compiler_params=pltpu.CompilerParams(
            dimension_semantics=("arbitrary", "arbitrary"), vmem_limit_bytes=VMEM_LIMIT),
        name="conv_ffn",
    )(h2, w_up, w_up, conv_w, conv_w, conv_b.reshape(1, 2 * D_FF), conv_b.reshape(1, 2 * D_FF), w_down, x1, g2)


def _fused_in_weight(w_in):
    sizes = [A_WIDTH, A_WIDTH, Q_WIDTH, KV_WIDTH, KV_WIDTH, IDX_HEADS * IDX_DIM, IDX_DIM, IDX_HEADS, D_MODEL, D_MODEL]
    offs = [0]
    for s in sizes:
        offs.append(offs[-1] + s)
    w16 = w_in.astype(BF16)
    u, va, q, k, vb, qi, ki, wi, ga, gb = [w16[:, offs[t]:offs[t + 1]] for t in range(len(sizes))]
    z = lambda ncol: jnp.zeros((D_MODEL, ncol), BF16)
    misc = jnp.concatenate([k, vb, ki, z(IDX_DIM), z(IDX_DIM), ki, wi, z(LANES - IDX_HEADS), z(LANES)], axis=1)
    w_cat = jnp.concatenate([ga, gb, u, va, q, misc, qi], axis=1)
    assert w_cat.shape[1] == PROJ_COLS and misc.shape[1] == PROJ_TN
    return w_cat


def kernel(x, c, w_ada, b_ada, norm1_g, w_in, v_norm_g, w_spatial, b_spatial, q_norm_g, k_norm_g, w_proj_a,
           w_proj_b, w_out, norm2_g, w_up, conv_w, conv_b, w_down):
    bsz, seq, _ = x.shape
    n = bsz * seq
    x2 = x.reshape(n, D_MODEL)

    mod = _ada(c, w_ada, b_ada)
    sh1, sc1, g1, sh2, sc2, g2 = [mod[:, t * D_MODEL:(t + 1) * D_MODEL].reshape(bsz, 1, D_MODEL) for t in range(6)]

    proj = _proj(x2, sc1, sh1, norm1_g, _fused_in_weight(w_in), v_norm_g, q_norm_g, k_norm_g, seq)
    yb = _attention(proj, bsz, seq)
    x1, h2 = _merge(proj, yb, x2, g1, sc2, sh2, norm2_g, w_spatial.astype(BF16), b_spatial.T,
                    w_proj_a.astype(BF16), w_proj_b.astype(BF16), w_out.astype(BF16), seq)
    out = _ffn(h2, x1, g2, w_up.astype(BF16), conv_w, conv_b, w_down.astype(BF16), seq)
    return out.reshape(bsz, seq, D_MODEL)
```

```python
import functools
import math

import jax
import jax.numpy as jnp
from jax import lax
from jax.experimental import pallas as pl
from jax.experimental.pallas import tpu as pltpu

F32 = jnp.float32
BF16 = jnp.bfloat16

D_MODEL = 2048
CHUNK = 128
A_GROUPS = 8
A_GROUP_DIM = 128
A_WIDTH = A_GROUPS * A_GROUP_DIM
N_HEADS = 8
HEAD_DIM = 128
N_KV = 2
Q_WIDTH = N_HEADS * HEAD_DIM
KV_WIDTH = N_KV * HEAD_DIM
IDX_HEADS = 16
IDX_DIM = 64
TOPK_MAX = 256
D_FF = 5632
CONV_W = 3
EPS = 1e-6

LANES = 128
VMEM_LIMIT = 56 * 1024 * 1024

PROJ_TN = 1024
COL_GA, COL_GB, COL_U, COL_VA, COL_Q, COL_MISC, COL_QI = 0, 2048, 4096, 5120, 6144, 7168, 8192
PROJ_COLS = 9216
MISC_K, MISC_V, MISC_KI_EVEN, MISC_KI_ODD, MISC_WI = 0, 256, 512, 640, 768

ATT_T = 128
ATT_CH = 256
ATT_SPAN = 256
ATT_ONES_ROWS = 16
ATT_SAFE_SHIFT = 40.0
ATT_BOUND_SLACK = 1.02
NEG_BIG = -1e30
INT_MIN = -(2 ** 31)
KEY_NEG_INF = -2139095041
MOST_NEG_F32 = -3.4028234663852886e38


def _dot(a, b):
    return jnp.dot(a, b, preferred_element_type=F32)


def _dot_nt(a, b):
    return lax.dot_general(a, b, (((1,), (1,)), ((), ())), preferred_element_type=F32)


def _ada_kernel(c_ref, w_ref, b_ref, o_ref):
    cs = jax.nn.silu(c_ref[...]).astype(BF16)
    o_ref[...] = _dot(cs, w_ref[...].astype(BF16)) + b_ref[...]


def _ada(c, w_ada, b_ada):
    bsz = c.shape[0]
    n = w_ada.shape[1]
    tn = 1024
    return pl.pallas_call(
        _ada_kernel,
        out_shape=jax.ShapeDtypeStruct((bsz, n), F32),
        grid=(n // tn,),
        in_specs=[
            pl.BlockSpec((bsz, D_MODEL), lambda j: (0, 0)),
            pl.BlockSpec((D_MODEL, tn), lambda j: (0, j)),
            pl.BlockSpec((1, tn), lambda j: (0, j)),
        ],
        out_specs=pl.BlockSpec((bsz, tn), lambda j: (0, j)),
        compiler_params=pltpu.CompilerParams(
            dimension_semantics=("arbitrary",), vmem_limit_bytes=VMEM_LIMIT),
        name="ada_mod",
    )(c, w_ada, b_ada.reshape(1, n))


def _sigmoid(x):
    return 0.5 * jnp.tanh(0.5 * x) + 0.5


def _head_rms(x, g, scale):
    r = lax.rsqrt(jnp.mean(x * x, axis=-1, keepdims=True) + EPS)
    y = x * r * g
    return y if scale is None else y * scale


def _proj_kernel(x_ref, sc_ref, sh_ref, g_ref, w_ref, vg_ref, qg_ref, kg_ref, o_ref, h_ref):
    j = pl.program_id(1)

    def acc():
        return _dot(h_ref[...], w_ref[...])

    @pl.when(j == 0)
    def _():
        x = x_ref[...]
        r = lax.rsqrt(jnp.mean(x * x, axis=-1, keepdims=True) + EPS)
        h = ((x * r * g_ref[...]) * (1.0 + sc_ref[...]) + sh_ref[...]).astype(BF16)
        h_ref[...] = h
        o_ref[...] = _sigmoid(_dot(h, w_ref[...])).astype(BF16)

    @pl.when((j > 0) & (j < COL_U // PROJ_TN))
    def _():
        o_ref[...] = _sigmoid(acc()).astype(BF16)

    @pl.when(j == COL_U // PROJ_TN)
    def _():
        o_ref[...] = jax.nn.gelu(acc()).astype(BF16)

    @pl.when(j == COL_VA // PROJ_TN)
    def _():
        v = jax.nn.gelu(acc())
        r = lax.rsqrt(jnp.mean(v * v, axis=-1, keepdims=True) + EPS)
        o_ref[...] = (v * r * vg_ref[...]).astype(BF16)

    @pl.when(j == COL_Q // PROJ_TN)
    def _():
        qscale = (HEAD_DIM ** -0.5) * math.log2(math.e)
        a = acc()
        for h in range(N_HEADS):
            sl = slice(h * HEAD_DIM, (h + 1) * HEAD_DIM)
            o_ref[:, sl] = _head_rms(a[:, sl], qg_ref[...], qscale).astype(BF16)

    @pl.when(j == COL_MISC // PROJ_TN)
    def _():
        a = acc()
        for h in range(N_KV):
            sl = slice(MISC_K + h * HEAD_DIM, MISC_K + (h + 1) * HEAD_DIM)
            o_ref[:, sl] = _head_rms(a[:, sl], kg_ref[...], None).astype(BF16)
        o_ref[:, MISC_V:] = a[:, MISC_V:].astype(BF16)

    @pl.when(j == COL_QI // PROJ_TN)
    def _():
        o_ref[...] = acc().astype(BF16)


def _proj(x2, sc1, sh1, norm1_g, w_cat, v_norm_g, q_norm_g, k_norm_g, seq):
    n = x2.shape[0]
    tm = 1024
    tiles_per_batch = seq // tm
    mod_spec = pl.BlockSpec((None, 1, D_MODEL), lambda i, j: (i // tiles_per_batch, 0, 0))
    return pl.pallas_call(
        _proj_kernel,
        out_shape=jax.ShapeDtypeStruct((n, PROJ_COLS), BF16),
        grid=(n // tm, PROJ_COLS // PROJ_TN),
        in_specs=[
            pl.BlockSpec((tm, D_MODEL), lambda i, j: (i, 0)),
            mod_spec, mod_spec,
            pl.BlockSpec((1, D_MODEL), lambda i, j: (0, 0)),
            pl.BlockSpec((D_MODEL, PROJ_TN), lambda i, j: (0, j)),
            pl.BlockSpec((1, A_WIDTH), lambda i, j: (0, 0)),
            pl.BlockSpec((1, HEAD_DIM), lambda i, j: (0, 0)),
            pl.BlockSpec((1, HEAD_DIM), lambda i, j: (0, 0)),
        ],
        out_specs=pl.BlockSpec((tm, PROJ_TN), lambda i, j: (i, j)),
        scratch_shapes=[pltpu.VMEM((tm, D_MODEL), BF16)],
        compiler_params=pltpu.CompilerParams(
            dimension_semantics=("parallel", "arbitrary"), vmem_limit_bytes=VMEM_LIMIT),
        name="proj_in",
    )(x2, sc1, sh1, norm1_g.reshape(1, D_MODEL), w_cat, v_norm_g.reshape(1, A_WIDTH),
      q_norm_g.reshape(1, HEAD_DIM), k_norm_g.reshape(1, HEAD_DIM))


def _sublane_tree(x, op):
    r, c = x.shape
    x = x.reshape(r // 64, 8, 8, c)
    y = x[0]
    for t in range(1, r // 64):
        y = op(y, x[t])
    z = op(op(y[0], y[1]), op(y[2], y[3]))
    return op(z, op(op(y[4], y[5]), op(y[6], y[7])))


def _attn_block(s_eff, i, shifts, fast, q_ref, misc_ref, wi_ref, qi_ref, o_ref, vt_ref, score_ref, bias_ref, p_ref,
                topk):
    nch = s_eff // ATT_CH

    idx_scale = IDX_DIM ** -0.5 * IDX_HEADS ** -0.5
    w_t = wi_ref[...].astype(F32).T * idx_scale
    rhs = [jnp.concatenate([qi_ref[:, 256 * r:256 * r + 128], qi_ref[:, 256 * r + 128:256 * r + 256]], axis=0)
           for r in range(IDX_HEADS // 4)]
    q_pos = i * ATT_T + lax.broadcasted_iota(jnp.int32, (ATT_CH, ATT_T), 1)
    row_iota = lax.broadcasted_iota(jnp.int32, (ATT_CH, ATT_T), 0)

    for c in range(nch):
        rows = slice(c * ATT_CH, (c + 1) * ATT_CH)
        k_even = misc_ref[rows, MISC_KI_EVEN:MISC_KI_EVEN + LANES]
        k_odd = misc_ref[rows, MISC_KI_ODD:MISC_KI_ODD + LANES]
        acc = [None, None]
        for r in range(IDX_HEADS // 4):
            l_even = _dot_nt(k_even, rhs[r])
            l_odd = _dot_nt(k_odd, rhs[r])
            terms = (w_t[4 * r:4 * r + 1] * jnp.maximum(l_even[:, :ATT_T], 0.0),
                     w_t[4 * r + 1:4 * r + 2] * jnp.maximum(l_odd[:, :ATT_T], 0.0),
                     w_t[4 * r + 2:4 * r + 3] * jnp.maximum(l_even[:, ATT_T:], 0.0),
                     w_t[4 * r + 3:4 * r + 4] * jnp.maximum(l_odd[:, ATT_T:], 0.0))
            for t, term in enumerate(terms):
                acc[t % 2] = term if acc[t % 2] is None else acc[t % 2] + term
        score = acc[0] + acc[1]
        if (c + 1) * ATT_CH > s_eff - ATT_SPAN:
            score = jnp.where(c * ATT_CH + row_iota <= q_pos, score, -jnp.inf)
        score_ref[rows, :] = score

    def count_ge(thr_f):
        tot = None
        for c in range(nch):
            ones = jnp.where(score_ref[c * ATT_CH:(c + 1) * ATT_CH, :] >= thr_f, 1.0, 0.0)
            part = _sublane_tree(ones, jnp.add)
            tot = part if tot is None else tot + part
        return jnp.sum(tot, axis=0, keepdims=True)

    def code_to_float(code):
        key = code ^ INT_MIN
        return lax.bitcast_convert_type(key ^ ((key >> 31) & 0x7FFFFFFF), F32)

    def bs_step(cand, cand_f, code, n_ge):
        cnt = count_ge(cand_f)
        ok = cnt >= float(topk)
        return jnp.where(ok, cand, code), jnp.where(ok, cnt, n_ge)

    top = jnp.full((1, ATT_T), INT_MIN, jnp.int32)
    code, n_ge = bs_step(top, code_to_float(top), jnp.zeros((1, ATT_T), jnp.int32),
                         jnp.full((1, ATT_T), float(s_eff), F32))
    to_bits = jnp.where(code < 0, INT_MIN, -1)

    def bs_body(it, carry):
        code, n_ge = carry
        cand = code | lax.shift_left(jnp.int32(1), 31 - it)
        return bs_step(cand, lax.bitcast_convert_type(cand ^ to_bits, F32), code, n_ge)

    code, n_ge = lax.fori_loop(1, 32, bs_body, (code, n_ge))
    thr_f = code_to_float(jnp.maximum(code ^ INT_MIN, KEY_NEG_INF + 1) ^ INT_MIN)
    excess = jnp.max(jnp.where(thr_f > MOST_NEG_F32, n_ge - float(topk), 0.0))

    @pl.when(excess <= 0.0)
    def _():
        for c in range(nch):
            rows = slice(c * ATT_CH, (c + 1) * ATT_CH)
            bias_ref[rows, :] = jnp.where(score_ref[rows, :] >= thr_f, 0.0, NEG_BIG)

    @pl.when(excess > 0.0)
    def _():
        n_tie = None
        for c in range(nch):
            ones = jnp.where(score_ref[c * ATT_CH:(c + 1) * ATT_CH, :] == thr_f, 1.0, 0.0)
            part = _sublane_tree(ones, jnp.add)
            n_tie = part if n_tie is None else n_tie + part
        n_tie = jnp.sum(n_tie, axis=0, keepdims=True)
        need = float(topk) - (n_ge - n_tie)
        lower = (lax.broadcasted_iota(jnp.int32, (ATT_CH, ATT_CH), 1)
                 <= lax.broadcasted_iota(jnp.int32, (ATT_CH, ATT_CH), 0))
        tril = jnp.where(lower, 1.0, 0.0).astype(BF16)
        run = jnp.zeros((1, ATT_T), F32)
        for c in range(nch):
            rows = slice(c * ATT_CH, (c + 1) * ATT_CH)
            sc = score_ref[rows, :]
            tie = jnp.where(sc == thr_f, 1.0, 0.0)
            rank = _dot(tril, tie.astype(BF16)) + run
            keep = (sc > thr_f) | ((sc == thr_f) & (rank <= need))
            bias_ref[rows, :] = jnp.where(keep, 0.0, NEG_BIG)
            run = run + jnp.sum(_sublane_tree(tie, jnp.add), axis=0, keepdims=True)

    @pl.when(fast)
    def _():
        pairs = [_head_pair(q_ref, p) for p in range(N_HEADS // 2)]
        for c in range(nch):
            rows = slice(c * ATT_CH, (c + 1) * ATT_CH)
            bias = bias_ref[rows, :]
            bias2 = jnp.concatenate([bias, bias], axis=1)
            for p, (g, q_pair) in enumerate(pairs):
                s = _dot_nt(misc_ref[rows, MISC_K + g * HEAD_DIM:MISC_K + (g + 1) * HEAD_DIM], q_pair)
                p_ref[p, rows, :] = jnp.exp2(s + bias2 - shifts[p]).astype(BF16)
        outs = [_dot(vt_ref[g, :, 0:s_eff], p_ref[p, 0:s_eff, :]) for p, (g, _) in enumerate(pairs)]
        for p, out in enumerate(outs):
            _store_heads(o_ref, p, out[:HEAD_DIM] / out[HEAD_DIM:HEAD_DIM + 1])


def _head_pair(q_ref, p):
    g = (2 * p) // (N_HEADS // N_KV)
    return g, jnp.concatenate([q_ref[:, 256 * p:256 * p + 128], q_ref[:, 256 * p + 128:256 * p + 256]], axis=0)


def _store_heads(o_ref, p, out):
    o_ref[:, 256 * p:256 * p + 128] = out[:, :ATT_T].T.astype(BF16)
    o_ref[:, 256 * p + 128:256 * p + 256] = out[:, ATT_T:].T.astype(BF16)


def _softmax_shifts(q_ref, kmax_ref):
    ones_rows = jnp.ones((8, HEAD_DIM), BF16)
    shifts = []
    for p in range(N_HEADS // 2):
        g, q_pair = _head_pair(q_ref, p)
        q2 = q_pair.astype(F32)
        qn2 = _dot_nt(ones_rows, (q2 * q2).astype(BF16))[0:1]
        kmax = kmax_ref[g, 0:1, :]
        shifts.append(jnp.sqrt(qn2) * (jnp.concatenate([kmax, kmax], axis=1) * ATT_BOUND_SLACK))
    return shifts, jnp.max(jnp.concatenate(shifts, axis=0)) <= ATT_SAFE_SHIFT


def _attn_exact_max(nch, q_ref, misc_ref, o_ref, vt_ref, bias_ref, s_ref, p_ref, seq):
    def zero_tail(c, carry):
        p_ref[0, pl.ds(pl.multiple_of(c * ATT_CH, ATT_CH), ATT_CH), :] = jnp.zeros((ATT_CH, 2 * ATT_T), BF16)
        return carry

    lax.fori_loop(nch, seq // ATT_CH, zero_tail, 0)
    for p in range(N_HEADS // 2):
        g, q_pair = _head_pair(q_ref, p)

        def logits(c, mx, g=g, q_pair=q_pair):
            rows = pl.ds(pl.multiple_of(c * ATT_CH, ATT_CH), ATT_CH)
            bias = bias_ref[rows, :]
            s = _dot_nt(misc_ref[rows, MISC_K + g * HEAD_DIM:MISC_K + (g + 1) * HEAD_DIM], q_pair)
            s = s + jnp.concatenate([bias, bias], axis=1)
            s_ref[rows, :] = s
            return jnp.maximum(mx, _sublane_tree(s, jnp.maximum))

        mx = lax.fori_loop(0, nch, logits, jnp.full((8, 2 * ATT_T), -jnp.inf, F32))
        m = jnp.max(mx, axis=0, keepdims=True)

        def probs(c, ls, m=m):
            rows = pl.ds(pl.multiple_of(c * ATT_CH, ATT_CH), ATT_CH)
            pm = jnp.exp2(s_ref[rows, :] - m)
            p_ref[0, rows, :] = pm.astype(BF16)
            return ls + _sublane_tree(pm, jnp.add)

        ls = lax.fori_loop(0, nch, probs, jnp.zeros((8, 2 * ATT_T), F32))
        out = _dot(vt_ref[g], p_ref[0])
        _store_heads(o_ref, p, out[:HEAD_DIM] / jnp.sum(ls, axis=0, keepdims=True))


def _attn_kernel(q_ref, misc_ref, wi_ref, qi_ref, o_ref, vt_ref, kmax_ref, score_ref, bias_ref, s_ref, p_ref,
                 *, seq, topk):
    i = pl.program_id(1)

    @pl.when(i == 0)
    def _():
        for g in range(N_KV):
            for c in range(seq // ATT_CH):
                rows = slice(c * ATT_CH, (c + 1) * ATT_CH)
                vg = misc_ref[rows, MISC_V + g * HEAD_DIM:MISC_V + (g + 1) * HEAD_DIM]
                vt_ref[g, 0:HEAD_DIM, rows] = vg.astype(F32).T.astype(BF16)
            vt_ref[g, HEAD_DIM:, :] = jnp.ones((ATT_ONES_ROWS, seq), BF16)
            kg = misc_ref[:, MISC_K + g * HEAD_DIM:MISC_K + (g + 1) * HEAD_DIM].astype(F32)
            k_norm2 = jnp.max(jnp.sum(kg * kg, axis=1, keepdims=True))
            kmax_ref[g] = jnp.full((8, LANES), jnp.sqrt(k_norm2), F32)

    n_span = (i * ATT_T + ATT_T + ATT_SPAN - 1) // ATT_SPAN
    shifts, fast = _softmax_shifts(q_ref, kmax_ref)
    for ns in range(1, seq // ATT_SPAN + 1):
        @pl.when(n_span == ns)
        def _(ns=ns):
            _attn_block(ns * ATT_SPAN, i, shifts, fast, q_ref, misc_ref, wi_ref, qi_ref, o_ref, vt_ref, score_ref,
                        bias_ref, p_ref, topk)

    @pl.when(jnp.logical_not(fast))
    def _():
        _attn_exact_max(n_span * (ATT_SPAN // ATT_CH), q_ref, misc_ref, o_ref, vt_ref, bias_ref, s_ref, p_ref, seq)


def _attention(proj, bsz, seq):
    n = proj.shape[0]
    nblk = seq // ATT_T
    topk = min(TOPK_MAX, seq // 4)
    kern = functools.partial(_attn_kernel, seq=seq, topk=topk)
    return pl.pallas_call(
        kern,
        out_shape=jax.ShapeDtypeStruct((n, Q_WIDTH), BF16),
        grid=(bsz, nblk),
        in_specs=[
            pl.BlockSpec((ATT_T, Q_WIDTH), lambda b, i: (b * nblk + i, COL_Q // Q_WIDTH)),
            pl.BlockSpec((seq, PROJ_TN), lambda b, i: (b, COL_MISC // PROJ_TN)),
            pl.BlockSpec((ATT_T, LANES), lambda b, i: (b * nblk + i, (COL_MISC + MISC_WI) // LANES)),
            pl.BlockSpec((ATT_T, IDX_HEADS * IDX_DIM), lambda b, i: (b * nblk + i, COL_QI // (IDX_HEADS * IDX_DIM))),
        ],
        out_specs=pl.BlockSpec((ATT_T, Q_WIDTH), lambda b, i: (b * nblk + i, 0)),
        scratch_shapes=[
            pltpu.VMEM((N_KV, HEAD_DIM + ATT_ONES_ROWS, seq), BF16),
            pltpu.VMEM((N_KV, 8, LANES), F32),
            pltpu.VMEM((seq, ATT_T), F32),
            pltpu.VMEM((seq, ATT_T), F32),
            pltpu.VMEM((seq, 2 * ATT_T), F32),
            pltpu.VMEM((N_HEADS // 2, seq, 2 * ATT_T), BF16),
        ],
        compiler_params=pltpu.CompilerParams(
            dimension_semantics=("arbitrary", "arbitrary"), vmem_limit_bytes=VMEM_LIMIT),
        name="sparse_attn",
    )(proj, proj, proj, proj)


def _merge_kernel(ga_ref, gb_ref, u_ref, v_ref, yb_ref, x_ref, g1_ref, sc2_ref, sh2_ref, n2g_ref,
                  wsp_ref, bsp_ref, wpa_ref, wpb_ref, wout_ref, x1_ref, h2_ref, ya_ref, *, tm):
    row = lax.broadcasted_iota(jnp.int32, (CHUNK, CHUNK), 0)
    col = lax.broadcasted_iota(jnp.int32, (CHUNK, CHUNK), 1)
    causal = col <= row
    bsp = bsp_ref[...]
    for g in range(A_GROUPS):
        wm = jnp.where(causal, wsp_ref[g], jnp.zeros((), BF16))
        gcols = slice(g * A_GROUP_DIM, (g + 1) * A_GROUP_DIM)
        for ci in range(tm // CHUNK):
            rows = slice(ci * CHUNK, (ci + 1) * CHUNK)
            mixed = _dot(wm, v_ref[rows, gcols]) + bsp[:, g:g + 1]
            ya_ref[rows, gcols] = (u_ref[rows, gcols].astype(F32) * mixed).astype(BF16)

    a = _dot(ya_ref[...], wpa_ref[...])
    b = _dot(yb_ref[...], wpb_ref[...])
    merged = ga_ref[...].astype(F32) * a + gb_ref[...].astype(F32) * b
    o = _dot(merged.astype(BF16), wout_ref[...])
    x1 = x_ref[...] + g1_ref[...] * o
    x1_ref[...] = x1
    r = lax.rsqrt(jnp.mean(x1 * x1, axis=-1, keepdims=True) + EPS)
    h2 = (x1 * r * n2g_ref[...]) * (1.0 + sc2_ref[...]) + sh2_ref[...]
    h2_ref[...] = h2.astype(BF16)


def _merge(proj, yb, x2, g1, sc2, sh2, norm2_g, wsp, bsp_t, wpa, wpb, wout, seq):
    n = x2.shape[0]
    tm = 256
    tiles_per_batch = seq // tm
    mod_spec = pl.BlockSpec((None, 1, D_MODEL), lambda i: (i // tiles_per_batch, 0, 0))
    const2 = lambda i: (0, 0)
    kern = functools.partial(_merge_kernel, tm=tm)
    return pl.pallas_call(
        kern,
        out_shape=(jax.ShapeDtypeStruct((n, D_MODEL), F32), jax.ShapeDtypeStruct((n, D_MODEL), BF16)),
        grid=(n // tm,),
        in_specs=[
            pl.BlockSpec((tm, D_MODEL), lambda i: (i, COL_GA // D_MODEL)),
            pl.BlockSpec((tm, D_MODEL), lambda i: (i, COL_GB // D_MODEL)),
            pl.BlockSpec((tm, A_WIDTH), lambda i: (i, COL_U // A_WIDTH)),
            pl.BlockSpec((tm, A_WIDTH), lambda i: (i, COL_VA // A_WIDTH)),
            pl.BlockSpec((tm, Q_WIDTH), lambda i: (i, 0)),
            pl.BlockSpec((tm, D_MODEL), lambda i: (i, 0)),
            mod_spec, mod_spec, mod_spec,
            pl.BlockSpec((1, D_MODEL), const2),
            pl.BlockSpec((A_GROUPS, CHUNK, CHUNK), lambda i: (0, 0, 0)),
            pl.BlockSpec((CHUNK, A_GROUPS), const2),
            pl.BlockSpec((A_WIDTH, D_MODEL), const2, pipeline_mode=pl.Buffered(1)),
            pl.BlockSpec((Q_WIDTH, D_MODEL), const2, pipeline_mode=pl.Buffered(1)),
            pl.BlockSpec((D_MODEL, D_MODEL), const2, pipeline_mode=pl.Buffered(1)),
        ],
        out_specs=(pl.BlockSpec((tm, D_MODEL), lambda i: (i, 0)),
                   pl.BlockSpec((tm, D_MODEL), lambda i: (i, 0))),
        scratch_shapes=[pltpu.VMEM((tm, A_WIDTH), BF16)],
        compiler_params=pltpu.CompilerParams(
            dimension_semantics=("parallel",), vmem_limit_bytes=VMEM_LIMIT),
        name="merge_out",
    )(proj, proj, proj, proj, yb, x2, g1, sc2, sh2, norm2_g.reshape(1, D_MODEL), wsp, bsp_t, wpa, wpb, wout)


FFN_TM = 512
FFN_TF = 512
FFN_HALO = 8


def _ffn_kernel(h_ref, wa_ref, wb_ref, cwa_ref, cwb_ref, cba_ref, cbb_ref, wd_ref, x1_ref, g2_ref,
                o_ref, up_ref, tail_ref, acc_ref, *, tiles_per_batch):
    i = pl.program_id(0)
    f = pl.program_id(1)
    first = (i % tiles_per_batch) == 0

    @pl.when(first)
    def _():
        up_ref[:FFN_HALO] = jnp.zeros((FFN_HALO, 2 * FFN_TF), F32)

    @pl.when(jnp.logical_not(first))
    def _():
        up_ref[:FFN_HALO] = tail_ref[f]

    @pl.when(f == 0)
    def _():
        acc_ref[...] = jnp.zeros_like(acc_ref)

    h = h_ref[...]
    up_ref[FFN_HALO:, :FFN_TF] = _dot(h, wa_ref[...])
    up_ref[FFN_HALO:, FFN_TF:] = _dot(h, wb_ref[...])

    def conv(cols, cw_ref, cb_ref):
        y = cb_ref[...] + cw_ref[0:1] * up_ref[FFN_HALO - 2:FFN_HALO - 2 + FFN_TM, cols]
        y = y + cw_ref[1:2] * up_ref[FFN_HALO - 1:FFN_HALO - 1 + FFN_TM, cols]
        return y + cw_ref[2:3] * up_ref[FFN_HALO:FFN_HALO + FFN_TM, cols]

    a = conv(slice(0, FFN_TF), cwa_ref, cba_ref)
    b = conv(slice(FFN_TF, 2 * FFN_TF), cwb_ref, cbb_ref)
    gated = (jax.nn.silu(a) * b).astype(BF16)
    acc_ref[...] += _dot(gated, wd_ref[...])
    tail_ref[f] = up_ref[FFN_TM:FFN_TM + FFN_HALO]

    @pl.when(f == pl.num_programs(1) - 1)
    def _():
        o_ref[...] = x1_ref[...] + g2_ref[...] * acc_ref[...]


def _ffn(h2, x1, g2, w_up, conv_w, conv_b, w_down, seq):
    n = h2.shape[0]
    tm, tf = FFN_TM, FFN_TF
    nf = D_FF // tf
    tiles_per_batch = seq // tm
    kern = functools.partial(_ffn_kernel, tiles_per_batch=tiles_per_batch)
    return pl.pallas_call(
        kern,
        out_shape=jax.ShapeDtypeStruct((n, D_MODEL), F32),
        grid=(n // tm, nf),
        in_specs=[
            pl.BlockSpec((tm, D_MODEL), lambda i, f: (i, 0)),
            pl.BlockSpec((D_MODEL, tf), lambda i, f: (0, f)),
            pl.BlockSpec((D_MODEL, tf), lambda i, f: (0, f + nf)),
            pl.BlockSpec((CONV_W, tf), lambda i, f: (0, f)),
            pl.BlockSpec((CONV_W, tf), lambda i, f: (0, f + nf)),
            pl.BlockSpec((1, tf), lambda i, f: (0, f)),
            pl.BlockSpec((1, tf), lambda i, f: (0, f + nf)),
            pl.BlockSpec((tf, D_MODEL), lambda i, f: (f, 0)),
            pl.BlockSpec((tm, D_MODEL), lambda i, f: (i, 0)),
            pl.BlockSpec((None, 1, D_MODEL), lambda i, f: (i // tiles_per_batch, 0, 0)),
        ],
        out_specs=pl.BlockSpec((tm, D_MODEL), lambda i, f: (i, 0)),
        scratch_shapes=[
            pltpu.VMEM((FFN_HALO + tm, 2 * tf), F32),
            pltpu.VMEM((nf, FFN_HALO, 2 * tf), F32),
            pltpu.VMEM((tm, D_MODEL), F32),
        ],
        compiler_params=pltpu.CompilerParams(
            dimension_semantics=("arbitrary", "arbitrary"), vmem_limit_bytes=VMEM_LIMIT),
        name="conv_ffn",
    )(h2, w_up, w_up, conv_w, conv_w, conv_b.reshape(1, 2 * D_FF), conv_b.reshape(1, 2 * D_FF), w_down, x1, g2)


def _fused_in_weight(w_in):
    sizes = [A_WIDTH, A_WIDTH, Q_WIDTH, KV_WIDTH, KV_WIDTH, IDX_HEADS * IDX_DIM, IDX_DIM, IDX_HEADS, D_MODEL, D_MODEL]
    offs = [0]
    for s in sizes:
        offs.append(offs[-1] + s)
    w16 = w_in.astype(BF16)
    u, va, q, k, vb, qi, ki, wi, ga, gb = [w16[:, offs[t]:offs[t + 1]] for t in range(len(sizes))]
    z = lambda ncol: jnp.zeros((D_MODEL, ncol), BF16)
    misc = jnp.concatenate([k, vb, ki, z(IDX_DIM), z(IDX_DIM), ki, wi, z(LANES - IDX_HEADS), z(LANES)], axis=1)
    w_cat = jnp.concatenate([ga, gb, u, va, q, misc, qi], axis=1)
    assert w_cat.shape[1] == PROJ_COLS and misc.shape[1] == PROJ_TN
    return w_cat


def kernel(x, c, w_ada, b_ada, norm1_g, w_in, v_norm_g, w_spatial, b_spatial, q_norm_g, k_norm_g, w_proj_a,
           w_proj_b, w_out, norm2_g, w_up, conv_w, conv_b, w_down):
    bsz, seq, _ = x.shape
    n = bsz * seq
    x2 = x.reshape(n, D_MODEL)

    mod = _ada(c, w_ada, b_ada)
    sh1, sc1, g1, sh2, sc2, g2 = [mod[:, t * D_MODEL:(t + 1) * D_MODEL].reshape(bsz, 1, D_MODEL) for t in range(6)]

    proj = _proj(x2, sc1, sh1, norm1_g, _fused_in_weight(w_in), v_norm_g, q_norm_g, k_norm_g, seq)
    yb = _attention(proj, bsz, seq)
    x1, h2 = _merge(proj, yb, x2, g1, sc2, sh2, norm2_g, w_spatial.astype(BF16), b_spatial.T,
                    w_proj_a.astype(BF16), w_proj_b.astype(BF16), w_out.astype(BF16), seq)
    out = _ffn(h2, x1, g2, w_up.astype(BF16), conv_w, conv_b, w_down.astype(BF16), seq)
    return out.reshape(bsz, seq, D_MODEL)
```

```python
import functools
import math

import jax
import jax.numpy as jnp
from jax import lax
from jax.experimental import pallas as pl
from jax.experimental.pallas import tpu as pltpu

F32 = jnp.float32
BF16 = jnp.bfloat16

D_MODEL = 2048
CHUNK = 128
A_GROUPS = 8
A_GROUP_DIM = 128
A_WIDTH = A_GROUPS * A_GROUP_DIM
N_HEADS = 8
HEAD_DIM = 128
N_KV = 2
Q_WIDTH = N_HEADS * HEAD_DIM
KV_WIDTH = N_KV * HEAD_DIM
IDX_HEADS = 16
IDX_DIM = 64
TOPK_MAX = 256
D_FF = 5632
CONV_W = 3
EPS = 1e-6

LANES = 128
VMEM_LIMIT = 56 * 1024 * 1024

PROJ_TN = 1024
COL_GA, COL_GB, COL_U, COL_VA, COL_Q, COL_MISC, COL_QI = 0, 2048, 4096, 5120, 6144, 7168, 8192
PROJ_COLS = 9216
MISC_K, MISC_V, MISC_KI_EVEN, MISC_KI_ODD, MISC_WI = 0, 256, 512, 640, 768

ATT_T = 128
ATT_CH = 256
ATT_SPAN = 256
ATT_ONES_ROWS = 16
ATT_SAFE_SHIFT = 40.0
ATT_BOUND_SLACK = 1.02
NEG_BIG = -1e30
INT_MIN = -(2 ** 31)
KEY_NEG_INF = -2139095041
MOST_NEG_F32 = -3.4028234663852886e38


def _dot(a, b):
    return jnp.dot(a, b, preferred_element_type=F32)


def _dot_nt(a, b):
    return lax.dot_general(a, b, (((1,), (1,)), ((), ())), preferred_element_type=F32)


def _ada_kernel(c_ref, w_ref, b_ref, o_ref):
    cs = jax.nn.silu(c_ref[...]).astype(BF16)
    o_ref[...] = _dot(cs, w_ref[...].astype(BF16)) + b_ref[...]


def _ada(c, w_ada, b_ada):
    bsz = c.shape[0]
    n = w_ada.shape[1]
    tn = 1024
    return pl.pallas_call(
        _ada_kernel,
        out_shape=jax.ShapeDtypeStruct((bsz, n), F32),
        grid=(n // tn,),
        in_specs=[
            pl.BlockSpec((bsz, D_MODEL), lambda j: (0, 0)),
            pl.BlockSpec((D_MODEL, tn), lambda j: (0, j)),
            pl.BlockSpec((1, tn), lambda j: (0, j)),
        ],
        out_specs=pl.BlockSpec((bsz, tn), lambda j: (0, j)),
        compiler_params=pltpu.CompilerParams(
            dimension_semantics=("arbitrary",), vmem_limit_bytes=VMEM_LIMIT),
        name="ada_mod",
    )(c, w_ada, b_ada.reshape(1, n))


def _sigmoid(x):
    return 0.5 * jnp.tanh(0.5 * x) + 0.5


def _head_rms(x, g, scale):
    r = lax.rsqrt(jnp.mean(x * x, axis=-1, keepdims=True) + EPS)
    y = x * r * g
    return y if scale is None else y * scale


def _proj_kernel(x_ref, sc_ref, sh_ref, g_ref, w_ref, vg_ref, qg_ref, kg_ref, o_ref, h_ref):
    j = pl.program_id(1)

    def acc():
        return _dot(h_ref[...], w_ref[...])

    @pl.when(j == 0)
    def _():
        x = x_ref[...]
        r = lax.rsqrt(jnp.mean(x * x, axis=-1, keepdims=True) + EPS)
        h = ((x * r * g_ref[...]) * (1.0 + sc_ref[...]) + sh_ref[...]).astype(BF16)
        h_ref[...] = h
        o_ref[...] = _sigmoid(_dot(h, w_ref[...])).astype(BF16)

    @pl.when((j > 0) & (j < COL_U // PROJ_TN))
    def _():
        o_ref[...] = _sigmoid(acc()).astype(BF16)

    @pl.when(j == COL_U // PROJ_TN)
    def _():
        o_ref[...] = jax.nn.gelu(acc()).astype(BF16)

    @pl.when(j == COL_VA // PROJ_TN)
    def _():
        v = jax.nn.gelu(acc())
        r = lax.rsqrt(jnp.mean(v * v, axis=-1, keepdims=True) + EPS)
        o_ref[...] = (v * r * vg_ref[...]).astype(BF16)

    @pl.when(j == COL_Q // PROJ_TN)
    def _():
        qscale = (HEAD_DIM ** -0.5) * math.log2(math.e)
        a = acc()
        for h in range(N_HEADS):
            sl = slice(h * HEAD_DIM, (h + 1) * HEAD_DIM)
            o_ref[:, sl] = _head_rms(a[:, sl], qg_ref[...], qscale).astype(BF16)

    @pl.when(j == COL_MISC // PROJ_TN)
    def _():
        a = acc()
        for h in range(N_KV):
            sl = slice(MISC_K + h * HEAD_DIM, MISC_K + (h + 1) * HEAD_DIM)
            o_ref[:, sl] = _head_rms(a[:, sl], kg_ref[...], None).astype(BF16)
        o_ref[:, MISC_V:] = a[:, MISC_V:].astype(BF16)

    @pl.when(j == COL_QI // PROJ_TN)
    def _():
        o_ref[...] = acc().astype(BF16)


def _proj(x2, sc1, sh1, norm1_g, w_cat, v_norm_g, q_norm_g, k_norm_g, seq):
    n = x2.shape[0]
    tm = 1024
    tiles_per_batch = seq // tm
    mod_spec = pl.BlockSpec((None, 1, D_MODEL), lambda i, j: (i // tiles_per_batch, 0, 0))
    return pl.pallas_call(
        _proj_kernel,
        out_shape=jax.ShapeDtypeStruct((n, PROJ_COLS), BF16),
        grid=(n // tm, PROJ_COLS // PROJ_TN),
        in_specs=[
            pl.BlockSpec((tm, D_MODEL), lambda i, j: (i, 0)),
            mod_spec, mod_spec,
            pl.BlockSpec((1, D_MODEL), lambda i, j: (0, 0)),
            pl.BlockSpec((D_MODEL, PROJ_TN), lambda i, j: (0, j)),
            pl.BlockSpec((1, A_WIDTH), lambda i, j: (0, 0)),
            pl.BlockSpec((1, HEAD_DIM), lambda i, j: (0, 0)),
            pl.BlockSpec((1, HEAD_DIM), lambda i, j: (0, 0)),
        ],
        out_specs=pl.BlockSpec((tm, PROJ_TN), lambda i, j: (i, j)),
        scratch_shapes=[pltpu.VMEM((tm, D_MODEL), BF16)],
        compiler_params=pltpu.CompilerParams(
            dimension_semantics=("parallel", "arbitrary"), vmem_limit_bytes=VMEM_LIMIT),
        name="proj_in",
    )(x2, sc1, sh1, norm1_g.reshape(1, D_MODEL), w_cat, v_norm_g.reshape(1, A_WIDTH),
      q_norm_g.reshape(1, HEAD_DIM), k_norm_g.reshape(1, HEAD_DIM))


def _sublane_tree(x, op):
    r, c = x.shape
    x = x.reshape(r // 64, 8, 8, c)
    y = x[0]
    for t in range(1, r // 64):
        y = op(y, x[t])
    z = op(op(y[0], y[1]), op(y[2], y[3]))
    return op(z, op(op(y[4], y[5]), op(y[6], y[7])))


def _attn_block(s_eff, i, shifts, fast, q_ref, misc_ref, wi_ref, qi_ref, o_ref, vt_ref, score_ref, bias_ref, p_ref,
                topk):
    nch = s_eff // ATT_CH

    idx_scale = IDX_DIM ** -0.5 * IDX_HEADS ** -0.5
    w_t = wi_ref[...].astype(F32).T * idx_scale
    rhs = [jnp.concatenate([qi_ref[:, 256 * r:256 * r + 128], qi_ref[:, 256 * r + 128:256 * r + 256]], axis=0)
           for r in range(IDX_HEADS // 4)]
    q_pos = i * ATT_T + lax.broadcasted_iota(jnp.int32, (ATT_CH, ATT_T), 1)
    row_iota = lax.broadcasted_iota(jnp.int32, (ATT_CH, ATT_T), 0)

    for c in range(nch):
        rows = slice(c * ATT_CH, (c + 1) * ATT_CH)
        k_even = misc_ref[rows, MISC_KI_EVEN:MISC_KI_EVEN + LANES]
        k_odd = misc_ref[rows, MISC_KI_ODD:MISC_KI_ODD + LANES]
        acc = [None, None]
        for r in range(IDX_HEADS // 4):
            l_even = _dot_nt(k_even, rhs[r])
            l_odd = _dot_nt(k_odd, rhs[r])
            terms = (w_t[4 * r:4 * r + 1] * jnp.maximum(l_even[:, :ATT_T], 0.0),
                     w_t[4 * r + 1:4 * r + 2] * jnp.maximum(l_odd[:, :ATT_T], 0.0),
                     w_t[4 * r + 2:4 * r + 3] * jnp.maximum(l_even[:, ATT_T:], 0.0),
                     w_t[4 * r + 3:4 * r + 4] * jnp.maximum(l_odd[:, ATT_T:], 0.0))
            for t, term in enumerate(terms):
                acc[t % 2] = term if acc[t % 2] is None else acc[t % 2] + term
        score = acc[0] + acc[1]
        if (c + 1) * ATT_CH > s_eff - ATT_SPAN:
            score = jnp.where(c * ATT_CH + row_iota <= q_pos, score, -jnp.inf)
        score_ref[rows, :] = score

    def count_ge(thr_f):
        tot = None
        for c in range(nch):
            ones = jnp.where(score_ref[c * ATT_CH:(c + 1) * ATT_CH, :] >= thr_f, 1.0, 0.0)
            part = _sublane_tree(ones, jnp.add)
            tot = part if tot is None else tot + part
        return jnp.sum(tot, axis=0, keepdims=True)

    def code_to_float(code):
        key = code ^ INT_MIN
        return lax.bitcast_convert_type(key ^ ((key >> 31) & 0x7FFFFFFF), F32)

    def bs_step(cand, cand_f, code, n_ge):
        cnt = count_ge(cand_f)
        ok = cnt >= float(topk)
        return jnp.where(ok, cand, code), jnp.where(ok, cnt, n_ge)

    top = jnp.full((1, ATT_T), INT_MIN, jnp.int32)
    code, n_ge = bs_step(top, code_to_float(top), jnp.zeros((1, ATT_T), jnp.int32),
                         jnp.full((1, ATT_T), float(s_eff), F32))
    to_bits = jnp.where(code < 0, INT_MIN, -1)

    def bs_body(it, carry):
        code, n_ge = carry
        cand = code | lax.shift_left(jnp.int32(1), 31 - it)
        return bs_step(cand, lax.bitcast_convert_type(cand ^ to_bits, F32), code, n_ge)

    code, n_ge = lax.fori_loop(1, 32, bs_body, (code, n_ge))
    thr_f = code_to_float(jnp.maximum(code ^ INT_MIN, KEY_NEG_INF + 1) ^ INT_MIN)
    excess = jnp.max(jnp.where(thr_f > MOST_NEG_F32, n_ge - float(topk), 0.0))

    for c in range(nch):
        rows = slice(c * ATT_CH, (c + 1) * ATT_CH)
        bias_ref[rows, :] = jnp.where(score_ref[rows, :] >= thr_f, 0.0, NEG_BIG)

    @pl.when(excess > 0.0)
    def _():
        n_tie = None
        for c in range(nch):
            ones = jnp.where(score_ref[c * ATT_CH:(c + 1) * ATT_CH, :] == thr_f, 1.0, 0.0)
            part = _sublane_tree(ones, jnp.add)
            n_tie = part if n_tie is None else n_tie + part
        n_tie = jnp.sum(n_tie, axis=0, keepdims=True)
        need = float(topk) - (n_ge - n_tie)
        lower = (lax.broadcasted_iota(jnp.int32, (ATT_CH, ATT_CH), 1)
                 <= lax.broadcasted_iota(jnp.int32, (ATT_CH, ATT_CH), 0))
        tril = jnp.where(lower, 1.0, 0.0).astype(BF16)
        run = jnp.zeros((1, ATT_T), F32)
        for c in range(nch):
            rows = slice(c * ATT_CH, (c + 1) * ATT_CH)
            sc = score_ref[rows, :]
            tie = jnp.where(sc == thr_f, 1.0, 0.0)
            rank = _dot(tril, tie.astype(BF16)) + run
            keep = (sc > thr_f) | ((sc == thr_f) & (rank <= need))
            bias_ref[rows, :] = jnp.where(keep, 0.0, NEG_BIG)
            run = run + jnp.sum(_sublane_tree(tie, jnp.add), axis=0, keepdims=True)

    @pl.when(fast)
    def _():
        pairs = [_head_pair(q_ref, p) for p in range(N_HEADS // 2)]
        for c in range(nch):
            rows = slice(c * ATT_CH, (c + 1) * ATT_CH)
            bias = bias_ref[rows, :]
            bias2 = jnp.concatenate([bias, bias], axis=1)
            for p, (g, q_pair) in enumerate(pairs):
                s = _dot_nt(misc_ref[rows, MISC_K + g * HEAD_DIM:MISC_K + (g + 1) * HEAD_DIM], q_pair)
                p_ref[p, rows, :] = jnp.exp2(s + bias2 - shifts[p]).astype(BF16)
        outs = [_dot(vt_ref[g, :, 0:s_eff], p_ref[p, 0:s_eff, :]) for p, (g, _) in enumerate(pairs)]
        for p, out in enumerate(outs):
            _store_heads(o_ref, p, out[:HEAD_DIM] / out[HEAD_DIM:HEAD_DIM + 1])


def _head_pair(q_ref, p):
    g = (2 * p) // (N_HEADS // N_KV)
    return g, jnp.concatenate([q_ref[:, 256 * p:256 * p + 128], q_ref[:, 256 * p + 128:256 * p + 256]], axis=0)


def _store_heads(o_ref, p, out):
    o_ref[:, 256 * p:256 * p + 128] = out[:, :ATT_T].T.astype(BF16)
    o_ref[:, 256 * p + 128:256 * p + 256] = out[:, ATT_T:].T.astype(BF16)


def _softmax_shifts(q_ref, kmax_ref):
    ones_rows = jnp.ones((8, HEAD_DIM), BF16)
    pairs = [_head_pair(q_ref, p) for p in range(N_HEADS // 2)]
    squares = [jnp.square(q_pair.astype(F32)).astype(BF16) for _, q_pair in pairs]
    norms2 = [_dot_nt(ones_rows, sq)[0:1] for sq in squares]
    shifts = []
    for (g, _), qn2 in zip(pairs, norms2):
        kmax = kmax_ref[g, 0:1, :]
        shifts.append(jnp.sqrt(qn2) * (jnp.concatenate([kmax, kmax], axis=1) * ATT_BOUND_SLACK))
    return shifts, jnp.max(jnp.concatenate(shifts, axis=0)) <= ATT_SAFE_SHIFT


def _attn_exact_max(nch, q_ref, misc_ref, o_ref, vt_ref, bias_ref, s_ref, p_ref, seq):
    def zero_tail(c, carry):
        p_ref[0, pl.ds(pl.multiple_of(c * ATT_CH, ATT_CH), ATT_CH), :] = jnp.zeros((ATT_CH, 2 * ATT_T), BF16)
        return carry

    lax.fori_loop(nch, seq // ATT_CH, zero_tail, 0)
    for p in range(N_HEADS // 2):
        g, q_pair = _head_pair(q_ref, p)

        def logits(c, mx, g=g, q_pair=q_pair):
            rows = pl.ds(pl.multiple_of(c * ATT_CH, ATT_CH), ATT_CH)
            bias = bias_ref[rows, :]
            s = _dot_nt(misc_ref[rows, MISC_K + g * HEAD_DIM:MISC_K + (g + 1) * HEAD_DIM], q_pair)
            s = s + jnp.concatenate([bias, bias], axis=1)
            s_ref[rows, :] = s
            return jnp.maximum(mx, _sublane_tree(s, jnp.maximum))

        mx = lax.fori_loop(0, nch, logits, jnp.full((8, 2 * ATT_T), -jnp.inf, F32))
        m = jnp.max(mx, axis=0, keepdims=True)

        def probs(c, ls, m=m):
            rows = pl.ds(pl.multiple_of(c * ATT_CH, ATT_CH), ATT_CH)
            pm = jnp.exp2(s_ref[rows, :] - m)
            p_ref[0, rows, :] = pm.astype(BF16)
            return ls + _sublane_tree(pm, jnp.add)

        ls = lax.fori_loop(0, nch, probs, jnp.zeros((8, 2 * ATT_T), F32))
        out = _dot(vt_ref[g], p_ref[0])
        _store_heads(o_ref, p, out[:HEAD_DIM] / jnp.sum(ls, axis=0, keepdims=True))


def _attn_kernel(q_ref, misc_ref, wi_ref, qi_ref, o_ref, vt_ref, kmax_ref, score_ref, bias_ref, s_ref, p_ref,
                 *, seq, topk):
    i = pl.program_id(1)

    @pl.when(i == 0)
    def _():
        for g in range(N_KV):
            for c in range(seq // ATT_CH):
                rows = slice(c * ATT_CH, (c + 1) * ATT_CH)
                vg = misc_ref[rows, MISC_V + g * HEAD_DIM:MISC_V + (g + 1) * HEAD_DIM]
                vt_ref[g, 0:HEAD_DIM, rows] = vg.astype(F32).T.astype(BF16)
            vt_ref[g, HEAD_DIM:, :] = jnp.ones((ATT_ONES_ROWS, seq), BF16)
            kg = misc_ref[:, MISC_K + g * HEAD_DIM:MISC_K + (g + 1) * HEAD_DIM].astype(F32)
            k_norm2 = jnp.max(jnp.sum(kg * kg, axis=1, keepdims=True))
            kmax_ref[g] = jnp.full((8, LANES), jnp.sqrt(k_norm2), F32)

    n_span = (i * ATT_T + ATT_T + ATT_SPAN - 1) // ATT_SPAN
    shifts, fast = _softmax_shifts(q_ref, kmax_ref)
    for ns in range(1, seq // ATT_SPAN + 1):
        @pl.when(n_span == ns)
        def _(ns=ns):
            _attn_block(ns * ATT_SPAN, i, shifts, fast, q_ref, misc_ref, wi_ref, qi_ref, o_ref, vt_ref, score_ref,
                        bias_ref, p_ref, topk)

    @pl.when(jnp.logical_not(fast))
    def _():
        _attn_exact_max(n_span * (ATT_SPAN // ATT_CH), q_ref, misc_ref, o_ref, vt_ref, bias_ref, s_ref, p_ref, seq)


def _attention(proj, bsz, seq):
    n = proj.shape[0]
    nblk = seq // ATT_T
    topk = min(TOPK_MAX, seq // 4)
    kern = functools.partial(_attn_kernel, seq=seq, topk=topk)
    return pl.pallas_call(
        kern,
        out_shape=jax.ShapeDtypeStruct((n, Q_WIDTH), BF16),
        grid=(bsz, nblk),
        in_specs=[
            pl.BlockSpec((ATT_T, Q_WIDTH), lambda b, i: (b * nblk + i, COL_Q // Q_WIDTH)),
            pl.BlockSpec((seq, PROJ_TN), lambda b, i: (b, COL_MISC // PROJ_TN)),
            pl.BlockSpec((ATT_T, LANES), lambda b, i: (b * nblk + i, (COL_MISC + MISC_WI) // LANES)),
            pl.BlockSpec((ATT_T, IDX_HEADS * IDX_DIM), lambda b, i: (b * nblk + i, COL_QI // (IDX_HEADS * IDX_DIM))),
        ],
        out_specs=pl.BlockSpec((ATT_T, Q_WIDTH), lambda b, i: (b * nblk + i, 0)),
        scratch_shapes=[
            pltpu.VMEM((N_KV, HEAD_DIM + ATT_ONES_ROWS, seq), BF16),
            pltpu.VMEM((N_KV, 8, LANES), F32),
            pltpu.VMEM((seq, ATT_T), F32),
            pltpu.VMEM((seq, ATT_T), F32),
            pltpu.VMEM((seq, 2 * ATT_T), F32),
            pltpu.VMEM((N_HEADS // 2, seq, 2 * ATT_T), BF16),
        ],
        compiler_params=pltpu.CompilerParams(
            dimension_semantics=("arbitrary", "arbitrary"), vmem_limit_bytes=VMEM_LIMIT),
        name="sparse_attn",
    )(proj, proj, proj, proj)


def _merge_kernel(ga_ref, gb_ref, u_ref, v_ref, yb_ref, x_ref, g1_ref, sc2_ref, sh2_ref, n2g_ref,
                  wsp_ref, bsp_ref, wpa_ref, wpb_ref, wout_ref, x1_ref, h2_ref, ya_ref, *, tm):
    row = lax.broadcasted_iota(jnp.int32, (CHUNK, CHUNK), 0)
    col = lax.broadcasted_iota(jnp.int32, (CHUNK, CHUNK), 1)
    causal = col <= row
    bsp = bsp_ref[...]
    for g in range(A_GROUPS):
        wm = jnp.where(causal, wsp_ref[g], jnp.zeros((), BF16))
        gcols = slice(g * A_GROUP_DIM, (g + 1) * A_GROUP_DIM)
        for ci in range(tm // CHUNK):
            rows = slice(ci * CHUNK, (ci + 1) * CHUNK)
            mixed = _dot(wm, v_ref[rows, gcols]) + bsp[:, g:g + 1]
            ya_ref[rows, gcols] = (u_ref[rows, gcols].astype(F32) * mixed).astype(BF16)

    a = _dot(ya_ref[...], wpa_ref[...])
    b = _dot(yb_ref[...], wpb_ref[...])
    merged = ga_ref[...].astype(F32) * a + gb_ref[...].astype(F32) * b
    o = _dot(merged.astype(BF16), wout_ref[...])
    x1 = x_ref[...] + g1_ref[...] * o
    x1_ref[...] = x1
    r = lax.rsqrt(jnp.mean(x1 * x1, axis=-1, keepdims=True) + EPS)
    h2 = (x1 * r * n2g_ref[...]) * (1.0 + sc2_ref[...]) + sh2_ref[...]
    h2_ref[...] = h2.astype(BF16)


def _merge(proj, yb, x2, g1, sc2, sh2, norm2_g, wsp, bsp_t, wpa, wpb, wout, seq):
    n = x2.shape[0]
    tm = 256
    tiles_per_batch = seq // tm
    mod_spec = pl.BlockSpec((None, 1, D_MODEL), lambda i: (i // tiles_per_batch, 0, 0))
    const2 = lambda i: (0, 0)
    kern = functools.partial(_merge_kernel, tm=tm)
    return pl.pallas_call(
        kern,
        out_shape=(jax.ShapeDtypeStruct((n, D_MODEL), F32), jax.ShapeDtypeStruct((n, D_MODEL), BF16)),
        grid=(n // tm,),
        in_specs=[
            pl.BlockSpec((tm, D_MODEL), lambda i: (i, COL_GA // D_MODEL)),
            pl.BlockSpec((tm, D_MODEL), lambda i: (i, COL_GB // D_MODEL)),
            pl.BlockSpec((tm, A_WIDTH), lambda i: (i, COL_U // A_WIDTH)),
            pl.BlockSpec((tm, A_WIDTH), lambda i: (i, COL_VA // A_WIDTH)),
            pl.BlockSpec((tm, Q_WIDTH), lambda i: (i, 0)),
            pl.BlockSpec((tm, D_MODEL), lambda i: (i, 0)),
            mod_spec, mod_spec, mod_spec,
            pl.BlockSpec((1, D_MODEL), const2),
            pl.BlockSpec((A_GROUPS, CHUNK, CHUNK), lambda i: (0, 0, 0)),
            pl.BlockSpec((CHUNK, A_GROUPS), const2),
            pl.BlockSpec((A_WIDTH, D_MODEL), const2, pipeline_mode=pl.Buffered(1)),
            pl.BlockSpec((Q_WIDTH, D_MODEL), const2, pipeline_mode=pl.Buffered(1)),
            pl.BlockSpec((D_MODEL, D_MODEL), const2, pipeline_mode=pl.Buffered(1)),
        ],
        out_specs=(pl.BlockSpec((tm, D_MODEL), lambda i: (i, 0)),
                   pl.BlockSpec((tm, D_MODEL), lambda i: (i, 0))),
        scratch_shapes=[pltpu.VMEM((tm, A_WIDTH), BF16)],
        compiler_params=pltpu.CompilerParams(
            dimension_semantics=("parallel",), vmem_limit_bytes=VMEM_LIMIT),
        name="merge_out",
    )(proj, proj, proj, proj, yb, x2, g1, sc2, sh2, norm2_g.reshape(1, D_MODEL), wsp, bsp_t, wpa, wpb, wout)


FFN_TM = 512
FFN_TF = 512
FFN_HALO = 8


def _ffn_kernel(h_ref, wa_ref, wb_ref, cwa_ref, cwb_ref, cba_ref, cbb_ref, wd_ref, x1_ref, g2_ref,
                o_ref, up_ref, tail_ref, acc_ref, *, tiles_per_batch):
    i = pl.program_id(0)
    f = pl.program_id(1)
    first = (i % tiles_per_batch) == 0

    @pl.when(first)
    def _():
        up_ref[:FFN_HALO] = jnp.zeros((FFN_HALO, 2 * FFN_TF), F32)

    @pl.when(jnp.logical_not(first))
    def _():
        up_ref[:FFN_HALO] = tail_ref[f]

    @pl.when(f == 0)
    def _():
        acc_ref[...] = jnp.zeros_like(acc_ref)

    h = h_ref[...]
    up_ref[FFN_HALO:, :FFN_TF] = _dot(h, wa_ref[...])
    up_ref[FFN_HALO:, FFN_TF:] = _dot(h, wb_ref[...])

    def conv(cols, cw_ref, cb_ref):
        y = cb_ref[...] + cw_ref[0:1] * up_ref[FFN_HALO - 2:FFN_HALO - 2 + FFN_TM, cols]
        y = y + cw_ref[1:2] * up_ref[FFN_HALO - 1:FFN_HALO - 1 + FFN_TM, cols]
        return y + cw_ref[2:3] * up_ref[FFN_HALO:FFN_HALO + FFN_TM, cols]

    a = conv(slice(0, FFN_TF), cwa_ref, cba_ref)
    b = conv(slice(FFN_TF, 2 * FFN_TF), cwb_ref, cbb_ref)
    gated = (jax.nn.silu(a) * b).astype(BF16)
    acc_ref[...] += _dot(gated, wd_ref[...])
    tail_ref[f] = up_ref[FFN_TM:FFN_TM + FFN_HALO]

    @pl.when(f == pl.num_programs(1) - 1)
    def _():
        o_ref[...] = x1_ref[...] + g2_ref[...] * acc_ref[...]


def _ffn(h2, x1, g2, w_up, conv_w, conv_b, w_down, seq):
    n = h2.shape[0]
    tm, tf = FFN_TM, FFN_TF
    nf = D_FF // tf
    tiles_per_batch = seq // tm
    kern = functools.partial(_ffn_kernel, tiles_per_batch=tiles_per_batch)
    return pl.pallas_call(
        kern,
        out_shape=jax.ShapeDtypeStruct((n, D_MODEL), F32),
        grid=(n // tm, nf),
        in_specs=[
            pl.BlockSpec((tm, D_MODEL), lambda i, f: (i, 0)),
            pl.BlockSpec((D_MODEL, tf), lambda i, f: (0, f)),
            pl.BlockSpec((D_MODEL, tf), lambda i, f: (0, f + nf)),
            pl.BlockSpec((CONV_W, tf), lambda i, f: (0, f)),
            pl.BlockSpec((CONV_W, tf), lambda i, f: (0, f + nf)),
            pl.BlockSpec((1, tf), lambda i, f: (0, f)),
            pl.BlockSpec((1, tf), lambda i, f: (0, f + nf)),
            pl.BlockSpec((tf, D_MODEL), lambda i, f: (f, 0)),
            pl.BlockSpec((tm, D_MODEL), lambda i, f: (i, 0)),
            pl.BlockSpec((None, 1, D_MODEL), lambda i, f: (i // tiles_per_batch, 0, 0)),
        ],
        out_specs=pl.BlockSpec((tm, D_MODEL), lambda i, f: (i, 0)),
        scratch_shapes=[
            pltpu.VMEM((FFN_HALO + tm, 2 * tf), F32),
            pltpu.VMEM((nf, FFN_HALO, 2 * tf), F32),
            pltpu.VMEM((tm, D_MODEL), F32),
        ],
        compiler_params=pltpu.CompilerParams(
            dimension_semantics=("arbitrary", "arbitrary"), vmem_limit_bytes=VMEM_LIMIT),
        name="conv_ffn",
    )(h2, w_up, w_up, conv_w, conv_w, conv_b.reshape(1, 2 * D_FF), conv_b.reshape(1, 2 * D_FF), w_down, x1, g2)


def _fused_in_weight(w_in):
    sizes = [A_WIDTH, A_WIDTH, Q_WIDTH, KV_WIDTH, KV_WIDTH, IDX_HEADS * IDX_DIM, IDX_DIM, IDX_HEADS, D_MODEL, D_MODEL]
    offs = [0]
    for s in sizes:
        offs.append(offs[-1] + s)
    w16 = w_in.astype(BF16)
    u, va, q, k, vb, qi, ki, wi, ga, gb = [w16[:, offs[t]:offs[t + 1]] for t in range(len(sizes))]
    z = lambda ncol: jnp.zeros((D_MODEL, ncol), BF16)
    misc = jnp.concatenate([k, vb, ki, z(IDX_DIM), z(IDX_DIM), ki, wi, z(LANES - IDX_HEADS), z(LANES)], axis=1)
    w_cat = jnp.concatenate([ga, gb, u, va, q, misc, qi], axis=1)
    assert w_cat.shape[1] == PROJ_COLS and misc.shape[1] == PROJ_TN
    return w_cat


def kernel(x, c, w_ada, b_ada, norm1_g, w_in, v_norm_g, w_spatial, b_spatial, q_norm_g, k_norm_g, w_proj_a,
           w_proj_b, w_out, norm2_g, w_up, conv_w, conv_b, w_down):
    bsz, seq, _ = x.shape
    n = bsz * seq
    x2 = x.reshape(n, D_MODEL)

    mod = _ada(c, w_ada, b_ada)
    sh1, sc1, g1, sh2, sc2, g2 = [mod[:, t * D_MODEL:(t + 1) * D_MODEL].reshape(bsz, 1, D_MODEL) for t in range(6)]

    proj = _proj(x2, sc1, sh1, norm1_g, _fused_in_weight(w_in), v_norm_g, q_norm_g, k_norm_g, seq)
    yb = _attention(proj, bsz, seq)
    x1, h2 = _merge(proj, yb, x2, g1, sc2, sh2, norm2_g, w_spatial.astype(BF16), b_spatial.T,
                    w_proj_a.astype(BF16), w_proj_b.astype(BF16), w_out.astype(BF16), seq)
    out = _ffn(h2, x1, g2, w_up.astype(BF16), conv_w, conv_b, w_down.astype(BF16), seq)
    return out.reshape(bsz, seq, D_MODEL)
```

```python
import functools
import math

import jax
import jax.numpy as jnp
from jax import lax
from jax.experimental import pallas as pl
from jax.experimental.pallas import tpu as pltpu

F32 = jnp.float32
BF16 = jnp.bfloat16

D_MODEL = 2048
CHUNK = 128
A_GROUPS = 8
A_GROUP_DIM = 128
A_WIDTH = A_GROUPS * A_GROUP_DIM
N_HEADS = 8
HEAD_DIM = 128
N_KV = 2
Q_WIDTH = N_HEADS * HEAD_DIM
KV_WIDTH = N_KV * HEAD_DIM
IDX_HEADS = 16
IDX_DIM = 64
TOPK_MAX = 256
D_FF = 5632
CONV_W = 3
EPS = 1e-6

LANES = 128
VMEM_LIMIT = 56 * 1024 * 1024

PROJ_TN = 1024
COL_GA, COL_GB, COL_U, COL_VA, COL_Q, COL_MISC, COL_QI = 0, 2048, 4096, 5120, 6144, 7168, 8192
PROJ_COLS = 9216
MISC_K, MISC_V, MISC_KI_EVEN, MISC_KI_ODD, MISC_WI = 0, 256, 512, 640, 768

ATT_T = 128
ATT_CH = 256
ATT_SPAN = 256
ATT_ONES_ROWS = 16
ATT_SAFE_SHIFT = 40.0
ATT_BOUND_SLACK = 1.02
NEG_BIG = -1e30
INT_MIN = -(2 ** 31)
KEY_NEG_INF = -2139095041
MOST_NEG_F32 = -3.4028234663852886e38


def _dot(a, b):
    return jnp.dot(a, b, preferred_element_type=F32)


def _dot_nt(a, b):
    return lax.dot_general(a, b, (((1,), (1,)), ((), ())), preferred_element_type=F32)


def _ada_kernel(c_ref, w_ref, b_ref, o_ref):
    cs = jax.nn.silu(c_ref[...]).astype(BF16)
    o_ref[...] = _dot(cs, w_ref[...].astype(BF16)) + b_ref[...]


def _ada(c, w_ada, b_ada):
    bsz = c.shape[0]
    n = w_ada.shape[1]
    tn = 1024
    return pl.pallas_call(
        _ada_kernel,
        out_shape=jax.ShapeDtypeStruct((bsz, n), F32),
        grid=(n // tn,),
        in_specs=[
            pl.BlockSpec((bsz, D_MODEL), lambda j: (0, 0)),
            pl.BlockSpec((D_MODEL, tn), lambda j: (0, j)),
            pl.BlockSpec((1, tn), lambda j: (0, j)),
        ],
        out_specs=pl.BlockSpec((bsz, tn), lambda j: (0, j)),
        compiler_params=pltpu.CompilerParams(
            dimension_semantics=("arbitrary",), vmem_limit_bytes=VMEM_LIMIT),
        name="ada_mod",
    )(c, w_ada, b_ada.reshape(1, n))


def _sigmoid(x):
    return 0.5 * jnp.tanh(0.5 * x) + 0.5


def _head_rms(x, g, scale):
    r = lax.rsqrt(jnp.mean(x * x, axis=-1, keepdims=True) + EPS)
    y = x * r * g
    return y if scale is None else y * scale


def _proj_kernel(x_ref, sc_ref, sh_ref, g_ref, w_ref, vg_ref, qg_ref, kg_ref, o_ref, h_ref):
    j = pl.program_id(1)

    def acc():
        return _dot(h_ref[...], w_ref[...])

    @pl.when(j == 0)
    def _():
        x = x_ref[...]
        r = lax.rsqrt(jnp.mean(x * x, axis=-1, keepdims=True) + EPS)
        h = ((x * r * g_ref[...]) * (1.0 + sc_ref[...]) + sh_ref[...]).astype(BF16)
        h_ref[...] = h
        o_ref[...] = _sigmoid(_dot(h, w_ref[...])).astype(BF16)

    @pl.when((j > 0) & (j < COL_U // PROJ_TN))
    def _():
        o_ref[...] = _sigmoid(acc()).astype(BF16)

    @pl.when(j == COL_U // PROJ_TN)
    def _():
        o_ref[...] = jax.nn.gelu(acc()).astype(BF16)

    @pl.when(j == COL_VA // PROJ_TN)
    def _():
        v = jax.nn.gelu(acc())
        r = lax.rsqrt(jnp.mean(v * v, axis=-1, keepdims=True) + EPS)
        o_ref[...] = (v * r * vg_ref[...]).astype(BF16)

    @pl.when(j == COL_Q // PROJ_TN)
    def _():
        qscale = (HEAD_DIM ** -0.5) * math.log2(math.e)
        a = acc()
        for h in range(N_HEADS):
            sl = slice(h * HEAD_DIM, (h + 1) * HEAD_DIM)
            o_ref[:, sl] = _head_rms(a[:, sl], qg_ref[...], qscale).astype(BF16)

    @pl.when(j == COL_MISC // PROJ_TN)
    def _():
        a = acc()
        for h in range(N_KV):
            sl = slice(MISC_K + h * HEAD_DIM, MISC_K + (h + 1) * HEAD_DIM)
            o_ref[:, sl] = _head_rms(a[:, sl], kg_ref[...], None).astype(BF16)
        o_ref[:, MISC_V:] = a[:, MISC_V:].astype(BF16)

    @pl.when(j == COL_QI // PROJ_TN)
    def _():
        o_ref[...] = acc().astype(BF16)


def _proj(x2, sc1, sh1, norm1_g, w_cat, v_norm_g, q_norm_g, k_norm_g, seq):
    n = x2.shape[0]
    tm = 1024
    tiles_per_batch = seq // tm
    mod_spec = pl.BlockSpec((None, 1, D_MODEL), lambda i, j: (i // tiles_per_batch, 0, 0))
    return pl.pallas_call(
        _proj_kernel,
        out_shape=jax.ShapeDtypeStruct((n, PROJ_COLS), BF16),
        grid=(n // tm, PROJ_COLS // PROJ_TN),
        in_specs=[
            pl.BlockSpec((tm, D_MODEL), lambda i, j: (i, 0)),
            mod_spec, mod_spec,
            pl.BlockSpec((1, D_MODEL), lambda i, j: (0, 0)),
            pl.BlockSpec((D_MODEL, PROJ_TN), lambda i, j: (0, j)),
            pl.BlockSpec((1, A_WIDTH), lambda i, j: (0, 0)),
            pl.BlockSpec((1, HEAD_DIM), lambda i, j: (0, 0)),
            pl.BlockSpec((1, HEAD_DIM), lambda i, j: (0, 0)),
        ],
        out_specs=pl.BlockSpec((tm, PROJ_TN), lambda i, j: (i, j)),
        scratch_shapes=[pltpu.VMEM((tm, D_MODEL), BF16)],
        compiler_params=pltpu.CompilerParams(
            dimension_semantics=("parallel", "arbitrary"), vmem_limit_bytes=VMEM_LIMIT),
        name="proj_in",
    )(x2, sc1, sh1, norm1_g.reshape(1, D_MODEL), w_cat, v_norm_g.reshape(1, A_WIDTH),
      q_norm_g.reshape(1, HEAD_DIM), k_norm_g.reshape(1, HEAD_DIM))


def _sublane_tree(x, op):
    r, c = x.shape
    x = x.reshape(r // 64, 8, 8, c)
    y = x[0]
    for t in range(1, r // 64):
        y = op(y, x[t])
    z = op(op(y[0], y[1]), op(y[2], y[3]))
    return op(z, op(op(y[4], y[5]), op(y[6], y[7])))


def _attn_block(s_eff, i, shifts, fast, q_ref, misc_ref, wi_ref, qi_ref, o_ref, vt_ref, score_ref, bias_ref, p_ref,
                topk):
    nch = s_eff // ATT_CH

    idx_scale = IDX_DIM ** -0.5 * IDX_HEADS ** -0.5
    w_t = wi_ref[...].astype(F32).T * idx_scale
    rhs = [jnp.concatenate([qi_ref[:, 256 * r:256 * r + 128], qi_ref[:, 256 * r + 128:256 * r + 256]], axis=0)
           for r in range(IDX_HEADS // 4)]
    q_pos = i * ATT_T + lax.broadcasted_iota(jnp.int32, (ATT_CH, ATT_T), 1)
    row_iota = lax.broadcasted_iota(jnp.int32, (ATT_CH, ATT_T), 0)

    for c in range(nch):
        rows = slice(c * ATT_CH, (c + 1) * ATT_CH)
        k_even = misc_ref[rows, MISC_KI_EVEN:MISC_KI_EVEN + LANES]
        k_odd = misc_ref[rows, MISC_KI_ODD:MISC_KI_ODD + LANES]
        acc = [None, None]
        for r in range(IDX_HEADS // 4):
            l_even = _dot_nt(k_even, rhs[r])
            l_odd = _dot_nt(k_odd, rhs[r])
            terms = (w_t[4 * r:4 * r + 1] * jnp.maximum(l_even[:, :ATT_T], 0.0),
                     w_t[4 * r + 1:4 * r + 2] * jnp.maximum(l_odd[:, :ATT_T], 0.0),
                     w_t[4 * r + 2:4 * r + 3] * jnp.maximum(l_even[:, ATT_T:], 0.0),
                     w_t[4 * r + 3:4 * r + 4] * jnp.maximum(l_odd[:, ATT_T:], 0.0))
            for t, term in enumerate(terms):
                acc[t % 2] = term if acc[t % 2] is None else acc[t % 2] + term
        score = acc[0] + acc[1]
        if (c + 1) * ATT_CH > s_eff - ATT_SPAN:
            score = jnp.where(c * ATT_CH + row_iota <= q_pos, score, -jnp.inf)
        score_ref[rows, :] = score

    def count_ge(thr_f):
        tot = None
        for c in range(nch):
            ones = jnp.where(score_ref[c * ATT_CH:(c + 1) * ATT_CH, :] >= thr_f, 1.0, 0.0)
            part = _sublane_tree(ones, jnp.add)
            tot = part if tot is None else tot + part
        return jnp.sum(tot, axis=0, keepdims=True)

    def code_to_float(code):
        key = code ^ INT_MIN
        return lax.bitcast_convert_type(key ^ ((key >> 31) & 0x7FFFFFFF), F32)

    def bs_step(cand, cand_f, code, n_ge):
        cnt = count_ge(cand_f)
        ok = cnt >= float(topk)
        return jnp.where(ok, cand, code), jnp.where(ok, cnt, n_ge)

    top = jnp.full((1, ATT_T), INT_MIN, jnp.int32)
    code, n_ge = bs_step(top, code_to_float(top), jnp.zeros((1, ATT_T), jnp.int32),
                         jnp.full((1, ATT_T), float(s_eff), F32))
    to_bits = jnp.where(code < 0, INT_MIN, -1)

    def bs_body(it, carry):
        code, n_ge = carry
        cand = code | lax.shift_left(jnp.int32(1), 31 - it)
        return bs_step(cand, lax.bitcast_convert_type(cand ^ to_bits, F32), code, n_ge)

    code, n_ge = lax.fori_loop(1, 32, bs_body, (code, n_ge))
    thr_f = code_to_float(jnp.maximum(code ^ INT_MIN, KEY_NEG_INF + 1) ^ INT_MIN)
    excess = jnp.max(jnp.where(thr_f > MOST_NEG_F32, n_ge - float(topk), 0.0))

    for c in range(nch):
        rows = slice(c * ATT_CH, (c + 1) * ATT_CH)
        bias_ref[rows, :] = jnp.where(score_ref[rows, :] >= thr_f, 0.0, NEG_BIG)

    @pl.when(excess > 0.0)
    def _():
        n_tie = None
        for c in range(nch):
            ones = jnp.where(score_ref[c * ATT_CH:(c + 1) * ATT_CH, :] == thr_f, 1.0, 0.0)
            part = _sublane_tree(ones, jnp.add)
            n_tie = part if n_tie is None else n_tie + part
        n_tie = jnp.sum(n_tie, axis=0, keepdims=True)
        need = float(topk) - (n_ge - n_tie)
        lower = (lax.broadcasted_iota(jnp.int32, (ATT_CH, ATT_CH), 1)
                 <= lax.broadcasted_iota(jnp.int32, (ATT_CH, ATT_CH), 0))
        tril = jnp.where(lower, 1.0, 0.0).astype(BF16)
        run = jnp.zeros((1, ATT_T), F32)
        for c in range(nch):
            rows = slice(c * ATT_CH, (c + 1) * ATT_CH)
            sc = score_ref[rows, :]
            tie = jnp.where(sc == thr_f, 1.0, 0.0)
            rank = _dot(tril, tie.astype(BF16)) + run
            keep = (sc > thr_f) | ((sc == thr_f) & (rank <= need))
            bias_ref[rows, :] = jnp.where(keep, 0.0, NEG_BIG)
            run = run + jnp.sum(_sublane_tree(tie, jnp.add), axis=0, keepdims=True)

    @pl.when(fast)
    def _():
        pairs = [_head_pair(q_ref, p) for p in range(N_HEADS // 2)]
        for c in range(nch):
            rows = slice(c * ATT_CH, (c + 1) * ATT_CH)
            bias = bias_ref[rows, :]
            bias2 = jnp.concatenate([bias, bias], axis=1)
            for p, (g, q_pair) in enumerate(pairs):
                s = _dot_nt(misc_ref[rows, MISC_K + g * HEAD_DIM:MISC_K + (g + 1) * HEAD_DIM], q_pair)
                p_ref[p, rows, :] = jnp.exp2(s + bias2 - shifts[p]).astype(BF16)
        outs = [_dot(vt_ref[g, :, 0:s_eff], p_ref[p, 0:s_eff, :]) for p, (g, _) in enumerate(pairs)]
        for p, out in enumerate(outs):
            _store_heads(o_ref, p, out[:HEAD_DIM] / out[HEAD_DIM:HEAD_DIM + 1])


def _head_pair(q_ref, p):
    g = (2 * p) // (N_HEADS // N_KV)
    return g, jnp.concatenate([q_ref[:, 256 * p:256 * p + 128], q_ref[:, 256 * p + 128:256 * p + 256]], axis=0)


def _store_heads(o_ref, p, out):
    o_ref[:, 256 * p:256 * p + 128] = out[:, :ATT_T].T.astype(BF16)
    o_ref[:, 256 * p + 128:256 * p + 256] = out[:, ATT_T:].T.astype(BF16)


def _softmax_shifts(q_ref, kmax_ref):
    ones_rows = jnp.ones((8, HEAD_DIM), BF16)
    pairs = [_head_pair(q_ref, p) for p in range(N_HEADS // 2)]
    squares = [jnp.square(q_pair.astype(F32)).astype(BF16) for _, q_pair in pairs]
    norms2 = [_dot_nt(ones_rows, sq)[0:1] for sq in squares]
    shifts = []
    for (g, _), qn2 in zip(pairs, norms2):
        kmax = kmax_ref[g, 0:1, :]
        shifts.append(jnp.sqrt(qn2) * (jnp.concatenate([kmax, kmax], axis=1) * ATT_BOUND_SLACK))
    return shifts, jnp.max(jnp.concatenate(shifts, axis=0)) <= ATT_SAFE_SHIFT


def _attn_exact_max(nch, q_ref, misc_ref, o_ref, vt_ref, bias_ref, s_ref, p_ref, seq):
    def zero_tail(c, carry):
        p_ref[0, pl.ds(pl.multiple_of(c * ATT_CH, ATT_CH), ATT_CH), :] = jnp.zeros((ATT_CH, 2 * ATT_T), BF16)
        return carry

    lax.fori_loop(nch, seq // ATT_CH, zero_tail, 0)
    for p in range(N_HEADS // 2):
        g, q_pair = _head_pair(q_ref, p)

        def logits(c, mx, g=g, q_pair=q_pair):
            rows = pl.ds(pl.multiple_of(c * ATT_CH, ATT_CH), ATT_CH)
            bias = bias_ref[rows, :]
            s = _dot_nt(misc_ref[rows, MISC_K + g * HEAD_DIM:MISC_K + (g + 1) * HEAD_DIM], q_pair)
            s = s + jnp.concatenate([bias, bias], axis=1)
            s_ref[rows, :] = s
            return jnp.maximum(mx, _sublane_tree(s, jnp.maximum))

        mx = lax.fori_loop(0, nch, logits, jnp.full((8, 2 * ATT_T), -jnp.inf, F32))
        m = jnp.max(mx, axis=0, keepdims=True)

        def probs(c, ls, m=m):
            rows = pl.ds(pl.multiple_of(c * ATT_CH, ATT_CH), ATT_CH)
            pm = jnp.exp2(s_ref[rows, :] - m)
            p_ref[0, rows, :] = pm.astype(BF16)
            return ls + _sublane_tree(pm, jnp.add)

        ls = lax.fori_loop(0, nch, probs, jnp.zeros((8, 2 * ATT_T), F32))
        out = _dot(vt_ref[g], p_ref[0])
        _store_heads(o_ref, p, out[:HEAD_DIM] / jnp.sum(ls, axis=0, keepdims=True))


def _attn_kernel(q_ref, misc_ref, wi_ref, qi_ref, o_ref, vt_ref, kmax_ref, score_ref, bias_ref, s_ref, p_ref,
                 *, seq, topk):
    i = pl.program_id(1)

    @pl.when(i == 0)
    def _():
        for g in range(N_KV):
            for c in range(seq // ATT_CH):
                rows = slice(c * ATT_CH, (c + 1) * ATT_CH)
                vg = misc_ref[rows, MISC_V + g * HEAD_DIM:MISC_V + (g + 1) * HEAD_DIM]
                vt_ref[g, 0:HEAD_DIM, rows] = vg.astype(F32).T.astype(BF16)
            vt_ref[g, HEAD_DIM:, :] = jnp.ones((ATT_ONES_ROWS, seq), BF16)
            kg = misc_ref[:, MISC_K + g * HEAD_DIM:MISC_K + (g + 1) * HEAD_DIM].astype(F32)
            k_norm2 = jnp.max(jnp.sum(kg * kg, axis=1, keepdims=True))
            kmax_ref[g] = jnp.full((8, LANES), jnp.sqrt(k_norm2), F32)

    n_span = (i * ATT_T + ATT_T + ATT_SPAN - 1) // ATT_SPAN
    shifts, fast = _softmax_shifts(q_ref, kmax_ref)
    for ns in range(1, seq // ATT_SPAN + 1):
        @pl.when(n_span == ns)
        def _(ns=ns):
            _attn_block(ns * ATT_SPAN, i, shifts, fast, q_ref, misc_ref, wi_ref, qi_ref, o_ref, vt_ref, score_ref,
                        bias_ref, p_ref, topk)

    @pl.when(jnp.logical_not(fast))
    def _():
        _attn_exact_max(n_span * (ATT_SPAN // ATT_CH), q_ref, misc_ref, o_ref, vt_ref, bias_ref, s_ref, p_ref, seq)


def _attention(proj, bsz, seq):
    n = proj.shape[0]
    nblk = seq // ATT_T
    topk = min(TOPK_MAX, seq // 4)
    kern = functools.partial(_attn_kernel, seq=seq, topk=topk)
    return pl.pallas_call(
        kern,
        out_shape=jax.ShapeDtypeStruct((n, Q_WIDTH), BF16),
        grid=(bsz, nblk),
        in_specs=[
            pl.BlockSpec((ATT_T, Q_WIDTH), lambda b, i: (b * nblk + i, COL_Q // Q_WIDTH)),
            pl.BlockSpec((seq, PROJ_TN), lambda b, i: (b, COL_MISC // PROJ_TN)),
            pl.BlockSpec((ATT_T, LANES), lambda b, i: (b * nblk + i, (COL_MISC + MISC_WI) // LANES)),
            pl.BlockSpec((ATT_T, IDX_HEADS * IDX_DIM), lambda b, i: (b * nblk + i, COL_QI // (IDX_HEADS * IDX_DIM))),
        ],
        out_specs=pl.BlockSpec((ATT_T, Q_WIDTH), lambda b, i: (b * nblk + i, 0)),
        scratch_shapes=[
            pltpu.VMEM((N_KV, HEAD_DIM + ATT_ONES_ROWS, seq), BF16),
            pltpu.VMEM((N_KV, 8, LANES), F32),
            pltpu.VMEM((seq, ATT_T), F32),
            pltpu.VMEM((seq, ATT_T), F32),
            pltpu.VMEM((seq, 2 * ATT_T), F32),
            pltpu.VMEM((N_HEADS // 2, seq, 2 * ATT_T), BF16),
        ],
        compiler_params=pltpu.CompilerParams(
            dimension_semantics=("arbitrary", "arbitrary"), vmem_limit_bytes=VMEM_LIMIT),
        name="sparse_attn",
    )(proj, proj, proj, proj)


def _merge_kernel(ga_ref, gb_ref, u_ref, v_ref, yb_ref, x_ref, g1_ref, sc2_ref, sh2_ref, n2g_ref,
                  wsp_ref, bsp_ref, wpa_ref, wpb_ref, wout_ref, x1_ref, h2_ref, ya_ref, *, tm):
    row = lax.broadcasted_iota(jnp.int32, (CHUNK, CHUNK), 0)
    col = lax.broadcasted_iota(jnp.int32, (CHUNK, CHUNK), 1)
    causal = col <= row
    bsp = bsp_ref[...]
    for g in range(A_GROUPS):
        wm = jnp.where(causal, wsp_ref[g], jnp.zeros((), BF16))
        gcols = slice(g * A_GROUP_DIM, (g + 1) * A_GROUP_DIM)
        for ci in range(tm // CHUNK):
            rows = slice(ci * CHUNK, (ci + 1) * CHUNK)
            mixed = _dot(wm, v_ref[rows, gcols]) + bsp[:, g:g + 1]
            ya_ref[rows, gcols] = (u_ref[rows, gcols].astype(F32) * mixed).astype(BF16)

    a = _dot(ya_ref[...], wpa_ref[...])
    b = _dot(yb_ref[...], wpb_ref[...])
    merged = ga_ref[...].astype(F32) * a + gb_ref[...].astype(F32) * b
    o = _dot(merged.astype(BF16), wout_ref[...])
    x1 = x_ref[...] + g1_ref[...] * o
    x1_ref[...] = x1
    r = lax.rsqrt(jnp.mean(x1 * x1, axis=-1, keepdims=True) + EPS)
    h2 = (x1 * r * n2g_ref[...]) * (1.0 + sc2_ref[...]) + sh2_ref[...]
    h2_ref[...] = h2.astype(BF16)


def _merge(proj, yb, x2, g1, sc2, sh2, norm2_g, wsp, bsp_t, wpa, wpb, wout, seq):
    n = x2.shape[0]
    tm = 256
    tiles_per_batch = seq // tm
    mod_spec = pl.BlockSpec((None, 1, D_MODEL), lambda i: (i // tiles_per_batch, 0, 0))
    const2 = lambda i: (0, 0)
    kern = functools.partial(_merge_kernel, tm=tm)
    return pl.pallas_call(
        kern,
        out_shape=(jax.ShapeDtypeStruct((n, D_MODEL), F32), jax.ShapeDtypeStruct((n, D_MODEL), BF16)),
        grid=(n // tm,),
        in_specs=[
            pl.BlockSpec((tm, D_MODEL), lambda i: (i, COL_GA // D_MODEL)),
            pl.BlockSpec((tm, D_MODEL), lambda i: (i, COL_GB // D_MODEL)),
            pl.BlockSpec((tm, A_WIDTH), lambda i: (i, COL_U // A_WIDTH)),
            pl.BlockSpec((tm, A_WIDTH), lambda i: (i, COL_VA // A_WIDTH)),
            pl.BlockSpec((tm, Q_WIDTH), lambda i: (i, 0)),
            pl.BlockSpec((tm, D_MODEL), lambda i: (i, 0)),
            mod_spec, mod_spec, mod_spec,
            pl.BlockSpec((1, D_MODEL), const2),
            pl.BlockSpec((A_GROUPS, CHUNK, CHUNK), lambda i: (0, 0, 0)),
            pl.BlockSpec((CHUNK, A_GROUPS), const2),
            pl.BlockSpec((A_WIDTH, D_MODEL), const2, pipeline_mode=pl.Buffered(1)),
            pl.BlockSpec((Q_WIDTH, D_MODEL), const2, pipeline_mode=pl.Buffered(1)),
            pl.BlockSpec((D_MODEL, D_MODEL), const2, pipeline_mode=pl.Buffered(1)),
        ],
        out_specs=(pl.BlockSpec((tm, D_MODEL), lambda i: (i, 0)),
                   pl.BlockSpec((tm, D_MODEL), lambda i: (i, 0))),
        scratch_shapes=[pltpu.VMEM((tm, A_WIDTH), BF16)],
        compiler_params=pltpu.CompilerParams(
            dimension_semantics=("parallel",), vmem_limit_bytes=VMEM_LIMIT),
        name="merge_out",
    )(proj, proj, proj, proj, yb, x2, g1, sc2, sh2, norm2_g.reshape(1, D_MODEL), wsp, bsp_t, wpa, wpb, wout)


FFN_TM = 512
FFN_TF = 512
FFN_HALO = 8


def _ffn_kernel(h_ref, wa_ref, wb_ref, cwa_ref, cwb_ref, cba_ref, cbb_ref, wd_ref, x1_ref, g2_ref,
                o_ref, up_ref, tail_ref, acc_ref, *, tiles_per_batch):
    i = pl.program_id(0)
    f = pl.program_id(1)
    first = (i % tiles_per_batch) == 0

    @pl.when(first)
    def _():
        up_ref[:FFN_HALO] = jnp.zeros((FFN_HALO, 2 * FFN_TF), F32)

    @pl.when(jnp.logical_not(first))
    def _():
        up_ref[:FFN_HALO] = tail_ref[f]

    @pl.when(f == 0)
    def _():
        acc_ref[...] = jnp.zeros_like(acc_ref)

    h = h_ref[...]
    up_ref[FFN_HALO:, :FFN_TF] = _dot(h, wa_ref[...])
    up_ref[FFN_HALO:, FFN_TF:] = _dot(h, wb_ref[...])

    def conv(cols, cw_ref, cb_ref):
        y = cb_ref[...] + cw_ref[0:1] * up_ref[FFN_HALO - 2:FFN_HALO - 2 + FFN_TM, cols]
        y = y + cw_ref[1:2] * up_ref[FFN_HALO - 1:FFN_HALO - 1 + FFN_TM, cols]
        return y + cw_ref[2:3] * up_ref[FFN_HALO:FFN_HALO + FFN_TM, cols]

    a = conv(slice(0, FFN_TF), cwa_ref, cba_ref)
    b = conv(slice(FFN_TF, 2 * FFN_TF), cwb_ref, cbb_ref)
    gated = (jax.nn.silu(a) * b).astype(BF16)
    acc_ref[...] += _dot(gated, wd_ref[...])
    tail_ref[f] = up_ref[FFN_TM:FFN_TM + FFN_HALO]

    @pl.when(f == pl.num_programs(1) - 1)
    def _():
        o_ref[...] = x1_ref[...] + g2_ref[...] * acc_ref[...]


def _ffn(h2, x1, g2, w_up, conv_w, conv_b, w_down, seq):
    n = h2.shape[0]
    tm, tf = FFN_TM, FFN_TF
    nf = D_FF // tf
    tiles_per_batch = seq // tm
    kern = functools.partial(_ffn_kernel, tiles_per_batch=tiles_per_batch)
    return pl.pallas_call(
        kern,
        out_shape=jax.ShapeDtypeStruct((n, D_MODEL), F32),
        grid=(n // tm, nf),
        in_specs=[
            pl.BlockSpec((tm, D_MODEL), lambda i, f: (i, 0)),
            pl.BlockSpec((D_MODEL, tf), lambda i, f: (0, f)),
            pl.BlockSpec((D_MODEL, tf), lambda i, f: (0, f + nf)),
            pl.BlockSpec((CONV_W, tf), lambda i, f: (0, f)),
            pl.BlockSpec((CONV_W, tf), lambda i, f: (0, f + nf)),
            pl.BlockSpec((1, tf), lambda i, f: (0, f)),
            pl.BlockSpec((1, tf), lambda i, f: (0, f + nf)),
            pl.BlockSpec((tf, D_MODEL), lambda i, f: (f, 0)),
            pl.BlockSpec((tm, D_MODEL), lambda i, f: (i, 0)),
            pl.BlockSpec((None, 1, D_MODEL), lambda i, f: (i // tiles_per_batch, 0, 0)),
        ],
        out_specs=pl.BlockSpec((tm, D_MODEL), lambda i, f: (i, 0)),
        scratch_shapes=[
            pltpu.VMEM((FFN_HALO + tm, 2 * tf), F32),
            pltpu.VMEM((nf, FFN_HALO, 2 * tf), F32),
            pltpu.VMEM((tm, D_MODEL), F32),
        ],
        compiler_params=pltpu.CompilerParams(
            dimension_semantics=("arbitrary", "arbitrary"), vmem_limit_bytes=VMEM_LIMIT),
        name="conv_ffn",
    )(h2, w_up, w_up, conv_w, conv_w, conv_b.reshape(1, 2 * D_FF), conv_b.reshape(1, 2 * D_FF), w_down, x1, g2)


def _fused_in_weight(w_in):
    sizes = [A_WIDTH, A_WIDTH, Q_WIDTH, KV_WIDTH, KV_WIDTH, IDX_HEADS * IDX_DIM, IDX_DIM, IDX_HEADS, D_MODEL, D_MODEL]
    offs = [0]
    for s in sizes:
        offs.append(offs[-1] + s)
    u0, va0, q0, k0, vb0, qi0, ki0, wi0, ga0, gb0, end = offs
    w16 = w_in.astype(BF16)
    moves = [((ga0, end), COL_GA), ((u0, k0), COL_U), ((k0, qi0), COL_MISC + MISC_K), ((qi0, ki0), COL_QI),
             ((ki0, wi0), COL_MISC + MISC_KI_EVEN), ((ki0, wi0), COL_MISC + MISC_KI_ODD + IDX_DIM),
             ((wi0, ga0), COL_MISC + MISC_WI)]
    w_cat = jnp.zeros((D_MODEL, PROJ_COLS), BF16)
    for (lo, hi), dst in moves:
        w_cat = lax.dynamic_update_slice(w_cat, w16[:, lo:hi], (0, dst))
    return w_cat


def kernel(x, c, w_ada, b_ada, norm1_g, w_in, v_norm_g, w_spatial, b_spatial, q_norm_g, k_norm_g, w_proj_a,
           w_proj_b, w_out, norm2_g, w_up, conv_w, conv_b, w_down):
    bsz, seq, _ = x.shape
    n = bsz * seq
    x2 = x.reshape(n, D_MODEL)

    mod = _ada(c, w_ada, b_ada)
    sh1, sc1, g1, sh2, sc2, g2 = [mod[:, t * D_MODEL:(t + 1) * D_MODEL].reshape(bsz, 1, D_MODEL) for t in range(6)]

    proj = _proj(x2, sc1, sh1, norm1_g, _fused_in_weight(w_in), v_norm_g, q_norm_g, k_norm_g, seq)
    yb = _attention(proj, bsz, seq)
    x1, h2 = _merge(proj, yb, x2, g1, sc2, sh2, norm2_g, w_spatial.astype(BF16), b_spatial.T,
                    w_proj_a.astype(BF16), w_proj_b.astype(BF16), w_out.astype(BF16), seq)
    out = _ffn(h2, x1, g2, w_up.astype(BF16), conv_w, conv_b, w_down.astype(BF16), seq)
    return out.reshape(bsz, seq, D_MODEL)
```

```python
import functools
import math

import jax
import jax.numpy as jnp
from jax import lax
from jax.experimental import pallas as pl
from jax.experimental.pallas import tpu as pltpu

F32 = jnp.float32
BF16 = jnp.bfloat16

D_MODEL = 2048
CHUNK = 128
A_GROUPS = 8
A_GROUP_DIM = 128
A_WIDTH = A_GROUPS * A_GROUP_DIM
N_HEADS = 8
HEAD_DIM = 128
N_KV = 2
Q_WIDTH = N_HEADS * HEAD_DIM
KV_WIDTH = N_KV * HEAD_DIM
IDX_HEADS = 16
IDX_DIM = 64
TOPK_MAX = 256
D_FF = 5632
CONV_W = 3
EPS = 1e-6

LANES = 128
VMEM_LIMIT = 56 * 1024 * 1024

PROJ_TN = 1024
COL_GA, COL_GB, COL_U, COL_VA, COL_Q, COL_MISC, COL_QI = 0, 2048, 4096, 5120, 6144, 7168, 8192
PROJ_COLS = 9216
MISC_K, MISC_V, MISC_KI_EVEN, MISC_KI_ODD, MISC_WI = 0, 256, 512, 640, 768

ATT_T = 256
ATT_CH = 256
ATT_SPAN = 256
ATT_ONES_ROWS = 16
ATT_SAFE_SHIFT = 40.0
ATT_BOUND_SLACK = 1.02
NEG_BIG = -1e30
INT_MIN = -(2 ** 31)
KEY_NEG_INF = -2139095041
MOST_NEG_F32 = -3.4028234663852886e38


def _dot(a, b):
    return jnp.dot(a, b, preferred_element_type=F32)


def _dot_nt(a, b):
    return lax.dot_general(a, b, (((1,), (1,)), ((), ())), preferred_element_type=F32)


def _ada_kernel(c_ref, w_ref, b_ref, o_ref):
    cs = jax.nn.silu(c_ref[...]).astype(BF16)
    o_ref[...] = _dot(cs, w_ref[...].astype(BF16)) + b_ref[...]


def _ada(c, w_ada, b_ada):
    bsz = c.shape[0]
    n = w_ada.shape[1]
    tn = 1024
    return pl.pallas_call(
        _ada_kernel,
        out_shape=jax.ShapeDtypeStruct((bsz, n), F32),
        grid=(n // tn,),
        in_specs=[
            pl.BlockSpec((bsz, D_MODEL), lambda j: (0, 0)),
            pl.BlockSpec((D_MODEL, tn), lambda j: (0, j)),
            pl.BlockSpec((1, tn), lambda j: (0, j)),
        ],
        out_specs=pl.BlockSpec((bsz, tn), lambda j: (0, j)),
        compiler_params=pltpu.CompilerParams(
            dimension_semantics=("arbitrary",), vmem_limit_bytes=VMEM_LIMIT),
        name="ada_mod",
    )(c, w_ada, b_ada.reshape(1, n))


def _sigmoid(x):
    return 0.5 * jnp.tanh(0.5 * x) + 0.5


def _head_rms(x, g, scale):
    r = lax.rsqrt(jnp.mean(x * x, axis=-1, keepdims=True) + EPS)
    y = x * r * g
    return y if scale is None else y * scale


def _proj_kernel(x_ref, sc_ref, sh_ref, g_ref, w_ref, vg_ref, qg_ref, kg_ref, o_ref, h_ref):
    j = pl.program_id(1)

    def acc():
        return _dot(h_ref[...], w_ref[...])

    @pl.when(j == 0)
    def _():
        x = x_ref[...]
        r = lax.rsqrt(jnp.mean(x * x, axis=-1, keepdims=True) + EPS)
        h = ((x * r * g_ref[...]) * (1.0 + sc_ref[...]) + sh_ref[...]).astype(BF16)
        h_ref[...] = h
        o_ref[...] = _sigmoid(_dot(h, w_ref[...])).astype(BF16)

    @pl.when((j > 0) & (j < COL_U // PROJ_TN))
    def _():
        o_ref[...] = _sigmoid(acc()).astype(BF16)

    @pl.when(j == COL_U // PROJ_TN)
    def _():
        o_ref[...] = jax.nn.gelu(acc()).astype(BF16)

    @pl.when(j == COL_VA // PROJ_TN)
    def _():
        v = jax.nn.gelu(acc())
        r = lax.rsqrt(jnp.mean(v * v, axis=-1, keepdims=True) + EPS)
        o_ref[...] = (v * r * vg_ref[...]).astype(BF16)

    @pl.when(j == COL_Q // PROJ_TN)
    def _():
        qscale = (HEAD_DIM ** -0.5) * math.log2(math.e)
        a = acc()
        for h in range(N_HEADS):
            sl = slice(h * HEAD_DIM, (h + 1) * HEAD_DIM)
            o_ref[:, sl] = _head_rms(a[:, sl], qg_ref[...], qscale).astype(BF16)

    @pl.when(j == COL_MISC // PROJ_TN)
    def _():
        a = acc()
        for h in range(N_KV):
            sl = slice(MISC_K + h * HEAD_DIM, MISC_K + (h + 1) * HEAD_DIM)
            o_ref[:, sl] = _head_rms(a[:, sl], kg_ref[...], None).astype(BF16)
        o_ref[:, MISC_V:] = a[:, MISC_V:].astype(BF16)

    @pl.when(j == COL_QI // PROJ_TN)
    def _():
        o_ref[...] = acc().astype(BF16)


def _proj(x2, sc1, sh1, norm1_g, w_cat, v_norm_g, q_norm_g, k_norm_g, seq):
    n = x2.shape[0]
    tm = 1024
    tiles_per_batch = seq // tm
    mod_spec = pl.BlockSpec((None, 1, D_MODEL), lambda i, j: (i // tiles_per_batch, 0, 0))
    return pl.pallas_call(
        _proj_kernel,
        out_shape=jax.ShapeDtypeStruct((n, PROJ_COLS), BF16),
        grid=(n // tm, PROJ_COLS // PROJ_TN),
        in_specs=[
            pl.BlockSpec((tm, D_MODEL), lambda i, j: (i, 0)),
            mod_spec, mod_spec,
            pl.BlockSpec((1, D_MODEL), lambda i, j: (0, 0)),
            pl.BlockSpec((D_MODEL, PROJ_TN), lambda i, j: (0, j)),
            pl.BlockSpec((1, A_WIDTH), lambda i, j: (0, 0)),
            pl.BlockSpec((1, HEAD_DIM), lambda i, j: (0, 0)),
            pl.BlockSpec((1, HEAD_DIM), lambda i, j: (0, 0)),
        ],
        out_specs=pl.BlockSpec((tm, PROJ_TN), lambda i, j: (i, j)),
        scratch_shapes=[pltpu.VMEM((tm, D_MODEL), BF16)],
        compiler_params=pltpu.CompilerParams(
            dimension_semantics=("parallel", "arbitrary"), vmem_limit_bytes=VMEM_LIMIT),
        name="proj_in",
    )(x2, sc1, sh1, norm1_g.reshape(1, D_MODEL), w_cat, v_norm_g.reshape(1, A_WIDTH),
      q_norm_g.reshape(1, HEAD_DIM), k_norm_g.reshape(1, HEAD_DIM))


def _sublane_tree(x, op):
    r, c = x.shape
    x = x.reshape(r // 64, 8, 8, c)
    y = x[0]
    for t in range(1, r // 64):
        y = op(y, x[t])
    z = op(op(y[0], y[1]), op(y[2], y[3]))
    return op(z, op(op(y[4], y[5]), op(y[6], y[7])))


def _attn_block(s_eff, i, shifts, fast, q_ref, misc_ref, wi_ref, qi_ref, o_ref, vt_ref, score_ref, bias_ref, p_ref,
                topk):
    nch = s_eff // ATT_CH

    idx_scale = IDX_DIM ** -0.5 * IDX_HEADS ** -0.5
    w_t = wi_ref[...].astype(F32).T * idx_scale
    rhs = [jnp.concatenate([qi_ref[:, 256 * r:256 * r + 128], qi_ref[:, 256 * r + 128:256 * r + 256]], axis=0)
           for r in range(IDX_HEADS // 4)]
    q_pos = i * ATT_T + lax.broadcasted_iota(jnp.int32, (ATT_CH, ATT_T), 1)
    row_iota = lax.broadcasted_iota(jnp.int32, (ATT_CH, ATT_T), 0)

    def index_chunk(c, carry):
        r0 = pl.multiple_of(c * ATT_CH, ATT_CH)
        rows = pl.ds(r0, ATT_CH)
        k_even = misc_ref[rows, MISC_KI_EVEN:MISC_KI_EVEN + LANES]
        k_odd = misc_ref[rows, MISC_KI_ODD:MISC_KI_ODD + LANES]
        acc = [None, None]
        for r in range(IDX_HEADS // 4):
            l_even = _dot_nt(k_even, rhs[r])
            l_odd = _dot_nt(k_odd, rhs[r])
            terms = (w_t[4 * r:4 * r + 1] * jnp.maximum(l_even[:, :ATT_T], 0.0),
                     w_t[4 * r + 1:4 * r + 2] * jnp.maximum(l_odd[:, :ATT_T], 0.0),
                     w_t[4 * r + 2:4 * r + 3] * jnp.maximum(l_even[:, ATT_T:], 0.0),
                     w_t[4 * r + 3:4 * r + 4] * jnp.maximum(l_odd[:, ATT_T:], 0.0))
            for t, term in enumerate(terms):
                acc[t % 2] = term if acc[t % 2] is None else acc[t % 2] + term
        score = jnp.where(r0 + row_iota <= q_pos, acc[0] + acc[1], -jnp.inf)
        score_ref[rows, :] = score
        return carry

    lax.fori_loop(0, nch, index_chunk, 0)

    def count_ge(thr_f):
        tot = None
        for c in range(nch):
            ones = jnp.where(score_ref[c * ATT_CH:(c + 1) * ATT_CH, :] >= thr_f, 1.0, 0.0)
            part = _sublane_tree(ones, jnp.add)
            tot = part if tot is None else tot + part
        return jnp.sum(tot, axis=0, keepdims=True)

    def code_to_float(code):
        key = code ^ INT_MIN
        return lax.bitcast_convert_type(key ^ ((key >> 31) & 0x7FFFFFFF), F32)

    def bs_step(cand, cand_f, code, n_ge):
        cnt = count_ge(cand_f)
        ok = cnt >= float(topk)
        return jnp.where(ok, cand, code), jnp.where(ok, cnt, n_ge)

    top = jnp.full((1, ATT_T), INT_MIN, jnp.int32)
    code, n_ge = bs_step(top, code_to_float(top), jnp.zeros((1, ATT_T), jnp.int32),
                         jnp.full((1, ATT_T), float(s_eff), F32))
    to_bits = jnp.where(code < 0, INT_MIN, -1)

    def bs_body(it, carry):
        code, n_ge = carry
        cand = code | lax.shift_left(jnp.int32(1), 31 - it)
        return bs_step(cand, lax.bitcast_convert_type(cand ^ to_bits, F32), code, n_ge)

    code, n_ge = lax.fori_loop(1, 32, bs_body, (code, n_ge))
    thr_f = code_to_float(jnp.maximum(code ^ INT_MIN, KEY_NEG_INF + 1) ^ INT_MIN)
    excess = jnp.max(jnp.where(thr_f > MOST_NEG_F32, n_ge - float(topk), 0.0))

    for c in range(nch):
        rows = slice(c * ATT_CH, (c + 1) * ATT_CH)
        bias_ref[rows, :] = jnp.where(score_ref[rows, :] >= thr_f, 0.0, NEG_BIG)

    @pl.when(excess > 0.0)
    def _():
        n_tie = None
        for c in range(nch):
            ones = jnp.where(score_ref[c * ATT_CH:(c + 1) * ATT_CH, :] == thr_f, 1.0, 0.0)
            part = _sublane_tree(ones, jnp.add)
            n_tie = part if n_tie is None else n_tie + part
        n_tie = jnp.sum(n_tie, axis=0, keepdims=True)
        need = float(topk) - (n_ge - n_tie)
        lower = (lax.broadcasted_iota(jnp.int32, (ATT_CH, ATT_CH), 1)
                 <= lax.broadcasted_iota(jnp.int32, (ATT_CH, ATT_CH), 0))
        tril = jnp.where(lower, 1.0, 0.0).astype(BF16)
        run = jnp.zeros((1, ATT_T), F32)
        for c in range(nch):
            rows = slice(c * ATT_CH, (c + 1) * ATT_CH)
            sc = score_ref[rows, :]
            tie = jnp.where(sc == thr_f, 1.0, 0.0)
            rank = _dot(tril, tie.astype(BF16)) + run
            keep = (sc > thr_f) | ((sc == thr_f) & (rank <= need))
            bias_ref[rows, :] = jnp.where(keep, 0.0, NEG_BIG)
            run = run + jnp.sum(_sublane_tree(tie, jnp.add), axis=0, keepdims=True)

    @pl.when(fast)
    def _():
        pairs = [_head_pair(q_ref, p) for p in range(N_HEADS // 2)]
        def prob_chunk(c, carry):
            rows = pl.ds(pl.multiple_of(c * ATT_CH, ATT_CH), ATT_CH)
            bias = bias_ref[rows, :]
            bias2 = jnp.concatenate([bias, bias], axis=1)
            for p, (g, q_pair) in enumerate(pairs):
                s = _dot_nt(misc_ref[rows, MISC_K + g * HEAD_DIM:MISC_K + (g + 1) * HEAD_DIM], q_pair)
                p_ref[p, rows, :] = jnp.exp2(s + bias2 - shifts[p]).astype(BF16)
            return carry

        lax.fori_loop(0, nch, prob_chunk, 0)
        outs =[_dot(vt_ref[g, :, 0:s_eff], p_ref[p, 0:s_eff, :]) for p, (g, _) in enumerate(pairs)]
        for p, out in enumerate(outs):
            _store_heads(o_ref, p, out[:HEAD_DIM] / out[HEAD_DIM:HEAD_DIM + 1])


def _head_pair(q_ref, p):
    g = (2 * p) // (N_HEADS // N_KV)
    return g, jnp.concatenate([q_ref[:, 256 * p:256 * p + 128], q_ref[:, 256 * p + 128:256 * p + 256]], axis=0)


def _store_heads(o_ref, p, out):
    o_ref[:, 256 * p:256 * p + 128] = out[:, :ATT_T].T.astype(BF16)
    o_ref[:, 256 * p + 128:256 * p + 256] = out[:, ATT_T:].T.astype(BF16)


def _softmax_shifts(q_ref, kmax_ref):
    ones_rows = jnp.ones((8, HEAD_DIM), BF16)
    pairs = [_head_pair(q_ref, p) for p in range(N_HEADS // 2)]
    squares = [jnp.square(q_pair.astype(F32)).astype(BF16) for _, q_pair in pairs]
    norms2 = [_dot_nt(ones_rows, sq)[0:1] for sq in squares]
    shifts = []
    for (g, _), qn2 in zip(pairs, norms2):
        kmax = kmax_ref[g, 0:1, :]
        kmax = jnp.concatenate([kmax] * (2 * ATT_T // LANES), axis=1)
        shifts.append(jnp.sqrt(qn2) * (kmax * ATT_BOUND_SLACK))
    return shifts, jnp.max(jnp.concatenate(shifts, axis=0)) <= ATT_SAFE_SHIFT


def _attn_exact_max(nch, q_ref, misc_ref, o_ref, vt_ref, bias_ref, s_ref, p_ref, seq):
    def zero_tail(c, carry):
        p_ref[0, pl.ds(pl.multiple_of(c * ATT_CH, ATT_CH), ATT_CH), :] = jnp.zeros((ATT_CH, 2 * ATT_T), BF16)
        return carry

    lax.fori_loop(nch, seq // ATT_CH, zero_tail, 0)
    for p in range(N_HEADS // 2):
        g, q_pair = _head_pair(q_ref, p)

        def logits(c, mx, g=g, q_pair=q_pair):
            rows = pl.ds(pl.multiple_of(c * ATT_CH, ATT_CH), ATT_CH)
            bias = bias_ref[rows, :]
            s = _dot_nt(misc_ref[rows, MISC_K + g * HEAD_DIM:MISC_K + (g + 1) * HEAD_DIM], q_pair)
            s = s + jnp.concatenate([bias, bias], axis=1)
            s_ref[rows, :] = s
            return jnp.maximum(mx, _sublane_tree(s, jnp.maximum))

        mx = lax.fori_loop(0, nch, logits, jnp.full((8, 2 * ATT_T), -jnp.inf, F32))
        m = jnp.max(mx, axis=0, keepdims=True)

        def probs(c, ls, m=m):
            rows = pl.ds(pl.multiple_of(c * ATT_CH, ATT_CH), ATT_CH)
            pm = jnp.exp2(s_ref[rows, :] - m)
            p_ref[0, rows, :] = pm.astype(BF16)
            return ls + _sublane_tree(pm, jnp.add)

        ls = lax.fori_loop(0, nch, probs, jnp.zeros((8, 2 * ATT_T), F32))
        out = _dot(vt_ref[g], p_ref[0])
        _store_heads(o_ref, p, out[:HEAD_DIM] / jnp.sum(ls, axis=0, keepdims=True))


def _attn_kernel(q_ref, misc_ref, wi_ref, qi_ref, o_ref, vt_ref, kmax_ref, score_ref, bias_ref, s_ref, p_ref,
                 *, seq, topk):
    i = pl.program_id(1)

    @pl.when(i == 0)
    def _():
        for g in range(N_KV):
            for c in range(seq // ATT_CH):
                rows = slice(c * ATT_CH, (c + 1) * ATT_CH)
                vg = misc_ref[rows, MISC_V + g * HEAD_DIM:MISC_V + (g + 1) * HEAD_DIM]
                vt_ref[g, 0:HEAD_DIM, rows] = vg.astype(F32).T.astype(BF16)
            vt_ref[g, HEAD_DIM:, :] = jnp.ones((ATT_ONES_ROWS, seq), BF16)
            kg = misc_ref[:, MISC_K + g * HEAD_DIM:MISC_K + (g + 1) * HEAD_DIM].astype(F32)
            k_norm2 = jnp.max(jnp.sum(kg * kg, axis=1, keepdims=True))
            kmax_ref[g] = jnp.full((8, LANES), jnp.sqrt(k_norm2), F32)

    n_span = (i * ATT_T + ATT_T + ATT_SPAN - 1) // ATT_SPAN
    shifts, fast = _softmax_shifts(q_ref, kmax_ref)
    for ns in range(1, seq // ATT_SPAN + 1):
        @pl.when(n_span == ns)
        def _(ns=ns):
            _attn_block(ns * ATT_SPAN, i, shifts, fast, q_ref, misc_ref, wi_ref, qi_ref, o_ref, vt_ref, score_ref,
                        bias_ref, p_ref, topk)

    @pl.when(jnp.logical_not(fast))
    def _():
        _attn_exact_max(n_span * (ATT_SPAN // ATT_CH), q_ref, misc_ref, o_ref, vt_ref, bias_ref, s_ref, p_ref, seq)


def _attention(proj, bsz, seq):
    n = proj.shape[0]
    nblk = seq // ATT_T
    topk = min(TOPK_MAX, seq // 4)
    kern = functools.partial(_attn_kernel, seq=seq, topk=topk)
    return pl.pallas_call(
        kern,
        out_shape=jax.ShapeDtypeStruct((n, Q_WIDTH), BF16),
        grid=(bsz, nblk),
        in_specs=[
            pl.BlockSpec((ATT_T, Q_WIDTH), lambda b, i: (b * nblk + i, COL_Q // Q_WIDTH)),
            pl.BlockSpec((seq, PROJ_TN), lambda b, i: (b, COL_MISC // PROJ_TN)),
            pl.BlockSpec((ATT_T, LANES), lambda b, i: (b * nblk + i, (COL_MISC + MISC_WI) // LANES)),
            pl.BlockSpec((ATT_T, IDX_HEADS * IDX_DIM), lambda b, i: (b * nblk + i, COL_QI // (IDX_HEADS * IDX_DIM))),
        ],
        out_specs=pl.BlockSpec((ATT_T, Q_WIDTH), lambda b, i: (b * nblk + i, 0)),
        scratch_shapes=[
            pltpu.VMEM((N_KV, HEAD_DIM + ATT_ONES_ROWS, seq), BF16),
            pltpu.VMEM((N_KV, 8, LANES), F32),
            pltpu.VMEM((seq, ATT_T), F32),
            pltpu.VMEM((seq, ATT_T), F32),
            pltpu.VMEM((seq, 2 * ATT_T), F32),
            pltpu.VMEM((N_HEADS // 2, seq, 2 * ATT_T), BF16),
        ],
        compiler_params=pltpu.CompilerParams(
            dimension_semantics=("arbitrary", "arbitrary"), vmem_limit_bytes=VMEM_LIMIT),
        name="sparse_attn",
    )(proj, proj, proj, proj)


def _merge_kernel(ga_ref, gb_ref, u_ref, v_ref, yb_ref, x_ref, g1_ref, sc2_ref, sh2_ref, n2g_ref,
                  wsp_ref, bsp_ref, wpa_ref, wpb_ref, wout_ref, x1_ref, h2_ref, ya_ref, *, tm):
    row = lax.broadcasted_iota(jnp.int32, (CHUNK, CHUNK), 0)
    col = lax.broadcasted_iota(jnp.int32, (CHUNK, CHUNK), 1)
    causal = col <= row
    bsp = bsp_ref[...]
    for g in range(A_GROUPS):
        wm = jnp.where(causal, wsp_ref[g], jnp.zeros((), BF16))
        gcols = slice(g * A_GROUP_DIM, (g + 1) * A_GROUP_DIM)
        for ci in range(tm // CHUNK):
            rows = slice(ci * CHUNK, (ci + 1) * CHUNK)
            mixed = _dot(wm, v_ref[rows, gcols]) + bsp[:, g:g + 1]
            ya_ref[rows, gcols] = (u_ref[rows, gcols].astype(F32) * mixed).astype(BF16)

    a = _dot(ya_ref[...], wpa_ref[...])
    b = _dot(yb_ref[...], wpb_ref[...])
    merged = ga_ref[...].astype(F32) * a + gb_ref[...].astype(F32) * b
    o = _dot(merged.astype(BF16), wout_ref[...])
    x1 = x_ref[...] + g1_ref[...] * o
    x1_ref[...] = x1
    r = lax.rsqrt(jnp.mean(x1 * x1, axis=-1, keepdims=True) + EPS)
    h2 = (x1 * r * n2g_ref[...]) * (1.0 + sc2_ref[...]) + sh2_ref[...]
    h2_ref[...] = h2.astype(BF16)


def _merge(proj, yb, x2, g1, sc2, sh2, norm2_g, wsp, bsp_t, wpa, wpb, wout, seq):
    n = x2.shape[0]
    tm = 256
    tiles_per_batch = seq // tm
    mod_spec = pl.BlockSpec((None, 1, D_MODEL), lambda i: (i // tiles_per_batch, 0, 0))
    const2 = lambda i: (0, 0)
    kern = functools.partial(_merge_kernel, tm=tm)
    return pl.pallas_call(
        kern,
        out_shape=(jax.ShapeDtypeStruct((n, D_MODEL), F32), jax.ShapeDtypeStruct((n, D_MODEL), BF16)),
        grid=(n // tm,),
        in_specs=[
            pl.BlockSpec((tm, D_MODEL), lambda i: (i, COL_GA // D_MODEL)),
            pl.BlockSpec((tm, D_MODEL), lambda i: (i, COL_GB // D_MODEL)),
            pl.BlockSpec((tm, A_WIDTH), lambda i: (i, COL_U // A_WIDTH)),
            pl.BlockSpec((tm, A_WIDTH), lambda i: (i, COL_VA // A_WIDTH)),
            pl.BlockSpec((tm, Q_WIDTH), lambda i: (i, 0)),
            pl.BlockSpec((tm, D_MODEL), lambda i: (i, 0)),
            mod_spec, mod_spec, mod_spec,
            pl.BlockSpec((1, D_MODEL), const2),
            pl.BlockSpec((A_GROUPS, CHUNK, CHUNK), lambda i: (0, 0, 0)),
            pl.BlockSpec((CHUNK, A_GROUPS), const2),
            pl.BlockSpec((A_WIDTH, D_MODEL), const2, pipeline_mode=pl.Buffered(1)),
            pl.BlockSpec((Q_WIDTH, D_MODEL), const2, pipeline_mode=pl.Buffered(1)),
            pl.BlockSpec((D_MODEL, D_MODEL), const2, pipeline_mode=pl.Buffered(1)),
        ],
        out_specs=(pl.BlockSpec((tm, D_MODEL), lambda i: (i, 0)),
                   pl.BlockSpec((tm, D_MODEL), lambda i: (i, 0))),
        scratch_shapes=[pltpu.VMEM((tm, A_WIDTH), BF16)],
        compiler_params=pltpu.CompilerParams(
            dimension_semantics=("parallel",), vmem_limit_bytes=VMEM_LIMIT),
        name="merge_out",
    )(proj, proj, proj, proj, yb, x2, g1, sc2, sh2, norm2_g.reshape(1, D_MODEL), wsp, bsp_t, wpa, wpb, wout)


FFN_TM = 512
FFN_TF = 512
FFN_HALO = 8


def _ffn_kernel(h_ref, wa_ref, wb_ref, cwa_ref, cwb_ref, cba_ref, cbb_ref, wd_ref, x1_ref, g2_ref,
                o_ref, up_ref, tail_ref, acc_ref, *, tiles_per_batch):
    i = pl.program_id(0)
    f = pl.program_id(1)
    first = (i % tiles_per_batch) == 0

    @pl.when(first)
    def _():
        up_ref[:FFN_HALO] = jnp.zeros((FFN_HALO, 2 * FFN_TF), F32)

    @pl.when(jnp.logical_not(first))
    def _():
        up_ref[:FFN_HALO] = tail_ref[f]

    @pl.when(f == 0)
    def _():
        acc_ref[...] = jnp.zeros_like(acc_ref)

    h = h_ref[...]
    up_ref[FFN_HALO:, :FFN_TF] = _dot(h, wa_ref[...])
    up_ref[FFN_HALO:, FFN_TF:] = _dot(h, wb_ref[...])

    def conv(cols, cw_ref, cb_ref):
        y = cb_ref[...] + cw_ref[0:1] * up_ref[FFN_HALO - 2:FFN_HALO - 2 + FFN_TM, cols]
        y = y + cw_ref[1:2] * up_ref[FFN_HALO - 1:FFN_HALO - 1 + FFN_TM, cols]
        return y + cw_ref[2:3] * up_ref[FFN_HALO:FFN_HALO + FFN_TM, cols]

    a = conv(slice(0, FFN_TF), cwa_ref, cba_ref)
    b = conv(slice(FFN_TF, 2 * FFN_TF), cwb_ref, cbb_ref)
    gated = (jax.nn.silu(a) * b).astype(BF16)
    acc_ref[...] += _dot(gated, wd_ref[...])
    tail_ref[f] = up_ref[FFN_TM:FFN_TM + FFN_HALO]

    @pl.when(f == pl.num_programs(1) - 1)
    def _():
        o_ref[...] = x1_ref[...] + g2_ref[...] * acc_ref[...]


def _ffn(h2, x1, g2, w_up, conv_w, conv_b, w_down, seq):
    n = h2.shape[0]
    tm, tf = FFN_TM, FFN_TF
    nf = D_FF // tf
    tiles_per_batch = seq // tm
    kern = functools.partial(_ffn_kernel, tiles_per_batch=tiles_per_batch)
    return pl.pallas_call(
        kern,
        out_shape=jax.ShapeDtypeStruct((n, D_MODEL), F32),
        grid=(n // tm, nf),
        in_specs=[
            pl.BlockSpec((tm, D_MODEL), lambda i, f: (i, 0)),
            pl.BlockSpec((D_MODEL, tf), lambda i, f: (0, f)),
            pl.BlockSpec((D_MODEL, tf), lambda i, f: (0, f + nf)),
            pl.BlockSpec((CONV_W, tf), lambda i, f: (0, f)),
            pl.BlockSpec((CONV_W, tf), lambda i, f: (0, f + nf)),
            pl.BlockSpec((1, tf), lambda i, f: (0, f)),
            pl.BlockSpec((1, tf), lambda i, f: (0, f + nf)),
            pl.BlockSpec((tf, D_MODEL), lambda i, f: (f, 0)),
            pl.BlockSpec((tm, D_MODEL), lambda i, f: (i, 0)),
            pl.BlockSpec((None, 1, D_MODEL), lambda i, f: (i // tiles_per_batch, 0, 0)),
        ],
        out_specs=pl.BlockSpec((tm, D_MODEL), lambda i, f: (i, 0)),
        scratch_shapes=[
            pltpu.VMEM((FFN_HALO + tm, 2 * tf), F32),
            pltpu.VMEM((nf, FFN_HALO, 2 * tf), F32),
            pltpu.VMEM((tm, D_MODEL), F32),
        ],
        compiler_params=pltpu.CompilerParams(
            dimension_semantics=("arbitrary", "arbitrary"), vmem_limit_bytes=VMEM_LIMIT),
        name="conv_ffn",
    )(h2, w_up, w_up, conv_w, conv_w, conv_b.reshape(1, 2 * D_FF), conv_b.reshape(1, 2 * D_FF), w_down, x1, g2)


def _fused_in_weight(w_in):
    sizes = [A_WIDTH, A_WIDTH, Q_WIDTH, KV_WIDTH, KV_WIDTH, IDX_HEADS * IDX_DIM, IDX_DIM, IDX_HEADS, D_MODEL, D_MODEL]
    offs = [0]
    for s in sizes:
        offs.append(offs[-1] + s)
    w16 = w_in.astype(BF16)
    u, va, q, k, vb, qi, ki, wi, ga, gb = [w16[:, offs[t]:offs[t + 1]] for t in range(len(sizes))]
    z = lambda ncol: jnp.zeros((D_MODEL, ncol), BF16)
    misc = jnp.concatenate([k, vb, ki, z(IDX_DIM), z(IDX_DIM), ki, wi, z(LANES - IDX_HEADS), z(LANES)], axis=1)
    w_cat = jnp.concatenate([ga, gb, u, va, q, misc, qi], axis=1)
    assert w_cat.shape[1] == PROJ_COLS and misc.shape[1] == PROJ_TN
    return w_cat


def kernel(x, c, w_ada, b_ada, norm1_g, w_in, v_norm_g, w_spatial, b_spatial, q_norm_g, k_norm_g, w_proj_a,
           w_proj_b, w_out, norm2_g, w_up, conv_w, conv_b, w_down):
    bsz, seq, _ = x.shape
    n = bsz * seq
    x2 = x.reshape(n, D_MODEL)

    mod = _ada(c, w_ada, b_ada)
    sh1, sc1, g1, sh2, sc2, g2 = [mod[:, t * D_MODEL:(t + 1) * D_MODEL].reshape(bsz, 1, D_MODEL) for t in range(6)]

    proj = _proj(x2, sc1, sh1, norm1_g, _fused_in_weight(w_in), v_norm_g, q_norm_g, k_norm_g, seq)
    yb = _attention(proj, bsz, seq)
    x1, h2 = _merge(proj, yb, x2, g1, sc2, sh2, norm2_g, w_spatial.astype(BF16), b_spatial.T,
                    w_proj_a.astype(BF16), w_proj_b.astype(BF16), w_out.astype(BF16), seq)
    out = _ffn(h2, x1, g2, w_up.astype(BF16), conv_w, conv_b, w_down.astype(BF16), seq)
    return out.reshape(bsz, seq, D_MODEL)
```
